```python
import jax, jax.numpy as jnp
from jax import lax
import numpy as np

D_MODEL = 2048
BATCH = 1
SEQ = 8192
DEPTH = 1
DEC_BATCH = 32
DEC_SEQ = 64
PAST_LEN = 1024

CHUNK = 64
H_RET = 8
DK_RET = D_MODEL // 16
DV_RET = D_MODEL // 16
H_ML = 8
DK_ML = D_MODEL // 32
DV_ML = D_MODEL // 16
RET_QK = H_RET * DK_RET
RET_V = H_RET * DV_RET
ML_QK = H_ML * DK_ML
ML_V = H_ML * DV_ML
IN_SPLITS = (RET_QK, RET_QK, RET_V, RET_V, ML_QK, ML_QK, ML_V, ML_V, H_ML, H_ML, D_MODEL, D_MODEL)
IN_COLS = 2 * RET_QK + 2 * RET_V + 2 * ML_QK + 2 * ML_V + 2 * H_ML + 2 * D_MODEL
N_EXPERTS = 32
TOP_K = 4
D_FF = D_MODEL
SWIGLU_ALPHA = 1.702
SWIGLU_LIMIT = 7.0
ROPE_BASE = 10000.0
NORM_EPS = 1e-6

kernel_name = 'hybrid_retention_mlstm_moe_stream_step'


def rmsnorm(x, g):
    xf = x.astype(jnp.float32)
    xf = xf * lax.rsqrt(jnp.mean(xf * xf, axis=-1, keepdims=True) + NORM_EPS)
    return (xf * g.astype(jnp.float32)).astype(x.dtype)


def head_norm(h, g):
    mu = jnp.mean(h, axis=-1, keepdims=True)
    var = jnp.mean(jnp.square(h - mu), axis=-1, keepdims=True)
    h = (h - mu) * lax.rsqrt(var + NORM_EPS)
    return h.reshape(h.shape[0], h.shape[1], -1) * g.astype(jnp.float32)


def rotary(x, pos):
    half = x.shape[-1] // 2
    inv = ROPE_BASE ** (-jnp.arange(half, dtype=jnp.float32) / half)
    ang = pos.astype(jnp.float32)[:, None] * inv[None, :]
    cos = jnp.cos(ang)[None, :, None, :]
    sin = jnp.sin(ang)[None, :, None, :]
    x1, x2 = x[..., :half], x[..., half:]
    return jnp.concatenate([x1 * cos - x2 * sin, x1 * sin + x2 * cos], axis=-1)


def split_cols(z):
    idx = np.cumsum(IN_SPLITS)[:-1].tolist()
    return jnp.split(z, idx, axis=-1)


def to_chunks(a, L):
    B, H, T = a.shape[:3]
    a = a.reshape(B, H, T // L, L, *a.shape[3:])
    return jnp.moveaxis(a, 2, 0)


def from_chunks(a):
    a = jnp.moveaxis(a, 0, 2)
    return a.reshape(a.shape[0], a.shape[1], -1, *a.shape[4:])


def retention(q, k, v, S0):
    B, H, T, dk = q.shape
    L = min(CHUNK, T)
    log_g = jnp.log(1.0 - 2.0 ** (-5.0 - jnp.arange(H, dtype=jnp.float32)))
    idx = jnp.arange(L, dtype=jnp.float32)
    rel = idx[:, None] - idx[None, :]
    decay_in = jnp.where(rel >= 0, jnp.exp(rel[None] * log_g[:, None, None]), 0.0)
    q_dec = jnp.exp((idx[None, :] + 1.0) * log_g[:, None])[..., None]
    k_dec = jnp.exp((L - 1.0 - idx[None, :]) * log_g[:, None])[..., None]
    chunk_dec = jnp.exp(L * log_g)[:, None, None]
    k = k * dk ** -0.5

    def step(S, inp):
        qc, kc, vc = inp
        att = jnp.einsum('bhid,bhjd->bhij', qc, kc) * decay_in
        o = jnp.einsum('bhij,bhje->bhie', att, vc) + jnp.einsum('bhid,bhde->bhie', qc * q_dec, S)
        S = chunk_dec * S + jnp.einsum('bhjd,bhje->bhde', kc * k_dec, vc)
        return S, o

    S, o = lax.scan(step, S0, (to_chunks(q, L), to_chunks(k, L), to_chunks(v, L)))
    return from_chunks(o), S


def mlstm(q, k, v, ig, lf, C0, n0, m0):
    B, H, T, dk = q.shape
    L = min(CHUNK, T)
    q = q * dk ** -0.5
    causal = jnp.tril(jnp.ones((L, L), dtype=bool))

    def step(carry, inp):
        C, n, m = carry
        qc, kc, vc, ic, fc = inp
        b = jnp.cumsum(fc, axis=-1)
        logw = jnp.where(causal, b[..., :, None] - b[..., None, :] + ic[..., None, :], -jnp.inf)
        inter = b + m[..., None]
        m_row = jnp.maximum(inter, jnp.max(logw, axis=-1))
        w = jnp.exp(logw - m_row[..., None]) * jnp.einsum('bhid,bhjd->bhij', qc, kc)
        w_inter = jnp.exp(inter - m_row)
        num = jnp.einsum('bhij,bhje->bhie', w, vc) + w_inter[..., None] * jnp.einsum('bhid,bhed->bhie', qc, C)
        den = jnp.sum(w, axis=-1) + w_inter * jnp.einsum('bhid,bhd->bhi', qc, n)
        h = num / jnp.maximum(jnp.abs(den), jnp.exp(-m_row))[..., None]
        m_new = m_row[..., -1]
        decay = jnp.exp(b[..., -1] + m - m_new)
        wk = jnp.exp(b[..., -1:] - b + ic - m_new[..., None])
        C = decay[..., None, None] * C + jnp.einsum('bhje,bhjd->bhed', vc * wk[..., None], kc)
        n = decay[..., None] * n + jnp.einsum('bhj,bhjd->bhd', wk, kc)
        return (C, n, m_new), h

    (C, n, m), h = lax.scan(step, (C0, n0, m0),
                            (to_chunks(q, L), to_chunks(k, L), to_chunks(v, L), to_chunks(ig, L), to_chunks(lf, L)))
    return from_chunks(h), C, n, m


def mixer_block(x, pos, S_ret, C_ml, n_ml, m_ml, g_norm1, w_in, b_if, g_ret_gn, g_ml_gn, w_ret_br, w_ml_br, w_out):
    B, T, _ = x.shape
    f32 = jnp.float32
    u = rmsnorm(x, g_norm1)
    z = u @ w_in
    rq, rk, rv, rg, mq, mk, mv, mo, mi, mf, gate_a, gate_b = split_cols(z)

    def heads(a, h):
        return a.astype(f32).reshape(B, T, h, -1)

    def bhtd(a):
        return jnp.swapaxes(a, 1, 2)

    rq = rotary(heads(rq, H_RET), pos)
    rk = rotary(heads(rk, H_RET), pos)
    ret, S_new = retention(bhtd(rq), bhtd(rk), bhtd(heads(rv, H_RET)), S_ret.astype(f32))
    y_a = head_norm(bhtd(ret), g_ret_gn) * jax.nn.silu(rg.astype(f32))

    ig = bhtd((mi + b_if[0]).astype(f32))
    lf = jax.nn.log_sigmoid(bhtd((mf + b_if[1]).astype(f32)))
    ml, C_new, n_new, m_new = mlstm(bhtd(heads(mq, H_ML)), bhtd(heads(mk, H_ML)), bhtd(heads(mv, H_ML)),
                                    ig, lf, C_ml.astype(f32), n_ml.astype(f32), m_ml.astype(f32))
    y_b = jax.nn.sigmoid(mo.astype(f32)) * head_norm(bhtd(ml), g_ml_gn)

    br_a = y_a.astype(x.dtype) @ w_ret_br
    br_b = y_b.astype(x.dtype) @ w_ml_br
    merged = (jax.nn.sigmoid(gate_a.astype(f32)) * br_a.astype(f32)
              + jax.nn.sigmoid(gate_b.astype(f32)) * br_b.astype(f32))
    out = x + merged.astype(x.dtype) @ w_out
    return out, S_new, C_new, n_new, m_new


def moe(u, w_router, b_router, w_gate, b_gate, w_up, b_up, w_down, b_down):
    logits = (u @ w_router).astype(jnp.float32) + b_router.astype(jnp.float32)
    top_v, top_i = lax.top_k(logits, TOP_K)
    probs = jax.nn.softmax(top_v, axis=-1)
    combine = jnp.sum(jax.nn.one_hot(top_i, N_EXPERTS, dtype=jnp.float32) * probs[..., None], axis=1)
    out = jnp.zeros((u.shape[0], u.shape[1]), jnp.float32)
    for e in range(N_EXPERTS):
        g = jnp.minimum((u @ w_gate[e] + b_gate[e]).astype(jnp.float32), SWIGLU_LIMIT)
        up = jnp.clip((u @ w_up[e] + b_up[e]).astype(jnp.float32), -SWIGLU_LIMIT, SWIGLU_LIMIT)
        act = (g * jax.nn.sigmoid(SWIGLU_ALPHA * g) * (up + 1.0)).astype(u.dtype)
        out = out + combine[:, e:e + 1] * (act @ w_down[e] + b_down[e]).astype(jnp.float32)
    return out


def setup_inputs(seed: int = 0) -> dict:
    key = jax.random.key(seed)
    ks = jax.random.split(key, 24)

    def nrm(k, shape, scale):
        return jax.random.normal(k, shape, jnp.float32) * scale

    f_bias = jnp.linspace(3.0, 6.0, H_ML, dtype=jnp.float32)
    gate_bias = jnp.stack([jnp.zeros((H_ML,), jnp.float32), f_bias])[None]
    return {
        'x_prompt': nrm(ks[0], (BATCH, SEQ, D_MODEL), 1.0),
        'x_sample': nrm(ks[1], (DEC_BATCH, DEC_SEQ, D_MODEL), 1.0),
        'state_ret_S': nrm(ks[2], (DEPTH, DEC_BATCH, H_RET, DK_RET, DV_RET), 0.5),
        'state_mlstm_C': nrm(ks[3], (DEPTH, DEC_BATCH, H_ML, DV_ML, DK_ML), 1.0),
        'state_mlstm_n': nrm(ks[4], (DEPTH, DEC_BATCH, H_ML, DK_ML), 1.0),
        'state_mlstm_m': nrm(ks[5], (DEPTH, DEC_BATCH, H_ML), 1.0),
        'g_norm1': 1.0 + nrm(ks[6], (DEPTH, D_MODEL), 0.02),
        'w_in': nrm(ks[7], (DEPTH, D_MODEL, IN_COLS), D_MODEL ** -0.5),
        'b_if': gate_bias + nrm(ks[8], (DEPTH, 2, H_ML), 0.1),
        'g_ret_gn': 1.0 + nrm(ks[9], (DEPTH, RET_V), 0.02),
        'g_ml_gn': 1.0 + nrm(ks[10], (DEPTH, ML_V), 0.02),
        'w_ret_br': nrm(ks[11], (DEPTH, RET_V, D_MODEL), RET_V ** -0.5),
        'w_ml_br': nrm(ks[12], (DEPTH, ML_V, D_MODEL), ML_V ** -0.5),
        'w_out': nrm(ks[13], (DEPTH, D_MODEL, D_MODEL), D_MODEL ** -0.5),
        'g_norm2': 1.0 + nrm(ks[14], (DEPTH, D_MODEL), 0.02),
        'w_router': nrm(ks[15], (DEPTH, D_MODEL, N_EXPERTS), D_MODEL ** -0.5),
        'b_router': nrm(ks[16], (DEPTH, N_EXPERTS), 0.01),
        'w_gate': nrm(ks[17], (DEPTH, N_EXPERTS, D_MODEL, D_FF), D_MODEL ** -0.5),
        'b_gate': nrm(ks[18], (DEPTH, N_EXPERTS, D_FF), 0.01),
        'w_up': nrm(ks[19], (DEPTH, N_EXPERTS, D_MODEL, D_FF), D_MODEL ** -0.5),
        'b_up': nrm(ks[20], (DEPTH, N_EXPERTS, D_FF), 0.01),
        'w_down': nrm(ks[21], (DEPTH, N_EXPERTS, D_FF, D_MODEL), D_FF ** -0.5),
        'b_down': nrm(ks[22], (DEPTH, N_EXPERTS, D_MODEL), 0.01),
        'g_final': 1.0 + nrm(ks[23], (D_MODEL,), 0.02),
    }


def reference(x_prompt, x_sample, state_ret_S, state_mlstm_C, state_mlstm_n, state_mlstm_m,
              g_norm1, w_in, b_if, g_ret_gn, g_ml_gn, w_ret_br, w_ml_br, w_out,
              g_norm2, w_router, b_router, w_gate, b_gate, w_up, b_up, w_down, b_down, g_final):
    f32 = jnp.float32
    Bp, Tp, _ = x_prompt.shape
    Bs, Ts, _ = x_sample.shape
    pos_p = jnp.arange(Tp)
    pos_s = PAST_LEN + jnp.arange(Ts)
    hp, hs = x_prompt, x_sample
    n_prompt_tok = Bp * Tp
    rS_p, mC_p, mn_p, mm_p = [], [], [], []
    rS_s, mC_s, mn_s, mm_s = [], [], [], []
    for layer in range(DEPTH):
        hp, S1, C1, n1, m1 = mixer_block(
            hp, pos_p,
            jnp.zeros((Bp, H_RET, DK_RET, DV_RET), f32), jnp.zeros((Bp, H_ML, DV_ML, DK_ML), f32),
            jnp.zeros((Bp, H_ML, DK_ML), f32), jnp.zeros((Bp, H_ML), f32),
            g_norm1[layer], w_in[layer], b_if[layer], g_ret_gn[layer], g_ml_gn[layer],
            w_ret_br[layer], w_ml_br[layer], w_out[layer])
        hs, S2, C2, n2, m2 = mixer_block(
            hs, pos_s, state_ret_S[layer], state_mlstm_C[layer], state_mlstm_n[layer], state_mlstm_m[layer],
            g_norm1[layer], w_in[layer], b_if[layer], g_ret_gn[layer], g_ml_gn[layer],
            w_ret_br[layer], w_ml_br[layer], w_out[layer])
        tok = jnp.concatenate([hp.reshape(-1, D_MODEL), hs.reshape(-1, D_MODEL)], axis=0)
        tok = tok + moe(rmsnorm(tok, g_norm2[layer]), w_router[layer], b_router[layer], w_gate[layer],
                        b_gate[layer], w_up[layer], b_up[layer], w_down[layer], b_down[layer]).astype(tok.dtype)
        hp = tok[:n_prompt_tok].reshape(Bp, Tp, D_MODEL)
        hs = tok[n_prompt_tok:].reshape(Bs, Ts, D_MODEL)
        rS_p.append(S1); mC_p.append(C1); mn_p.append(n1); mm_p.append(m1)
        rS_s.append(S2); mC_s.append(C2); mn_s.append(n2); mm_s.append(m2)
    y_prompt = rmsnorm(hp, g_final)
    y_sample = rmsnorm(hs, g_final)
    return (y_prompt, y_sample,
            jnp.stack(rS_p), jnp.stack(mC_p), jnp.stack(mn_p), jnp.stack(mm_p),
            jnp.stack(rS_s), jnp.stack(mC_s), jnp.stack(mn_s), jnp.stack(mm_s))
```

```python
import functools

import jax
import jax.numpy as jnp
import numpy as np
from jax import lax
from jax.experimental import pallas as pl
from jax.experimental.pallas import tpu as pltpu

F32 = jnp.float32
BF16 = jnp.bfloat16
U32 = jnp.uint32
I32 = jnp.int32

N_HEADS = 8
DK_RET = 128
DV_RET = 128
DK_ML = 64
DV_ML = 128
TOP_K = 4
PAST_LEN = 1024
ROPE_BASE = 10000.0
NORM_EPS = 1e-6
SWIGLU_ALPHA = 1.702
SWIGLU_LIMIT = 7.0

LANES = 128
VMEM_LIMIT = 56 * 1024 * 1024

NT_DIMS = (((1,), (1,)), ((), ()))


def _cparams(sem):
    return pltpu.CompilerParams(dimension_semantics=sem, vmem_limit_bytes=VMEM_LIMIT)


def _sigmoid(x):
    return 1.0 / (1.0 + jnp.exp(-x))


def _log_sigmoid(x):
    return jnp.minimum(x, 0.0) - jnp.log(1.0 + jnp.exp(-jnp.abs(x)))


def _rms(x, g):
    return x * lax.rsqrt(jnp.mean(x * x, axis=-1, keepdims=True) + NORM_EPS) * g


def _head_norm(o, g):
    mu = jnp.mean(o, axis=-1, keepdims=True)
    d = o - mu
    var = jnp.mean(d * d, axis=-1, keepdims=True)
    return d * lax.rsqrt(var + NORM_EPS) * g


def _inproj_kernel(x_ref, g_ref, w_ref, wif_ref, wift_ref, bif_ref, bift_ref,
                   z_ref, gl_ref, glt_ref, u_scr):
    @pl.when(pl.program_id(1) == 0)
    def _():
        ub = _rms(x_ref[...], g_ref[...]).astype(BF16)
        u_scr[...] = ub
        gp = jnp.dot(ub, wif_ref[...], preferred_element_type=F32) + bif_ref[...]
        col = lax.broadcasted_iota(I32, gp.shape, 1)
        gl_ref[...] = jnp.where(col < N_HEADS, gp, _log_sigmoid(gp))
        gpt = lax.dot_general(wift_ref[...], ub, NT_DIMS, preferred_element_type=F32) + bift_ref[...]
        row = lax.broadcasted_iota(I32, gpt.shape, 0)
        glt_ref[...] = jnp.where(row < N_HEADS, gpt, _log_sigmoid(gpt))

    z_ref[...] = jnp.dot(u_scr[...], w_ref[...], preferred_element_type=F32)


def _inproj(x, g1, w_cat, wif, wift, bif, bift, *, tm, tn):
    n, d = x.shape
    c = w_cat.shape[1]
    return pl.pallas_call(
        _inproj_kernel,
        grid=(n // tm, c // tn),
        in_specs=[
            pl.BlockSpec((tm, d), lambda i, j: (i, 0)),
            pl.BlockSpec((1, d), lambda i, j: (0, 0)),
            pl.BlockSpec((d, tn), lambda i, j: (0, j)),
            pl.BlockSpec((d, LANES), lambda i, j: (0, 0)),
            pl.BlockSpec((2 * N_HEADS, d), lambda i, j: (0, 0)),
            pl.BlockSpec((1, LANES), lambda i, j: (0, 0)),
            pl.BlockSpec((2 * N_HEADS, 1), lambda i, j: (0, 0)),
        ],
        out_specs=[
            pl.BlockSpec((tm, tn), lambda i, j: (i, j)),
            pl.BlockSpec((tm, LANES), lambda i, j: (i, 0)),
            pl.BlockSpec((2 * N_HEADS, tm), lambda i, j: (0, i)),
        ],
        out_shape=[
            jax.ShapeDtypeStruct((n, c), F32),
            jax.ShapeDtypeStruct((n, LANES), F32),
            jax.ShapeDtypeStruct((2 * N_HEADS, n), F32),
        ],
        scratch_shapes=[pltpu.VMEM((tm, d), BF16)],
        compiler_params=_cparams(("parallel", "arbitrary")),
        name="inproj",
    )(x, g1, w_cat, wif, wift, bif, bift)


def _ret_kernel(cd_ref, q_ref, k_ref, v_ref, rg_ref, cos_ref, sin_ref, s0_ref, g_ref,
                din_ref, qd_ref, kd_ref, y_ref, sout_ref, s_scr):
    @pl.when(pl.program_id(1) == 0)
    def _():
        s_scr[...] = s0_ref[0]

    cos = cos_ref[...]
    sin = sin_ref[...]
    for h in range(N_HEADS):
        sl = slice(h * DK_RET, (h + 1) * DK_RET)
        q = q_ref[:, sl]
        k = k_ref[:, sl]
        q = q * cos + pltpu.roll(q, DK_RET // 2, 1) * sin
        k = (k * cos + pltpu.roll(k, DK_RET // 2, 1) * sin) * (DK_RET ** -0.5)
        v = v_ref[:, sl].astype(BF16)
        s = s_scr[h]
        att = lax.dot_general(q.astype(BF16), k.astype(BF16), NT_DIMS,
                              preferred_element_type=F32) * din_ref[h]
        o = (jnp.dot(att.astype(BF16), v, preferred_element_type=F32)
             + jnp.dot((q * qd_ref[h]).astype(BF16), s.astype(BF16), preferred_element_type=F32))
        s_new = cd_ref[h] * s + jnp.dot((k * kd_ref[h]).T.astype(BF16), v, preferred_element_type=F32)
        s_scr[h] = s_new
        sout_ref[0, h] = s_new
        rg = rg_ref[:, sl]
        y = _head_norm(o, g_ref[:, sl]) * (rg * _sigmoid(rg))
        y_ref[:, sl] = y.astype(BF16)


def _retention(z, row0, n_seq, t_len, L, cos, sin, s0, g_gn, tabs, colblk):
    din, qd, kd, cd = tabs
    nch = t_len // L
    rb0 = row0 // L
    w = N_HEADS * DK_RET
    zspec = lambda cb: pl.BlockSpec((L, w), lambda s, c: (rb0 + s * nch + c, cb))
    tspec = pl.BlockSpec((L, DK_RET), lambda s, c: (c, 0))
    const3 = lambda shape: pl.BlockSpec(shape, lambda s, c: (0, 0, 0))
    y, s_out = pl.pallas_call(
        _ret_kernel,
        grid=(n_seq, nch),
        in_specs=[
            pl.BlockSpec(memory_space=pltpu.SMEM),
            zspec(colblk["rq"]), zspec(colblk["rk"]), zspec(colblk["rv"]), zspec(colblk["rg"]),
            tspec, tspec,
            pl.BlockSpec((1, N_HEADS, DK_RET, DV_RET), lambda s, c: (s, 0, 0, 0)),
            pl.BlockSpec((1, w), lambda s, c: (0, 0)),
            const3((N_HEADS, L, L)), const3((N_HEADS, L, DK_RET)), const3((N_HEADS, L, DK_RET)),
        ],
        out_specs=[
            pl.BlockSpec((L, w), lambda s, c: (s * nch + c, 0)),
            pl.BlockSpec((1, N_HEADS, DK_RET, DV_RET), lambda s, c: (s, 0, 0, 0)),
        ],
        scratch_shapes=[pltpu.VMEM((N_HEADS, DK_RET, DV_RET), F32)],
        out_shape=[
            jax.ShapeDtypeStruct((n_seq * t_len, w), BF16),
            jax.ShapeDtypeStruct((n_seq, N_HEADS, DK_RET, DV_RET), F32),
        ],
        compiler_params=_cparams(("parallel", "arbitrary")),
        name="retention",
    )(cd, z, z, z, z, cos, sin, s0, g_gn, din, qd, kd)
    return y, s_out


def _retention_tables(L):
    log_g = np.log(1.0 - 2.0 ** (-5.0 - np.arange(N_HEADS, dtype=np.float64)))
    idx = np.arange(L, dtype=np.float64)
    rel = idx[:, None] - idx[None, :]
    din = np.where(rel >= 0, np.exp(rel[None] * log_g[:, None, None]), 0.0)
    qd = np.exp((idx[None, :] + 1.0) * log_g[:, None])[..., None] * np.ones((1, 1, DK_RET))
    kd = np.exp((L - 1.0 - idx[None, :]) * log_g[:, None])[..., None] * np.ones((1, 1, DK_RET))
    cd = np.exp(L * log_g)
    return (jnp.asarray(din, F32), jnp.asarray(qd, F32), jnp.asarray(kd, F32), jnp.asarray(cd, F32))


def _rotary_tables(pos):
    half = DK_RET // 2
    inv = ROPE_BASE ** (-jnp.arange(half, dtype=F32) / half)
    ang = pos.astype(F32)[:, None] * inv[None, :]
    c, s = jnp.cos(ang), jnp.sin(ang)
    return jnp.concatenate([c, c], axis=-1), jnp.concatenate([-s, s], axis=-1)


def _ml_kernel(q_ref, k_ref, v_ref, mo_ref, gl_ref, glt_ref, c0_ref, n0_ref, m0_ref, g_ref,
               y_ref, cout_ref, nout_ref, mout_ref, c_scr, n_scr, m_scr):
    @pl.when(pl.program_id(1) == 0)
    def _():
        c_scr[...] = c0_ref[0]
        n_scr[...] = n0_ref[0]
        m_scr[...] = m0_ref[0]

    L = q_ref.shape[0]
    ii = lax.broadcasted_iota(I32, (L, L), 0)
    jj = lax.broadcasted_iota(I32, (L, L), 1)
    causal = jj <= ii
    gl = gl_ref[...]
    glt = glt_ref[0]
    for h in range(N_HEADS):
        qs = slice(h * DK_ML, (h + 1) * DK_ML)
        vs = slice(h * DV_ML, (h + 1) * DV_ML)
        qf = q_ref[:, qs] * (DK_ML ** -0.5)
        kf = k_ref[:, qs]
        vf = v_ref[:, vs]
        q = qf.astype(BF16)
        k = kf.astype(BF16)
        ig_c = gl[:, h:h + 1]
        lf_c = gl[:, N_HEADS + h:N_HEADS + h + 1]
        ig_r = glt[h:h + 1, :]
        lf_r = glt[N_HEADS + h:N_HEADS + h + 1, :]
        b_c = jnp.sum(jnp.where(causal, jnp.broadcast_to(lf_r, (L, L)), 0.0), axis=1, keepdims=True)
        b_r = jnp.sum(jnp.where(ii <= jj, jnp.broadcast_to(lf_c, (L, L)), 0.0), axis=0, keepdims=True)
        logw = jnp.where(causal, b_c - b_r + ig_r, -jnp.inf)
        m_prev = m_scr[h:h + 1, 0:1]
        inter = b_c + m_prev
        m_row = jnp.maximum(inter, jnp.max(logw, axis=1, keepdims=True))
        qk = lax.dot_general(q, k, NT_DIMS, preferred_element_type=F32)
        w = jnp.exp(logw - m_row) * qk
        w_inter = jnp.exp(inter - m_row)
        c_old = c_scr[h]
        n_old = n_scr[h:h + 1, :]
        num = (jnp.dot(w.astype(BF16), vf.astype(BF16), preferred_element_type=F32)
               + w_inter * lax.dot_general(q, c_old.astype(BF16), NT_DIMS, preferred_element_type=F32))
        den = (jnp.sum(w, axis=1, keepdims=True)
               + w_inter * jnp.sum(qf * n_old, axis=1, keepdims=True))
        hh = num / jnp.maximum(jnp.abs(den), jnp.exp(-m_row))
        m_new = m_row[L - 1:L, :]
        b_last = b_c[L - 1:L, :]
        decay = jnp.exp(b_last + m_prev - m_new)
        wk_c = jnp.exp(b_last - b_c + ig_c - m_new)
        c_new = decay * c_old + jnp.dot((vf * wk_c).T.astype(BF16), k, preferred_element_type=F32)
        n_new = decay * n_old + jnp.sum(wk_c * kf, axis=0, keepdims=True)
        m_new_b = jnp.broadcast_to(m_new, (1, LANES))
        c_scr[h] = c_new
        n_scr[h:h + 1, :] = n_new
        m_scr[h:h + 1, :] = m_new_b
        cout_ref[0, h] = c_new
        nout_ref[0, h:h + 1, :] = n_new
        mout_ref[0, h:h + 1, :] = m_new_b
        y = _sigmoid(mo_ref[:, vs]) * _head_norm(hh, g_ref[:, vs])
        y_ref[:, vs] = y.astype(BF16)


def _mlstm(z, gl, glt, row0, n_seq, t_len, L, c0, n0, m0, g_gn, colblk):
    nch = t_len // L
    rb0 = row0 // L
    wqk = N_HEADS * DK_ML
    wv = N_HEADS * DV_ML
    qspec = lambda cb: pl.BlockSpec((L, wqk), lambda s, c: (rb0 + s * nch + c, cb))
    vspec = lambda cb: pl.BlockSpec((L, wv), lambda s, c: (rb0 + s * nch + c, cb))
    m0b = jnp.broadcast_to(m0[..., None], (n_seq, N_HEADS, LANES))
    glt = glt[:, row0:row0 + n_seq * t_len].reshape(2 * N_HEADS, n_seq * nch, L).transpose(1, 0, 2)
    y, c_out, n_out, m_out = pl.pallas_call(
        _ml_kernel,
        grid=(n_seq, nch),
        in_specs=[
            qspec(colblk["mq"]), qspec(colblk["mk"]), vspec(colblk["mv"]), vspec(colblk["mo"]),
            pl.BlockSpec((L, LANES), lambda s, c: (rb0 + s * nch + c, 0)),
            pl.BlockSpec((1, 2 * N_HEADS, L), lambda s, c: (s * nch + c, 0, 0)),
            pl.BlockSpec((1, N_HEADS, DV_ML, DK_ML), lambda s, c: (s, 0, 0, 0)),
            pl.BlockSpec((1, N_HEADS, DK_ML), lambda s, c: (s, 0, 0)),
            pl.BlockSpec((1, N_HEADS, LANES), lambda s, c: (s, 0, 0)),
            pl.BlockSpec((1, wv), lambda s, c: (0, 0)),
        ],
        out_specs=[
            pl.BlockSpec((L, wv), lambda s, c: (s * nch + c, 0)),
            pl.BlockSpec((1, N_HEADS, DV_ML, DK_ML), lambda s, c: (s, 0, 0, 0)),
            pl.BlockSpec((1, N_HEADS, DK_ML), lambda s, c: (s, 0, 0)),
            pl.BlockSpec((1, N_HEADS, LANES), lambda s, c: (s, 0, 0)),
        ],
        out_shape=[
            jax.ShapeDtypeStruct((n_seq * t_len, wv), BF16),
            jax.ShapeDtypeStruct((n_seq, N_HEADS, DV_ML, DK_ML), F32),
            jax.ShapeDtypeStruct((n_seq, N_HEADS, DK_ML), F32),
            jax.ShapeDtypeStruct((n_seq, N_HEADS, LANES), F32),
        ],
        scratch_shapes=[
            pltpu.VMEM((N_HEADS, DV_ML, DK_ML), F32),
            pltpu.VMEM((N_HEADS, DK_ML), F32),
            pltpu.VMEM((N_HEADS, LANES), F32),
        ],
        compiler_params=_cparams(("parallel", "arbitrary")),
        name="mlstm",
    )(z, z, z, z, gl, glt, c0, n0, m0b, g_gn)
    return y, c_out, n_out, m_out[..., 0]


def _pack_bf16_pairs(ub):
    c = ub.shape[1] // 2
    bits = lax.bitcast_convert_type(ub.astype(F32), U32)
    return (bits[:, :c] >> 16) | (bits[:, c:] & jnp.uint32(0xFFFF0000))


def _unpack_bf16_pairs(w):
    lo = lax.bitcast_convert_type(w << 16, F32).astype(BF16)
    hi = lax.bitcast_convert_type(w & jnp.uint32(0xFFFF0000), F32).astype(BF16)
    return lo, hi


def _merge_kernel(x_ref, ya_ref, yb_ref, ga_ref, gb_ref, wa_ref, wb_ref, wo_ref, g2_ref,
                  wrt_ref, brt_ref, h_ref, u2p_ref, lgt_ref):
    bra = jnp.dot(ya_ref[...], wa_ref[...], preferred_element_type=F32)
    brb = jnp.dot(yb_ref[...], wb_ref[...], preferred_element_type=F32)
    merged = _sigmoid(ga_ref[...]) * bra + _sigmoid(gb_ref[...]) * brb
    hmid = x_ref[...] + jnp.dot(merged.astype(BF16), wo_ref[...], preferred_element_type=F32)
    h_ref[...] = hmid
    ub = _rms(hmid, g2_ref[...]).astype(BF16)
    lgt_ref[...] = lax.dot_general(wrt_ref[...], ub, NT_DIMS, preferred_element_type=F32) + brt_ref[...]
    u2p_ref[...] = _pack_bf16_pairs(ub)


def _merge(x, ya, yb, z, wa, wb, wo, g2, wrt, brt, *, tm, colblk):
    n, d = x.shape
    e = wrt.shape[0]
    full = lambda a: pl.BlockSpec(a.shape, lambda i: (0,) * a.ndim, pipeline_mode=pl.Buffered(1))
    return pl.pallas_call(
        _merge_kernel,
        grid=(n // tm,),
        in_specs=[
            pl.BlockSpec((tm, d), lambda i: (i, 0)),
            pl.BlockSpec((tm, ya.shape[1]), lambda i: (i, 0)),
            pl.BlockSpec((tm, yb.shape[1]), lambda i: (i, 0)),
            pl.BlockSpec((tm, d), lambda i: (i, colblk["ga"])),
            pl.BlockSpec((tm, d), lambda i: (i, colblk["gb"])),
            full(wa), full(wb), full(wo), full(g2), full(wrt), full(brt),
        ],
        out_specs=[
            pl.BlockSpec((tm, d), lambda i: (i, 0)),
            pl.BlockSpec((tm, d // 2), lambda i: (i, 0)),
            pl.BlockSpec((e, tm), lambda i: (0, i)),
        ],
        out_shape=[
            jax.ShapeDtypeStruct((n, d), F32),
            jax.ShapeDtypeStruct((n, d // 2), U32),
            jax.ShapeDtypeStruct((e, n), F32),
        ],
        compiler_params=_cparams(("parallel",)),
        name="merge",
    )(x, ya, yb, z, z, wa, wb, wo, g2, wrt, brt)


def _route_kernel(lg_ref, ti_ref, tp_ref):
    l = lg_ref[...]
    e = l.shape[0]
    eid = lax.broadcasted_iota(I32, l.shape, 0)
    vals, idxs = [], []
    for _ in range(TOP_K):
        mx = jnp.max(l, axis=0, keepdims=True)
        ix = jnp.min(jnp.where(l == mx, eid, e), axis=0, keepdims=True)
        vals.append(mx)
        idxs.append(ix)
        l = jnp.where(eid == ix, -jnp.inf, l)
    ex = [jnp.exp(v - vals[0]) for v in vals]
    tot = ex[0] + ex[1] + ex[2] + ex[3]
    for kk in range(TOP_K):
        ti_ref[kk:kk + 1, :] = idxs[kk]
        tp_ref[kk:kk + 1, :] = ex[kk] / tot


def _route(lgt, *, tn):
    e, n = lgt.shape
    return pl.pallas_call(
        _route_kernel,
        grid=(n // tn,),
        in_specs=[pl.BlockSpec((e, tn), lambda i: (0, i))],
        out_specs=[pl.BlockSpec((TOP_K, tn), lambda i: (0, i)),
                   pl.BlockSpec((TOP_K, tn), lambda i: (0, i))],
        out_shape=[jax.ShapeDtypeStruct((TOP_K, n), I32),
                   jax.ShapeDtypeStruct((TOP_K, n), F32)],
        compiler_params=_cparams(("parallel",)),
        name="route",
    )(lgt)


def _dispatch_kernel(pos_ref, u_ref, xs_in_ref, xs_ref, sem):
    del xs_in_ref
    tm = u_ref.shape[0]

    def row_copy(r, kk):
        return pltpu.make_async_copy(u_ref.at[pl.ds(r, 1)], xs_ref.at[pl.ds(pos_ref[kk, r], 1)], sem)

    def start(r, carry):
        for kk in range(TOP_K):
            row_copy(r, kk).start()
        return carry

    def wait(r, carry):
        for kk in range(TOP_K):
            row_copy(r, kk).wait()
        return carry

    lax.fori_loop(0, tm, start, 0)
    lax.fori_loop(0, tm, wait, 0)


def _dispatch(pos, u2p, n_slots, *, tm):
    n, c = u2p.shape
    xs0 = jnp.zeros((n_slots, c), U32)
    return pl.pallas_call(
        _dispatch_kernel,
        grid=(n // tm,),
        in_specs=[
            pl.BlockSpec((TOP_K, tm), lambda i: (0, i), memory_space=pltpu.SMEM),
            pl.BlockSpec((tm, c), lambda i: (i, 0)),
            pl.BlockSpec(memory_space=pl.ANY),
        ],
        out_specs=pl.BlockSpec(memory_space=pl.ANY),
        out_shape=jax.ShapeDtypeStruct((n_slots, c), U32),
        scratch_shapes=[pltpu.SemaphoreType.DMA(())],
        input_output_aliases={2: 0},
        compiler_params=_cparams(("arbitrary",)),
        name="dispatch",
    )(pos, u2p, xs0)


def _expert_changed(te_ref, i):
    return jnp.logical_or(i == 0, te_ref[i] != te_ref[jnp.maximum(i - 1, 0)])


def _expert_up_kernel(te_ref, nu_ref, xs_ref, wg_ref, wu_ref, bg_ref, bu_ref, h_ref, wg_scr, wu_scr):
    i = pl.program_id(1)
    valid = i < nu_ref[0]

    @pl.when(jnp.logical_and(valid, _expert_changed(te_ref, i)))
    def _():
        wg_scr[...] = wg_ref[0].astype(BF16)
        wu_scr[...] = wu_ref[0].astype(BF16)

    @pl.when(valid)
    def _():
        lo, hi = _unpack_bf16_pairs(xs_ref[...])
        c = lo.shape[1]
        g = (jnp.dot(lo, wg_scr[:c, :], preferred_element_type=F32)
             + jnp.dot(hi, wg_scr[c:, :], preferred_element_type=F32) + bg_ref[0])
        up = (jnp.dot(lo, wu_scr[:c, :], preferred_element_type=F32)
              + jnp.dot(hi, wu_scr[c:, :], preferred_element_type=F32) + bu_ref[0])
        g = jnp.minimum(g, SWIGLU_LIMIT)
        up = jnp.clip(up, -SWIGLU_LIMIT, SWIGLU_LIMIT)
        h_ref[...] = (g * _sigmoid(SWIGLU_ALPHA * g) * (up + 1.0)).astype(BF16)

    @pl.when(jnp.logical_not(valid))
    def _():
        h_ref[...] = jnp.zeros_like(h_ref)


def _expert_up(te, nused, xs, w_gate, w_up, b_gate, b_up, *, tr, tf):
    p, c = xs.shape
    e, d, f = w_gate.shape
    row = lambda j, i, te_, nu_: (jnp.minimum(i, nu_[0] - 1), 0)
    wmap = lambda j, i, te_, nu_: (te_[i], 0, j)
    return pl.pallas_call(
        _expert_up_kernel,
        grid_spec=pltpu.PrefetchScalarGridSpec(
            num_scalar_prefetch=2,
            grid=(f // tf, p // tr),
            in_specs=[
                pl.BlockSpec((tr, c), row),
                pl.BlockSpec((1, d, tf), wmap),
                pl.BlockSpec((1, d, tf), wmap),
                pl.BlockSpec((1, 1, tf), wmap),
                pl.BlockSpec((1, 1, tf), wmap),
            ],
            out_specs=pl.BlockSpec((tr, tf), lambda j, i, te_, nu_: (i, j)),
            scratch_shapes=[pltpu.VMEM((d, tf), BF16), pltpu.VMEM((d, tf), BF16)],
        ),
        out_shape=jax.ShapeDtypeStruct((p, f), BF16),
        compiler_params=_cparams(("arbitrary", "arbitrary")),
        name="expert_up",
    )(te, nused, xs, w_gate, w_up, b_gate.reshape(e, 1, f), b_up.reshape(e, 1, f))


def _expert_down_kernel(te_ref, nu_ref, h_ref, wd_ref, bd_ref, y_ref, wd_scr):
    i = pl.program_id(1)
    valid = i < nu_ref[0]

    @pl.when(jnp.logical_and(valid, _expert_changed(te_ref, i)))
    def _():
        wd_scr[...] = wd_ref[0].astype(BF16)

    @pl.when(valid)
    def _():
        y_ref[...] = jnp.dot(h_ref[...], wd_scr[...], preferred_element_type=F32) + bd_ref[0]

    @pl.when(jnp.logical_not(valid))
    def _():
        y_ref[...] = jnp.zeros_like(y_ref)


def _expert_down(te, nused, hmid, w_down, b_down, *, tr, tn):
    p, f = hmid.shape
    e, _, d = w_down.shape
    wmap = lambda j, i, te_, nu_: (te_[i], 0, j)
    return pl.pallas_call(
        _expert_down_kernel,
        grid_spec=pltpu.PrefetchScalarGridSpec(
            num_scalar_prefetch=2,
            grid=(d // tn, p // tr),
            in_specs=[
                pl.BlockSpec((tr, f), lambda j, i, te_, nu_: (jnp.minimum(i, nu_[0] - 1), 0)),
                pl.BlockSpec((1, f, tn), wmap),
                pl.BlockSpec((1, 1, tn), wmap),
            ],
            out_specs=pl.BlockSpec((tr, tn), lambda j, i, te_, nu_: (i, j)),
            scratch_shapes=[pltpu.VMEM((f, tn), BF16)],
        ),
        out_shape=jax.ShapeDtypeStruct((p, d), F32),
        compiler_params=_cparams(("arbitrary", "arbitrary")),
        name="expert_down",
    )(te, nused, hmid, w_down, b_down.reshape(e, 1, d))


def _combine_kernel(pos_ref, h_ref, p_ref, gf_ref, ys_ref, o_ref, buf, sem):
    tm = h_ref.shape[0]

    def row_copy(r, kk):
        return pltpu.make_async_copy(ys_ref.at[pl.ds(pos_ref[kk, r], 1)], buf.at[kk, pl.ds(r, 1)], sem)

    def start(r, carry):
        for kk in range(TOP_K):
            row_copy(r, kk).start()
        return carry

    def wait(r, carry):
        for kk in range(TOP_K):
            row_copy(r, kk).wait()
        return carry

    lax.fori_loop(0, tm, start, 0)
    lax.fori_loop(0, tm, wait, 0)
    p = p_ref[...]
    acc = h_ref[...]
    for kk in range(TOP_K):
        acc = acc + p[:, kk:kk + 1] * buf[kk]
    o_ref[...] = _rms(acc, gf_ref[...])


def _combine(pos, h, probs_t, g_final, ys, *, tm):
    n, d = h.shape
    return pl.pallas_call(
        _combine_kernel,
        grid=(n // tm,),
        in_specs=[
            pl.BlockSpec((TOP_K, tm), lambda i: (0, i), memory_space=pltpu.SMEM),
            pl.BlockSpec((tm, d), lambda i: (i, 0)),
            pl.BlockSpec((tm, TOP_K), lambda i: (i, 0)),
            pl.BlockSpec((1, d), lambda i: (0, 0)),
            pl.BlockSpec(memory_space=pl.ANY),
        ],
        out_specs=pl.BlockSpec((tm, d), lambda i: (i, 0)),
        out_shape=jax.ShapeDtypeStruct((n, d), F32),
        scratch_shapes=[pltpu.VMEM((TOP_K, tm, d), F32), pltpu.SemaphoreType.DMA(())],
        compiler_params=_cparams(("arbitrary",)),
        name="combine",
    )(pos, h, probs_t, g_final, ys)


def _slot_layout(top_i, n_experts, tr, n_tiles):
    onehot = jnp.sum((top_i[:, :, None] == jnp.arange(n_experts, dtype=I32)[None, None, :]).astype(I32), axis=0)
    rank = jnp.cumsum(onehot, axis=0) - onehot
    counts = jnp.sum(onehot, axis=0)
    cpad = ((counts + tr - 1) // tr) * tr
    cend = jnp.cumsum(cpad)
    off = cend - cpad
    pos = jnp.take_along_axis((off[None, :] + rank).T, top_i, axis=0)
    nused = (cend[-1] // tr).astype(I32)
    tile_row = jnp.arange(n_tiles, dtype=I32) * tr
    te = jnp.minimum(jnp.searchsorted(cend, tile_row, side="right"), n_experts - 1).astype(I32)
    te = jnp.where(jnp.arange(n_tiles) < nused, te, te[nused - 1])
    return pos.astype(I32), te, nused.reshape(1)


def kernel(x_prompt, x_sample, state_ret_S, state_mlstm_C, state_mlstm_n, state_mlstm_m, g_norm1, w_in, b_if,
           g_ret_gn, g_ml_gn, w_ret_br, w_ml_br, w_out, g_norm2, w_router, b_router, w_gate, b_gate, w_up, b_up,
           w_down, b_down, g_final):
    bp, tp, d = x_prompt.shape
    bs, ts, _ = x_sample.shape
    depth = w_in.shape[0]
    n_experts = w_router.shape[-1]
    n_p, n_s = bp * tp, bs * ts
    n = n_p + n_s

    ret_qk, ret_v = N_HEADS * DK_RET, N_HEADS * DV_RET
    ml_qk, ml_v = N_HEADS * DK_ML, N_HEADS * DV_ML
    o_mi = 2 * ret_qk + 2 * ret_v + 2 * ml_qk + 2 * ml_v
    o_ga = o_mi + 2 * N_HEADS
    colblk = {"ga": 0, "gb": 1}
    base = 2 * d
    colblk.update(rq=base // ret_qk, rk=(base + ret_qk) // ret_qk, rv=(base + 2 * ret_qk) // ret_v,
                  rg=(base + 2 * ret_qk + ret_v) // ret_v)
    mbase = base + 2 * ret_qk + 2 * ret_v
    colblk.update(mq=mbase // ml_qk, mk=(mbase + ml_qk) // ml_qk, mv=(mbase + 2 * ml_qk) // ml_v,
                  mo=(mbase + 2 * ml_qk + ml_v) // ml_v)

    L_p = min(256, tp)
    L_s = min(64, ts)
    cos_p, sin_p = _rotary_tables(jnp.arange(tp))
    cos_s, sin_s = _rotary_tables(PAST_LEN + jnp.arange(ts))
    tabs_p = _retention_tables(L_p)
    tabs_s = _retention_tables(L_s)

    tr = 256
    n_slots = TOP_K * n + n_experts * tr
    n_tiles = n_slots // tr
    assert tp % L_p == 0 and ts % L_s == 0 and n_p % L_s == 0 and n % tr == 0 and n_s % tr == 0

    hp = x_prompt.reshape(n_p, d)
    hs = x_sample.reshape(n_s, d)
    outs_p, outs_s = [], []
    for layer in range(depth):
        x = jnp.concatenate([hp, hs], axis=0)
        wl = w_in[layer]
        w_cat = jnp.concatenate([wl[:, o_ga:], wl[:, :o_mi]], axis=1).astype(BF16)
        w_if = wl[:, o_mi:o_ga]
        wif = jnp.pad(w_if, ((0, 0), (0, LANES - 2 * N_HEADS))).astype(BF16)
        wift = w_if.T.astype(BF16)
        bflat = b_if[layer].reshape(1, 2 * N_HEADS)
        bif = jnp.pad(bflat, ((0, 0), (0, LANES - 2 * N_HEADS)))
        bift = bflat.reshape(2 * N_HEADS, 1)

        z, gl, glt = _inproj(x, g_norm1[layer].reshape(1, d), w_cat, wif, wift, bif, bift,
                             tm=min(1024, n_s), tn=512)

        g_r = g_ret_gn[layer].reshape(1, ret_v)
        g_m = g_ml_gn[layer].reshape(1, ml_v)
        zeros = lambda *s: jnp.zeros(s, F32)
        ya_p, s_p = _retention(z, 0, bp, tp, L_p, cos_p, sin_p, zeros(bp, N_HEADS, DK_RET, DV_RET), g_r,
                               tabs_p, colblk)
        ya_s, s_s = _retention(z, n_p, bs, ts, L_s, cos_s, sin_s, state_ret_S[layer].astype(F32), g_r,
                               tabs_s, colblk)
        yb_p, c_p, nn_p, m_p = _mlstm(z, gl, glt, 0, bp, tp, L_p, zeros(bp, N_HEADS, DV_ML, DK_ML),
                                      zeros(bp, N_HEADS, DK_ML), zeros(bp, N_HEADS), g_m, colblk)
        yb_s, c_s, nn_s, m_s = _mlstm(z, gl, glt, n_p, bs, ts, L_s, state_mlstm_C[layer].astype(F32),
                                      state_mlstm_n[layer].astype(F32), state_mlstm_m[layer].astype(F32),
                                      g_m, colblk)
        ya = jnp.concatenate([ya_p, ya_s], axis=0)
        yb = jnp.concatenate([yb_p, yb_s], axis=0)

        h, u2p, lgt = _merge(x, ya, yb, z, w_ret_br[layer].astype(BF16), w_ml_br[layer].astype(BF16),
                             w_out[layer].astype(BF16), g_norm2[layer].reshape(1, d),
                             w_router[layer].T.astype(BF16), b_router[layer].reshape(n_experts, 1),
                             tm=256, colblk=colblk)

        top_i, top_p = _route(lgt, tn=min(2048, n_s))
        pos, te, nused = _slot_layout(top_i, n_experts, tr, n_tiles)

        xs = _dispatch(pos, u2p, n_slots, tm=256)
        hmid = _expert_up(te, nused, xs, w_gate[layer], w_up[layer], b_gate[layer], b_up[layer], tr=tr, tf=512)
        ys = _expert_down(te, nused, hmid, w_down[layer], b_down[layer], tr=tr, tn=512)

        last = layer == depth - 1
        gf = g_final.reshape(1, d) if last else None
        assert last, "DEPTH > 1 needs an un-normalised combine output"
        y = _combine(pos, h, top_p.T, gf, ys, tm=256)
        hp, hs = y[:n_p], y[n_p:]
        outs_p.append((s_p, c_p, nn_p, m_p))
        outs_s.append((s_s, c_s, nn_s, m_s))

    stack = lambda outs, j: jnp.stack([o[j] for o in outs])
    return (hp.reshape(bp, tp, d), hs.reshape(bs, ts, d),
            stack(outs_p, 0), stack(outs_p, 1), stack(outs_p, 2), stack(outs_p, 3),
            stack(outs_s, 0), stack(outs_s, 1), stack(outs_s, 2), stack(outs_s, 3))
```

```python
import functools

import jax
import jax.numpy as jnp
import numpy as np
from jax import lax
from jax.experimental import pallas as pl
from jax.experimental.pallas import tpu as pltpu

F32 = jnp.float32
BF16 = jnp.bfloat16
U32 = jnp.uint32
I32 = jnp.int32

N_HEADS = 8
DK_RET = 128
DV_RET = 128
DK_ML = 64
DV_ML = 128
TOP_K = 4
PAST_LEN = 1024
ROPE_BASE = 10000.0
NORM_EPS = 1e-6
SWIGLU_ALPHA = 1.702
SWIGLU_LIMIT = 7.0

LANES = 128
VMEM_LIMIT = 56 * 1024 * 1024

NT_DIMS = (((1,), (1,)), ((), ()))


def _cparams(sem):
    return pltpu.CompilerParams(dimension_semantics=sem, vmem_limit_bytes=VMEM_LIMIT)


def _sigmoid(x):
    return 1.0 / (1.0 + jnp.exp(-x))


def _log_sigmoid(x):
    return jnp.minimum(x, 0.0) - jnp.log(1.0 + jnp.exp(-jnp.abs(x)))


def _rms(x, g):
    return x * lax.rsqrt(jnp.mean(x * x, axis=-1, keepdims=True) + NORM_EPS) * g


def _head_norm(o, g):
    mu = jnp.mean(o, axis=-1, keepdims=True)
    d = o - mu
    var = jnp.mean(d * d, axis=-1, keepdims=True)
    return d * lax.rsqrt(var + NORM_EPS) * g


def _inproj_kernel(x_ref, g_ref, wg_ref, wm_ref, wif_ref, wift_ref, bif_ref, bift_ref,
                   z_ref, gl_ref, glt_ref, u_scr, *, n_gate_tiles):
    j = pl.program_id(1)

    @pl.when(j == 0)
    def _():
        ub = _rms(x_ref[...], g_ref[...]).astype(BF16)
        u_scr[...] = ub
        gp = jnp.dot(ub, wif_ref[...], preferred_element_type=F32) + bif_ref[...]
        col = lax.broadcasted_iota(I32, gp.shape, 1)
        gl_ref[...] = jnp.where(col < N_HEADS, gp, _log_sigmoid(gp))
        gpt = lax.dot_general(wift_ref[...], ub, NT_DIMS, preferred_element_type=F32) + bift_ref[...]
        row = lax.broadcasted_iota(I32, gpt.shape, 0)
        glt_ref[...] = jnp.where(row < N_HEADS, gpt, _log_sigmoid(gpt))

    @pl.when(j < n_gate_tiles)
    def _():
        z_ref[...] = jnp.dot(u_scr[...], wg_ref[...], preferred_element_type=F32)

    @pl.when(j >= n_gate_tiles)
    def _():
        z_ref[...] = jnp.dot(u_scr[...], wm_ref[...], preferred_element_type=F32)


def _inproj(x, g1, w_gates, w_main, wif, wift, bif, bift, *, tm, tn):
    n, d = x.shape
    ng = w_gates.shape[1] // tn
    c = w_gates.shape[1] + w_main.shape[1]
    return pl.pallas_call(
        functools.partial(_inproj_kernel, n_gate_tiles=ng),
        grid=(n // tm, c // tn),
        in_specs=[
            pl.BlockSpec((tm, d), lambda i, j: (i, 0)),
            pl.BlockSpec((1, d), lambda i, j: (0, 0)),
            pl.BlockSpec((d, tn), lambda i, j: (0, jnp.minimum(j, ng - 1))),
            pl.BlockSpec((d, tn), lambda i, j: (0, jnp.maximum(j - ng, 0))),
            pl.BlockSpec((d, LANES), lambda i, j: (0, 0)),
            pl.BlockSpec((2 * N_HEADS, d), lambda i, j: (0, 0)),
            pl.BlockSpec((1, LANES), lambda i, j: (0, 0)),
            pl.BlockSpec((2 * N_HEADS, 1), lambda i, j: (0, 0)),
        ],
        out_specs=[
            pl.BlockSpec((tm, tn), lambda i, j: (i, j)),
            pl.BlockSpec((tm, LANES), lambda i, j: (i, 0)),
            pl.BlockSpec((2 * N_HEADS, tm), lambda i, j: (0, i)),
        ],
        out_shape=[
            jax.ShapeDtypeStruct((n, c), F32),
            jax.ShapeDtypeStruct((n, LANES), F32),
            jax.ShapeDtypeStruct((2 * N_HEADS, n), F32),
        ],
        scratch_shapes=[pltpu.VMEM((tm, d), BF16)],
        compiler_params=_cparams(("parallel", "arbitrary")),
        name="inproj",
    )(x, g1, w_gates, w_main, wif, wift, bif, bift)


def _ret_kernel(cd_ref, q_ref, k_ref, v_ref, rg_ref, cos_ref, sin_ref, s0_ref, g_ref,
                din_ref, qd_ref, kd_ref, y_ref, sout_ref, s_scr):
    @pl.when(pl.program_id(1) == 0)
    def _():
        s_scr[...] = s0_ref[0]

    cos = cos_ref[...]
    sin = sin_ref[...]
    for h in range(N_HEADS):
        sl = slice(h * DK_RET, (h + 1) * DK_RET)
        q = q_ref[:, sl]
        k = k_ref[:, sl]
        q = q * cos + pltpu.roll(q, DK_RET // 2, 1) * sin
        k = (k * cos + pltpu.roll(k, DK_RET // 2, 1) * sin) * (DK_RET ** -0.5)
        v = v_ref[:, sl].astype(BF16)
        s = s_scr[h]
        att = lax.dot_general(q.astype(BF16), k.astype(BF16), NT_DIMS,
                              preferred_element_type=F32) * din_ref[h]
        o = (jnp.dot(att.astype(BF16), v, preferred_element_type=F32)
             + jnp.dot((q * qd_ref[h]).astype(BF16), s.astype(BF16), preferred_element_type=F32))
        s_new = cd_ref[h] * s + jnp.dot((k * kd_ref[h]).T.astype(BF16), v, preferred_element_type=F32)
        s_scr[h] = s_new
        sout_ref[0, h] = s_new
        rg = rg_ref[:, sl]
        y = _head_norm(o, g_ref[:, sl]) * (rg * _sigmoid(rg))
        y_ref[:, sl] = y.astype(BF16)


def _retention(z, row0, n_seq, t_len, L, cos, sin, s0, g_gn, tabs, colblk):
    din, qd, kd, cd = tabs
    nch = t_len // L
    rb0 = row0 // L
    w = N_HEADS * DK_RET
    zspec = lambda cb: pl.BlockSpec((L, w), lambda s, c: (rb0 + s * nch + c, cb))
    tspec = pl.BlockSpec((L, DK_RET), lambda s, c: (c, 0))
    const3 = lambda shape: pl.BlockSpec(shape, lambda s, c: (0, 0, 0))
    y, s_out = pl.pallas_call(
        _ret_kernel,
        grid=(n_seq, nch),
        in_specs=[
            pl.BlockSpec(memory_space=pltpu.SMEM),
            zspec(colblk["rq"]), zspec(colblk["rk"]), zspec(colblk["rv"]), zspec(colblk["rg"]),
            tspec, tspec,
            pl.BlockSpec((1, N_HEADS, DK_RET, DV_RET), lambda s, c: (s, 0, 0, 0)),
            pl.BlockSpec((1, w), lambda s, c: (0, 0)),
            const3((N_HEADS, L, L)), const3((N_HEADS, L, DK_RET)), const3((N_HEADS, L, DK_RET)),
        ],
        out_specs=[
            pl.BlockSpec((L, w), lambda s, c: (s * nch + c, 0)),
            pl.BlockSpec((1, N_HEADS, DK_RET, DV_RET), lambda s, c: (s, 0, 0, 0)),
        ],
        scratch_shapes=[pltpu.VMEM((N_HEADS, DK_RET, DV_RET), F32)],
        out_shape=[
            jax.ShapeDtypeStruct((n_seq * t_len, w), BF16),
            jax.ShapeDtypeStruct((n_seq, N_HEADS, DK_RET, DV_RET), F32),
        ],
        compiler_params=_cparams(("parallel", "arbitrary")),
        name="retention",
    )(cd, z, z, z, z, cos, sin, s0, g_gn, din, qd, kd)
    return y, s_out


def _retention_tables(L):
    log_g = np.log(1.0 - 2.0 ** (-5.0 - np.arange(N_HEADS, dtype=np.float64)))
    idx = np.arange(L, dtype=np.float64)
    rel = idx[:, None] - idx[None, :]
    din = np.where(rel >= 0, np.exp(rel[None] * log_g[:, None, None]), 0.0)
    qd = np.exp((idx[None, :] + 1.0) * log_g[:, None])[..., None] * np.ones((1, 1, DK_RET))
    kd = np.exp((L - 1.0 - idx[None, :]) * log_g[:, None])[..., None] * np.ones((1, 1, DK_RET))
    cd = np.exp(L * log_g)
    return (jnp.asarray(din, F32), jnp.asarray(qd, F32), jnp.asarray(kd, F32), jnp.asarray(cd, F32))


def _rotary_tables(pos0, t_len):
    half = DK_RET // 2
    inv = ROPE_BASE ** (-np.arange(half, dtype=np.float64) / half)
    ang = (pos0 + np.arange(t_len, dtype=np.float64))[:, None] * inv[None, :]
    c, s = np.cos(ang), np.sin(ang)
    return (jnp.asarray(np.concatenate([c, c], axis=-1), F32),
            jnp.asarray(np.concatenate([-s, s], axis=-1), F32))


def _ml_kernel(q_ref, k_ref, v_ref, mo_ref, gl_ref, glt_ref, c0_ref, n0_ref, m0_ref, g_ref,
               y_ref, cout_ref, nout_ref, mout_ref, c_scr, n_scr, m_scr):
    @pl.when(pl.program_id(1) == 0)
    def _():
        c_scr[...] = c0_ref[0]
        n_scr[...] = n0_ref[0]
        m_scr[...] = m0_ref[0]

    L = q_ref.shape[0]
    ii = lax.broadcasted_iota(I32, (L, L), 0)
    jj = lax.broadcasted_iota(I32, (L, L), 1)
    causal = jj <= ii
    gl = gl_ref[...]
    glt = glt_ref[0]
    for h in range(N_HEADS):
        qs = slice(h * DK_ML, (h + 1) * DK_ML)
        vs = slice(h * DV_ML, (h + 1) * DV_ML)
        qf = q_ref[:, qs] * (DK_ML ** -0.5)
        kf = k_ref[:, qs]
        vf = v_ref[:, vs]
        q = qf.astype(BF16)
        k = kf.astype(BF16)
        ig_c = gl[:, h:h + 1]
        lf_c = gl[:, N_HEADS + h:N_HEADS + h + 1]
        ig_r = glt[h:h + 1, :]
        lf_r = glt[N_HEADS + h:N_HEADS + h + 1, :]
        b_c = jnp.sum(jnp.where(causal, jnp.broadcast_to(lf_r, (L, L)), 0.0), axis=1, keepdims=True)
        b_r = jnp.sum(jnp.where(ii <= jj, jnp.broadcast_to(lf_c, (L, L)), 0.0), axis=0, keepdims=True)
        logw = jnp.where(causal, b_c - b_r + ig_r, -jnp.inf)
        m_prev = m_scr[h:h + 1, 0:1]
        inter = b_c + m_prev
        m_row = jnp.maximum(inter, jnp.max(logw, axis=1, keepdims=True))
        qk = lax.dot_general(q, k, NT_DIMS, preferred_element_type=F32)
        w = jnp.exp(logw - m_row) * qk
        w_inter = jnp.exp(inter - m_row)
        c_old = c_scr[h]
        n_old = n_scr[h:h + 1, :]
        num = (jnp.dot(w.astype(BF16), vf.astype(BF16), preferred_element_type=F32)
               + w_inter * lax.dot_general(q, c_old.astype(BF16), NT_DIMS, preferred_element_type=F32))
        den = (jnp.sum(w, axis=1, keepdims=True)
               + w_inter * jnp.sum(qf * n_old, axis=1, keepdims=True))
        hh = num / jnp.maximum(jnp.abs(den), jnp.exp(-m_row))
        m_new = m_row[L - 1:L, :]
        b_last = b_c[L - 1:L, :]
        decay = jnp.exp(b_last + m_prev - m_new)
        wk_c = jnp.exp(b_last - b_c + ig_c - m_new)
        c_new = decay * c_old + jnp.dot((vf * wk_c).T.astype(BF16), k, preferred_element_type=F32)
        n_new = decay * n_old + jnp.sum(wk_c * kf, axis=0, keepdims=True)
        m_new_b = jnp.broadcast_to(m_new, (1, LANES))
        c_scr[h] = c_new
        n_scr[h:h + 1, :] = n_new
        m_scr[h:h + 1, :] = m_new_b
        cout_ref[0, h] = c_new
        nout_ref[0, h:h + 1, :] = n_new
        mout_ref[0, h:h + 1, :] = m_new_b
        y = _sigmoid(mo_ref[:, vs]) * _head_norm(hh, g_ref[:, vs])
        y_ref[:, vs] = y.astype(BF16)


def _mlstm(z, gl, glt, row0, n_seq, t_len, L, c0, n0, m0, g_gn, colblk):
    nch = t_len // L
    rb0 = row0 // L
    wqk = N_HEADS * DK_ML
    wv = N_HEADS * DV_ML
    qspec = lambda cb: pl.BlockSpec((L, wqk), lambda s, c: (rb0 + s * nch + c, cb))
    vspec = lambda cb: pl.BlockSpec((L, wv), lambda s, c: (rb0 + s * nch + c, cb))
    m0b = jnp.broadcast_to(m0[..., None], (n_seq, N_HEADS, LANES))
    glt = glt[:, row0:row0 + n_seq * t_len].reshape(2 * N_HEADS, n_seq * nch, L).transpose(1, 0, 2)
    y, c_out, n_out, m_out = pl.pallas_call(
        _ml_kernel,
        grid=(n_seq, nch),
        in_specs=[
            qspec(colblk["mq"]), qspec(colblk["mk"]), vspec(colblk["mv"]), vspec(colblk["mo"]),
            pl.BlockSpec((L, LANES), lambda s, c: (rb0 + s * nch + c, 0)),
            pl.BlockSpec((1, 2 * N_HEADS, L), lambda s, c: (s * nch + c, 0, 0)),
            pl.BlockSpec((1, N_HEADS, DV_ML, DK_ML), lambda s, c: (s, 0, 0, 0)),
            pl.BlockSpec((1, N_HEADS, DK_ML), lambda s, c: (s, 0, 0)),
            pl.BlockSpec((1, N_HEADS, LANES), lambda s, c: (s, 0, 0)),
            pl.BlockSpec((1, wv), lambda s, c: (0, 0)),
        ],
        out_specs=[
            pl.BlockSpec((L, wv), lambda s, c: (s * nch + c, 0)),
            pl.BlockSpec((1, N_HEADS, DV_ML, DK_ML), lambda s, c: (s, 0, 0, 0)),
            pl.BlockSpec((1, N_HEADS, DK_ML), lambda s, c: (s, 0, 0)),
            pl.BlockSpec((1, N_HEADS, LANES), lambda s, c: (s, 0, 0)),
        ],
        out_shape=[
            jax.ShapeDtypeStruct((n_seq * t_len, wv), BF16),
            jax.ShapeDtypeStruct((n_seq, N_HEADS, DV_ML, DK_ML), F32),
            jax.ShapeDtypeStruct((n_seq, N_HEADS, DK_ML), F32),
            jax.ShapeDtypeStruct((n_seq, N_HEADS, LANES), F32),
        ],
        scratch_shapes=[
            pltpu.VMEM((N_HEADS, DV_ML, DK_ML), F32),
            pltpu.VMEM((N_HEADS, DK_ML), F32),
            pltpu.VMEM((N_HEADS, LANES), F32),
        ],
        compiler_params=_cparams(("parallel", "arbitrary")),
        name="mlstm",
    )(z, z, z, z, gl, glt, c0, n0, m0b, g_gn)
    return y, c_out, n_out, m_out[..., 0]


def _pack_bf16_pairs(ub):
    c = ub.shape[1] // 2
    bits = lax.bitcast_convert_type(ub.astype(F32), U32)
    return (bits[:, :c] >> 16) | (bits[:, c:] & jnp.uint32(0xFFFF0000))


def _unpack_bf16_pairs(w):
    lo = lax.bitcast_convert_type(w << 16, F32).astype(BF16)
    hi = lax.bitcast_convert_type(w & jnp.uint32(0xFFFF0000), F32).astype(BF16)
    return lo, hi


def _merge_kernel(x_ref, ya_ref, yb_ref, ga_ref, gb_ref, wa_ref, wb_ref, wo_ref, g2_ref,
                  wrt_ref, brt_ref, h_ref, u2p_ref, lgt_ref):
    bra = jnp.dot(ya_ref[...], wa_ref[...], preferred_element_type=F32)
    brb = jnp.dot(yb_ref[...], wb_ref[...], preferred_element_type=F32)
    merged = _sigmoid(ga_ref[...]) * bra + _sigmoid(gb_ref[...]) * brb
    hmid = x_ref[...] + jnp.dot(merged.astype(BF16), wo_ref[...], preferred_element_type=F32)
    h_ref[...] = hmid
    ub = _rms(hmid, g2_ref[...]).astype(BF16)
    lgt_ref[...] = lax.dot_general(wrt_ref[...], ub, NT_DIMS, preferred_element_type=F32) + brt_ref[...]
    u2p_ref[...] = _pack_bf16_pairs(ub)


def _merge(x, ya, yb, z, wa, wb, wo, g2, wrt, brt, *, tm, colblk):
    n, d = x.shape
    e = wrt.shape[0]
    full = lambda a: pl.BlockSpec(a.shape, lambda i: (0,) * a.ndim, pipeline_mode=pl.Buffered(1))
    return pl.pallas_call(
        _merge_kernel,
        grid=(n // tm,),
        in_specs=[
            pl.BlockSpec((tm, d), lambda i: (i, 0)),
            pl.BlockSpec((tm, ya.shape[1]), lambda i: (i, 0)),
            pl.BlockSpec((tm, yb.shape[1]), lambda i: (i, 0)),
            pl.BlockSpec((tm, d), lambda i: (i, colblk["ga"])),
            pl.BlockSpec((tm, d), lambda i: (i, colblk["gb"])),
            full(wa), full(wb), full(wo), full(g2), full(wrt), full(brt),
        ],
        out_specs=[
            pl.BlockSpec((tm, d), lambda i: (i, 0)),
            pl.BlockSpec((tm, d // 2), lambda i: (i, 0)),
            pl.BlockSpec((e, tm), lambda i: (0, i)),
        ],
        out_shape=[
            jax.ShapeDtypeStruct((n, d), F32),
            jax.ShapeDtypeStruct((n, d // 2), U32),
            jax.ShapeDtypeStruct((e, n), F32),
        ],
        compiler_params=_cparams(("parallel",)),
        name="merge",
    )(x, ya, yb, z, z, wa, wb, wo, g2, wrt, brt)


def _route_kernel(lg_ref, ti_ref, tp_ref):
    l = lg_ref[...]
    e = l.shape[0]
    eid = lax.broadcasted_iota(I32, l.shape, 0)
    vals, idxs = [], []
    for _ in range(TOP_K):
        mx = jnp.max(l, axis=0, keepdims=True)
        ix = jnp.min(jnp.where(l == mx, eid, e), axis=0, keepdims=True)
        vals.append(mx)
        idxs.append(ix)
        l = jnp.where(eid == ix, -jnp.inf, l)
    ex = [jnp.exp(v - vals[0]) for v in vals]
    tot = ex[0] + ex[1] + ex[2] + ex[3]
    for kk in range(TOP_K):
        ti_ref[kk:kk + 1, :] = idxs[kk]
        tp_ref[kk:kk + 1, :] = ex[kk] / tot


def _route(lgt, *, tn):
    e, n = lgt.shape
    return pl.pallas_call(
        _route_kernel,
        grid=(n // tn,),
        in_specs=[pl.BlockSpec((e, tn), lambda i: (0, i))],
        out_specs=[pl.BlockSpec((TOP_K, tn), lambda i: (0, i)),
                   pl.BlockSpec((TOP_K, tn), lambda i: (0, i))],
        out_shape=[jax.ShapeDtypeStruct((TOP_K, n), I32),
                   jax.ShapeDtypeStruct((TOP_K, n), F32)],
        compiler_params=_cparams(("parallel",)),
        name="route",
    )(lgt)


def _dispatch_kernel(pos_ref, u_ref, xs_in_ref, xs_ref, sem):
    del xs_in_ref
    tm = u_ref.shape[0]

    def row_copy(r, kk):
        return pltpu.make_async_copy(u_ref.at[pl.ds(r, 1)], xs_ref.at[pl.ds(pos_ref[kk, r], 1)], sem)

    def start(r, carry):
        for kk in range(TOP_K):
            row_copy(r, kk).start()
        return carry

    def wait(r, carry):
        for kk in range(TOP_K):
            row_copy(r, kk).wait()
        return carry

    lax.fori_loop(0, tm, start, 0)
    lax.fori_loop(0, tm, wait, 0)


def _dispatch(pos, u2p, xs0, *, tm):
    n, c = u2p.shape
    n_slots = xs0.shape[0]
    return pl.pallas_call(
        _dispatch_kernel,
        grid=(n // tm,),
        in_specs=[
            pl.BlockSpec((TOP_K, tm), lambda i: (0, i), memory_space=pltpu.SMEM),
            pl.BlockSpec((tm, c), lambda i: (i, 0)),
            pl.BlockSpec(memory_space=pl.ANY),
        ],
        out_specs=pl.BlockSpec(memory_space=pl.ANY),
        out_shape=jax.ShapeDtypeStruct((n_slots, c), U32),
        scratch_shapes=[pltpu.SemaphoreType.DMA(())],
        input_output_aliases={2: 0},
        compiler_params=_cparams(("arbitrary",)),
        name="dispatch",
    )(pos, u2p, xs0)


def _expert_changed(te_ref, i):
    return jnp.logical_or(i == 0, te_ref[i] != te_ref[jnp.maximum(i - 1, 0)])


def _expert_up_kernel(te_ref, nu_ref, xs_ref, wg_ref, wu_ref, bg_ref, bu_ref, h_ref, wg_scr, wu_scr):
    i = pl.program_id(1)
    valid = i < nu_ref[0]

    @pl.when(jnp.logical_and(valid, _expert_changed(te_ref, i)))
    def _():
        wg_scr[...] = wg_ref[0].astype(BF16)
        wu_scr[...] = wu_ref[0].astype(BF16)

    @pl.when(valid)
    def _():
        lo, hi = _unpack_bf16_pairs(xs_ref[...])
        c = lo.shape[1]
        g = (jnp.dot(lo, wg_scr[:c, :], preferred_element_type=F32)
             + jnp.dot(hi, wg_scr[c:, :], preferred_element_type=F32) + bg_ref[0])
        up = (jnp.dot(lo, wu_scr[:c, :], preferred_element_type=F32)
              + jnp.dot(hi, wu_scr[c:, :], preferred_element_type=F32) + bu_ref[0])
        g = jnp.minimum(g, SWIGLU_LIMIT)
        up = jnp.clip(up, -SWIGLU_LIMIT, SWIGLU_LIMIT)
        h_ref[...] = (g * _sigmoid(SWIGLU_ALPHA * g) * (up + 1.0)).astype(BF16)

    @pl.when(jnp.logical_not(valid))
    def _():
        h_ref[...] = jnp.zeros_like(h_ref)


def _expert_up(te, nused, xs, w_gate, w_up, b_gate, b_up, *, tr, tf):
    p, c = xs.shape
    e, d, f = w_gate.shape
    row = lambda j, i, te_, nu_: (jnp.minimum(i, nu_[0] - 1), 0)
    wmap = lambda j, i, te_, nu_: (te_[i], 0, j)
    return pl.pallas_call(
        _expert_up_kernel,
        grid_spec=pltpu.PrefetchScalarGridSpec(
            num_scalar_prefetch=2,
            grid=(f // tf, p // tr),
            in_specs=[
                pl.BlockSpec((tr, c), row),
                pl.BlockSpec((1, d, tf), wmap),
                pl.BlockSpec((1, d, tf), wmap),
                pl.BlockSpec((1, 1, tf), wmap),
                pl.BlockSpec((1, 1, tf), wmap),
            ],
            out_specs=pl.BlockSpec((tr, tf), lambda j, i, te_, nu_: (i, j)),
            scratch_shapes=[pltpu.VMEM((d, tf), BF16), pltpu.VMEM((d, tf), BF16)],
        ),
        out_shape=jax.ShapeDtypeStruct((p, f), BF16),
        compiler_params=_cparams(("arbitrary", "arbitrary")),
        name="expert_up",
    )(te, nused, xs, w_gate, w_up, b_gate.reshape(e, 1, f), b_up.reshape(e, 1, f))


def _expert_down_kernel(te_ref, nu_ref, h_ref, wd_ref, bd_ref, y_ref, wd_scr):
    i = pl.program_id(1)
    valid = i < nu_ref[0]

    @pl.when(jnp.logical_and(valid, _expert_changed(te_ref, i)))
    def _():
        wd_scr[...] = wd_ref[0].astype(BF16)

    @pl.when(valid)
    def _():
        y_ref[...] = jnp.dot(h_ref[...], wd_scr[...], preferred_element_type=F32) + bd_ref[0]

    @pl.when(jnp.logical_not(valid))
    def _():
        y_ref[...] = jnp.zeros_like(y_ref)


def _expert_down(te, nused, hmid, w_down, b_down, *, tr, tn):
    p, f = hmid.shape
    e, _, d = w_down.shape
    wmap = lambda j, i, te_, nu_: (te_[i], 0, j)
    return pl.pallas_call(
        _expert_down_kernel,
        grid_spec=pltpu.PrefetchScalarGridSpec(
            num_scalar_prefetch=2,
            grid=(d // tn, p // tr),
            in_specs=[
                pl.BlockSpec((tr, f), lambda j, i, te_, nu_: (jnp.minimum(i, nu_[0] - 1), 0)),
                pl.BlockSpec((1, f, tn), wmap),
                pl.BlockSpec((1, 1, tn), wmap),
            ],
            out_specs=pl.BlockSpec((tr, tn), lambda j, i, te_, nu_: (i, j)),
            scratch_shapes=[pltpu.VMEM((f, tn), BF16)],
        ),
        out_shape=jax.ShapeDtypeStruct((p, d), F32),
        compiler_params=_cparams(("arbitrary", "arbitrary")),
        name="expert_down",
    )(te, nused, hmid, w_down, b_down.reshape(e, 1, d))


def _combine_kernel(pos_ref, h_ref, p_ref, gf_ref, ys_ref, o_ref, buf, sem):
    tm = h_ref.shape[0]

    def row_copy(r, kk):
        return pltpu.make_async_copy(ys_ref.at[pl.ds(pos_ref[kk, r], 1)], buf.at[kk, pl.ds(r, 1)], sem)

    def start(r, carry):
        for kk in range(TOP_K):
            row_copy(r, kk).start()
        return carry

    def wait(r, carry):
        for kk in range(TOP_K):
            row_copy(r, kk).wait()
        return carry

    lax.fori_loop(0, tm, start, 0)
    lax.fori_loop(0, tm, wait, 0)
    p = p_ref[...]
    acc = h_ref[...]
    for kk in range(TOP_K):
        acc = acc + p[:, kk:kk + 1] * buf[kk]
    o_ref[...] = _rms(acc, gf_ref[...])


def _combine(pos, h, probs_t, g_final, ys, *, tm):
    n, d = h.shape
    return pl.pallas_call(
        _combine_kernel,
        grid=(n // tm,),
        in_specs=[
            pl.BlockSpec((TOP_K, tm), lambda i: (0, i), memory_space=pltpu.SMEM),
            pl.BlockSpec((tm, d), lambda i: (i, 0)),
            pl.BlockSpec((tm, TOP_K), lambda i: (i, 0)),
            pl.BlockSpec((1, d), lambda i: (0, 0)),
            pl.BlockSpec(memory_space=pl.ANY),
        ],
        out_specs=pl.BlockSpec((tm, d), lambda i: (i, 0)),
        out_shape=jax.ShapeDtypeStruct((n, d), F32),
        scratch_shapes=[pltpu.VMEM((TOP_K, tm, d), F32), pltpu.SemaphoreType.DMA(())],
        compiler_params=_cparams(("arbitrary",)),
        name="combine",
    )(pos, h, probs_t, g_final, ys)


def _slot_layout(top_i, n_experts, tr, n_tiles):
    onehot = jnp.sum((top_i[:, :, None] == jnp.arange(n_experts, dtype=I32)[None, None, :]).astype(I32), axis=0)
    rank = jnp.cumsum(onehot, axis=0) - onehot
    counts = jnp.sum(onehot, axis=0)
    cpad = ((counts + tr - 1) // tr) * tr
    cend = jnp.cumsum(cpad)
    off = cend - cpad
    pos = jnp.take_along_axis((off[None, :] + rank).T, top_i, axis=0)
    nused = (cend[-1] // tr).astype(I32)
    tile_row = jnp.arange(n_tiles, dtype=I32) * tr
    te = jnp.sum((cend[None, :] <= tile_row[:, None]).astype(I32), axis=1)
    te = jnp.minimum(te, n_experts - 1)
    te = jnp.where(jnp.arange(n_tiles) < nused, te, te[nused - 1])
    return pos.astype(I32), te, nused.reshape(1)


def kernel(x_prompt, x_sample, state_ret_S, state_mlstm_C, state_mlstm_n, state_mlstm_m, g_norm1, w_in, b_if,
           g_ret_gn, g_ml_gn, w_ret_br, w_ml_br, w_out, g_norm2, w_router, b_router, w_gate, b_gate, w_up, b_up,
           w_down, b_down, g_final):
    bp, tp, d = x_prompt.shape
    bs, ts, _ = x_sample.shape
    depth = w_in.shape[0]
    n_experts = w_router.shape[-1]
    n_p, n_s = bp * tp, bs * ts
    n = n_p + n_s

    ret_qk, ret_v = N_HEADS * DK_RET, N_HEADS * DV_RET
    ml_qk, ml_v = N_HEADS * DK_ML, N_HEADS * DV_ML
    o_mi = 2 * ret_qk + 2 * ret_v + 2 * ml_qk + 2 * ml_v
    o_ga = o_mi + 2 * N_HEADS
    colblk = {"ga": 0, "gb": 1}
    base = 2 * d
    colblk.update(rq=base // ret_qk, rk=(base + ret_qk) // ret_qk, rv=(base + 2 * ret_qk) // ret_v,
                  rg=(base + 2 * ret_qk + ret_v) // ret_v)
    mbase = base + 2 * ret_qk + 2 * ret_v
    colblk.update(mq=mbase // ml_qk, mk=(mbase + ml_qk) // ml_qk, mv=(mbase + 2 * ml_qk) // ml_v,
                  mo=(mbase + 2 * ml_qk + ml_v) // ml_v)

    assert depth == 1, "one layer: the combine kernel applies the final norm"
    tr = 256
    n_slots = TOP_K * n + n_experts * tr
    n_tiles = n_slots // tr
    assert n_p % tr == 0 and n_s % tr == 0

    wl = w_in[0]
    w_gates = wl[:, o_ga:].astype(BF16)
    w_main = wl[:, :o_mi].astype(BF16)
    w_if = wl[:, o_mi:o_ga]
    wif = jnp.pad(w_if, ((0, 0), (0, LANES - 2 * N_HEADS))).astype(BF16)
    wift = w_if.T.astype(BF16)
    bflat = b_if[0].reshape(1, 2 * N_HEADS)
    bif = jnp.pad(bflat, ((0, 0), (0, LANES - 2 * N_HEADS)))
    bift = bflat.reshape(2 * N_HEADS, 1)
    g1 = g_norm1[0].reshape(1, d)
    g_r = g_ret_gn[0].reshape(1, ret_v)
    g_m = g_ml_gn[0].reshape(1, ml_v)
    wa, wb, wo = w_ret_br[0].astype(BF16), w_ml_br[0].astype(BF16), w_out[0].astype(BF16)
    g2 = g_norm2[0].reshape(1, d)
    wrt = w_router[0].T.astype(BF16)
    brt = b_router[0].reshape(n_experts, 1)

    def mixers(x, n_seq, t_len, L, pos0, s0, c0, n0, m0):
        assert t_len % L == 0
        z, gl, glt = _inproj(x, g1, w_gates, w_main, wif, wift, bif, bift, tm=min(1024, x.shape[0]), tn=512)
        cos, sin = _rotary_tables(pos0, t_len)
        ya, s_new = _retention(z, 0, n_seq, t_len, L, cos, sin, s0, g_r, _retention_tables(L), colblk)
        yb, c_new, n_new, m_new = _mlstm(z, gl, glt, 0, n_seq, t_len, L, c0, n0, m0, g_m, colblk)
        h, u2p, lgt = _merge(x, ya, yb, z, wa, wb, wo, g2, wrt, brt, tm=256, colblk=colblk)
        return h, u2p, lgt, (s_new, c_new, n_new, m_new)

    zeros = lambda *s: jnp.zeros(s, F32)
    h_p, u2p_p, lgt_p, st_p = mixers(
        x_prompt.reshape(n_p, d), bp, tp, min(256, tp), 0,
        zeros(bp, N_HEADS, DK_RET, DV_RET), zeros(bp, N_HEADS, DV_ML, DK_ML), zeros(bp, N_HEADS, DK_ML),
        zeros(bp, N_HEADS))
    h_s, u2p_s, lgt_s, st_s = mixers(
        x_sample.reshape(n_s, d), bs, ts, min(64, ts), PAST_LEN,
        state_ret_S[0].astype(F32), state_mlstm_C[0].astype(F32), state_mlstm_n[0].astype(F32),
        state_mlstm_m[0].astype(F32))

    top_i, top_p = _route(jnp.concatenate([lgt_p, lgt_s], axis=1), tn=min(2048, n_s))
    pos, te, nused = _slot_layout(top_i, n_experts, tr, n_tiles)
    probs_t = top_p.T

    xs = _dispatch(pos[:, :n_p], u2p_p, jnp.zeros((n_slots, d // 2), U32), tm=256)
    xs = _dispatch(pos[:, n_p:], u2p_s, xs, tm=256)
    hmid = _expert_up(te, nused, xs, w_gate[0], w_up[0], b_gate[0], b_up[0], tr=tr, tf=1024)
    ys = _expert_down(te, nused, hmid, w_down[0], b_down[0], tr=tr, tn=d)

    gf = g_final.reshape(1, d)
    y_p = _combine(pos[:, :n_p], h_p, probs_t[:n_p], gf, ys, tm=256)
    y_s = _combine(pos[:, n_p:], h_s, probs_t[n_p:], gf, ys, tm=256)

    return (y_p.reshape(bp, tp, d), y_s.reshape(bs, ts, d),
            st_p[0][None], st_p[1][None], st_p[2][None], st_p[3][None],
            st_s[0][None], st_s[1][None], st_s[2][None], st_s[3][None])
```

```python
import functools

import jax
import jax.numpy as jnp
import numpy as np
from jax import lax
from jax.experimental import pallas as pl
from jax.experimental.pallas import tpu as pltpu

F32 = jnp.float32
BF16 = jnp.bfloat16
U32 = jnp.uint32
I32 = jnp.int32

N_HEADS = 8
DK_RET = 128
DV_RET = 128
DK_ML = 64
DV_ML = 128
TOP_K = 4
PAST_LEN = 1024
ROPE_BASE = 10000.0
NORM_EPS = 1e-6
SWIGLU_ALPHA = 1.702
SWIGLU_LIMIT = 7.0

LANES = 128
VMEM_LIMIT = 56 * 1024 * 1024

NT_DIMS = (((1,), (1,)), ((), ()))


def _cparams(sem):
    return pltpu.CompilerParams(dimension_semantics=sem, vmem_limit_bytes=VMEM_LIMIT)


def _sigmoid(x):
    return 1.0 / (1.0 + jnp.exp(-x))


def _log_sigmoid(x):
    return jnp.minimum(x, 0.0) - jnp.log(1.0 + jnp.exp(-jnp.abs(x)))


def _rms(x, g):
    return x * lax.rsqrt(jnp.mean(x * x, axis=-1, keepdims=True) + NORM_EPS) * g


def _head_norm(o, g):
    mu = jnp.mean(o, axis=-1, keepdims=True)
    d = o - mu
    var = jnp.mean(d * d, axis=-1, keepdims=True)
    return d * lax.rsqrt(var + NORM_EPS) * g


def _inproj_kernel(x_ref, g_ref, wg_ref, wm_ref, wif_ref, wift_ref, bif_ref, bift_ref,
                   z_ref, gl_ref, glt_ref, u_scr, *, n_gate_tiles):
    j = pl.program_id(1)

    @pl.when(j == 0)
    def _():
        ub = _rms(x_ref[...], g_ref[...]).astype(BF16)
        u_scr[...] = ub
        gp = jnp.dot(ub, wif_ref[...], preferred_element_type=F32) + bif_ref[...]
        col = lax.broadcasted_iota(I32, gp.shape, 1)
        gl_ref[...] = jnp.where(col < N_HEADS, gp, _log_sigmoid(gp))
        gpt = lax.dot_general(wift_ref[...], ub, NT_DIMS, preferred_element_type=F32) + bift_ref[...]
        row = lax.broadcasted_iota(I32, gpt.shape, 0)
        glt_ref[...] = jnp.where(row < N_HEADS, gpt, _log_sigmoid(gpt))

    @pl.when(j < n_gate_tiles)
    def _():
        z_ref[...] = jnp.dot(u_scr[...], wg_ref[...].astype(BF16), preferred_element_type=F32)

    @pl.when(j >= n_gate_tiles)
    def _():
        z_ref[...] = jnp.dot(u_scr[...], wm_ref[...].astype(BF16), preferred_element_type=F32)


def _inproj(x, g1, w_gates, w_in, n_main, wif, wift, bif, bift, *, tm, tn):
    n, d = x.shape
    ng = w_gates.shape[1] // tn
    c = w_gates.shape[1] + n_main
    assert n_main % tn == 0
    return pl.pallas_call(
        functools.partial(_inproj_kernel, n_gate_tiles=ng),
        grid=(n // tm, c // tn),
        in_specs=[
            pl.BlockSpec((tm, d), lambda i, j: (i, 0)),
            pl.BlockSpec((1, d), lambda i, j: (0, 0)),
            pl.BlockSpec((d, tn), lambda i, j: (0, jnp.minimum(j, ng - 1))),
            pl.BlockSpec((d, tn), lambda i, j: (0, jnp.maximum(j - ng, 0))),
            pl.BlockSpec((d, LANES), lambda i, j: (0, 0)),
            pl.BlockSpec((2 * N_HEADS, d), lambda i, j: (0, 0)),
            pl.BlockSpec((1, LANES), lambda i, j: (0, 0)),
            pl.BlockSpec((2 * N_HEADS, 1), lambda i, j: (0, 0)),
        ],
        out_specs=[
            pl.BlockSpec((tm, tn), lambda i, j: (i, j)),
            pl.BlockSpec((tm, LANES), lambda i, j: (i, 0)),
            pl.BlockSpec((2 * N_HEADS, tm), lambda i, j: (0, i)),
        ],
        out_shape=[
            jax.ShapeDtypeStruct((n, c), F32),
            jax.ShapeDtypeStruct((n, LANES), F32),
            jax.ShapeDtypeStruct((2 * N_HEADS, n), F32),
        ],
        scratch_shapes=[pltpu.VMEM((tm, d), BF16)],
        compiler_params=_cparams(("parallel", "arbitrary")),
        name="inproj",
    )(x, g1, w_gates, w_in, wif, wift, bif, bift)


def _ret_kernel(cd_ref, q_ref, k_ref, v_ref, rg_ref, cos_ref, sin_ref, s0_ref, g_ref,
                din_ref, qd_ref, kd_ref, y_ref, sout_ref, s_scr):
    @pl.when(pl.program_id(1) == 0)
    def _():
        s_scr[...] = s0_ref[0]

    cos = cos_ref[...]
    sin = sin_ref[...]
    for h in range(N_HEADS):
        sl = slice(h * DK_RET, (h + 1) * DK_RET)
        q = q_ref[:, sl]
        k = k_ref[:, sl]
        q = q * cos + pltpu.roll(q, DK_RET // 2, 1) * sin
        k = (k * cos + pltpu.roll(k, DK_RET // 2, 1) * sin) * (DK_RET ** -0.5)
        v = v_ref[:, sl].astype(BF16)
        s = s_scr[h]
        att = lax.dot_general(q.astype(BF16), k.astype(BF16), NT_DIMS,
                              preferred_element_type=F32) * din_ref[h]
        o = (jnp.dot(att.astype(BF16), v, preferred_element_type=F32)
             + jnp.dot((q * qd_ref[h]).astype(BF16), s.astype(BF16), preferred_element_type=F32))
        s_new = cd_ref[h] * s + jnp.dot((k * kd_ref[h]).T.astype(BF16), v, preferred_element_type=F32)
        s_scr[h] = s_new
        sout_ref[0, h] = s_new
        rg = rg_ref[:, sl]
        y = _head_norm(o, g_ref[:, sl]) * (rg * _sigmoid(rg))
        y_ref[:, sl] = y.astype(BF16)


def _retention(z, row0, n_seq, t_len, L, cos, sin, s0, g_gn, tabs, colblk):
    din, qd, kd, cd = tabs
    nch = t_len // L
    rb0 = row0 // L
    w = N_HEADS * DK_RET
    zspec = lambda cb: pl.BlockSpec((L, w), lambda s, c: (rb0 + s * nch + c, cb))
    tspec = pl.BlockSpec((L, DK_RET), lambda s, c: (c, 0))
    const3 = lambda shape: pl.BlockSpec(shape, lambda s, c: (0, 0, 0))
    y, s_out = pl.pallas_call(
        _ret_kernel,
        grid=(n_seq, nch),
        in_specs=[
            pl.BlockSpec(memory_space=pltpu.SMEM),
            zspec(colblk["rq"]), zspec(colblk["rk"]), zspec(colblk["rv"]), zspec(colblk["rg"]),
            tspec, tspec,
            pl.BlockSpec((1, N_HEADS, DK_RET, DV_RET), lambda s, c: (s, 0, 0, 0)),
            pl.BlockSpec((1, w), lambda s, c: (0, 0)),
            const3((N_HEADS, L, L)), const3((N_HEADS, L, DK_RET)), const3((N_HEADS, L, DK_RET)),
        ],
        out_specs=[
            pl.BlockSpec((L, w), lambda s, c: (s * nch + c, 0)),
            pl.BlockSpec((1, N_HEADS, DK_RET, DV_RET), lambda s, c: (s, 0, 0, 0)),
        ],
        scratch_shapes=[pltpu.VMEM((N_HEADS, DK_RET, DV_RET), F32)],
        out_shape=[
            jax.ShapeDtypeStruct((n_seq * t_len, w), BF16),
            jax.ShapeDtypeStruct((n_seq, N_HEADS, DK_RET, DV_RET), F32),
        ],
        compiler_params=_cparams(("parallel", "arbitrary")),
        name="retention",
    )(cd, z, z, z, z, cos, sin, s0, g_gn, din, qd, kd)
    return y, s_out


def _retention_tables(L):
    log_g = np.log(1.0 - 2.0 ** (-5.0 - np.arange(N_HEADS, dtype=np.float64)))
    idx = np.arange(L, dtype=np.float64)
    rel = idx[:, None] - idx[None, :]
    din = np.where(rel >= 0, np.exp(rel[None] * log_g[:, None, None]), 0.0)
    qd = np.exp((idx[None, :] + 1.0) * log_g[:, None])[..., None] * np.ones((1, 1, DK_RET))
    kd = np.exp((L - 1.0 - idx[None, :]) * log_g[:, None])[..., None] * np.ones((1, 1, DK_RET))
    cd = np.exp(L * log_g)
    return (jnp.asarray(din, F32), jnp.asarray(qd, F32), jnp.asarray(kd, F32), jnp.asarray(cd, F32))


def _rotary_tables(pos0, t_len):
    half = DK_RET // 2
    inv = ROPE_BASE ** (-np.arange(half, dtype=np.float64) / half)
    ang = (pos0 + np.arange(t_len, dtype=np.float64))[:, None] * inv[None, :]
    c, s = np.cos(ang), np.sin(ang)
    return (jnp.asarray(np.concatenate([c, c], axis=-1), F32),
            jnp.asarray(np.concatenate([-s, s], axis=-1), F32))


def _ml_kernel(q_ref, k_ref, v_ref, mo_ref, gl_ref, glt_ref, c0_ref, n0_ref, m0_ref, g_ref,
               y_ref, cout_ref, nout_ref, mout_ref, c_scr, n_scr, m_scr):
    @pl.when(pl.program_id(1) == 0)
    def _():
        c_scr[...] = c0_ref[0]
        n_scr[...] = n0_ref[0]
        m_scr[...] = m0_ref[0]

    L = q_ref.shape[0]
    ii = lax.broadcasted_iota(I32, (L, L), 0)
    jj = lax.broadcasted_iota(I32, (L, L), 1)
    causal = jj <= ii
    gl = gl_ref[...]
    glt = glt_ref[0]
    m_all = m_scr[...]
    n_all = n_scr[...]
    m_rows, n_rows = [], []
    for h in range(N_HEADS):
        qs = slice(h * DK_ML, (h + 1) * DK_ML)
        vs = slice(h * DV_ML, (h + 1) * DV_ML)
        qf = q_ref[:, qs] * (DK_ML ** -0.5)
        kf = k_ref[:, qs]
        vf = v_ref[:, vs]
        q = qf.astype(BF16)
        k = kf.astype(BF16)
        ig_c = gl[:, h:h + 1]
        lf_c = gl[:, N_HEADS + h:N_HEADS + h + 1]
        ig_r = glt[h:h + 1, :]
        lf_r = glt[N_HEADS + h:N_HEADS + h + 1, :]
        b_c = jnp.sum(jnp.where(causal, jnp.broadcast_to(lf_r, (L, L)), 0.0), axis=1, keepdims=True)
        b_r = jnp.sum(jnp.where(ii <= jj, jnp.broadcast_to(lf_c, (L, L)), 0.0), axis=0, keepdims=True)
        logw = jnp.where(causal, b_c - b_r + ig_r, -jnp.inf)
        m_prev = m_all[h:h + 1, 0:1]
        inter = b_c + m_prev
        m_row = jnp.maximum(inter, jnp.max(logw, axis=1, keepdims=True))
        qk = lax.dot_general(q, k, NT_DIMS, preferred_element_type=F32)
        w = jnp.exp(logw - m_row) * qk
        w_inter = jnp.exp(inter - m_row)
        c_old = c_scr[h]
        n_old = n_all[h:h + 1, :]
        num = (jnp.dot(w.astype(BF16), vf.astype(BF16), preferred_element_type=F32)
               + w_inter * lax.dot_general(q, c_old.astype(BF16), NT_DIMS, preferred_element_type=F32))
        den = (jnp.sum(w, axis=1, keepdims=True)
               + w_inter * jnp.sum(qf * n_old, axis=1, keepdims=True))
        hh = num / jnp.maximum(jnp.abs(den), jnp.exp(-m_row))
        m_new = m_row[L - 1:L, :]
        b_last = b_c[L - 1:L, :]
        decay = jnp.exp(b_last + m_prev - m_new)
        wk_c = jnp.exp(b_last - b_c + ig_c - m_new)
        c_new = decay * c_old + jnp.dot((vf * wk_c).T.astype(BF16), k, preferred_element_type=F32)
        n_new = decay * n_old + jnp.sum(wk_c * kf, axis=0, keepdims=True)
        m_rows.append(jnp.broadcast_to(m_new, (1, LANES)))
        n_rows.append(n_new)
        c_scr[h] = c_new
        cout_ref[0, h] = c_new
        y = _sigmoid(mo_ref[:, vs]) * _head_norm(hh, g_ref[:, vs])
        y_ref[:, vs] = y.astype(BF16)
    m_out = jnp.concatenate(m_rows, axis=0)
    n_out = jnp.concatenate(n_rows, axis=0)
    m_scr[...] = m_out
    n_scr[...] = n_out
    mout_ref[0] = m_out
    nout_ref[0] = n_out


def _mlstm(z, gl, glt, row0, n_seq, t_len, L, c0, n0, m0, g_gn, colblk):
    nch = t_len // L
    rb0 = row0 // L
    wqk = N_HEADS * DK_ML
    wv = N_HEADS * DV_ML
    qspec = lambda cb: pl.BlockSpec((L, wqk), lambda s, c: (rb0 + s * nch + c, cb))
    vspec = lambda cb: pl.BlockSpec((L, wv), lambda s, c: (rb0 + s * nch + c, cb))
    m0b = jnp.broadcast_to(m0[..., None], (n_seq, N_HEADS, LANES))
    glt = glt[:, row0:row0 + n_seq * t_len].reshape(2 * N_HEADS, n_seq * nch, L).transpose(1, 0, 2)
    y, c_out, n_out, m_out = pl.pallas_call(
        _ml_kernel,
        grid=(n_seq, nch),
        in_specs=[
            qspec(colblk["mq"]), qspec(colblk["mk"]), vspec(colblk["mv"]), vspec(colblk["mo"]),
            pl.BlockSpec((L, LANES), lambda s, c: (rb0 + s * nch + c, 0)),
            pl.BlockSpec((1, 2 * N_HEADS, L), lambda s, c: (s * nch + c, 0, 0)),
            pl.BlockSpec((1, N_HEADS, DV_ML, DK_ML), lambda s, c: (s, 0, 0, 0)),
            pl.BlockSpec((1, N_HEADS, DK_ML), lambda s, c: (s, 0, 0)),
            pl.BlockSpec((1, N_HEADS, LANES), lambda s, c: (s, 0, 0)),
            pl.BlockSpec((1, wv), lambda s, c: (0, 0)),
        ],
        out_specs=[
            pl.BlockSpec((L, wv), lambda s, c: (s * nch + c, 0)),
            pl.BlockSpec((1, N_HEADS, DV_ML, DK_ML), lambda s, c: (s, 0, 0, 0)),
            pl.BlockSpec((1, N_HEADS, DK_ML), lambda s, c: (s, 0, 0)),
            pl.BlockSpec((1, N_HEADS, LANES), lambda s, c: (s, 0, 0)),
        ],
        out_shape=[
            jax.ShapeDtypeStruct((n_seq * t_len, wv), BF16),
            jax.ShapeDtypeStruct((n_seq, N_HEADS, DV_ML, DK_ML), F32),
            jax.ShapeDtypeStruct((n_seq, N_HEADS, DK_ML), F32),
            jax.ShapeDtypeStruct((n_seq, N_HEADS, LANES), F32),
        ],
        scratch_shapes=[
            pltpu.VMEM((N_HEADS, DV_ML, DK_ML), F32),
            pltpu.VMEM((N_HEADS, DK_ML), F32),
            pltpu.VMEM((N_HEADS, LANES), F32),
        ],
        compiler_params=_cparams(("parallel", "arbitrary")),
        name="mlstm",
    )(z, z, z, z, gl, glt, c0, n0, m0b, g_gn)
    return y, c_out, n_out, m_out[..., 0]


def _pack_bf16_pairs(ub):
    c = ub.shape[1] // 2
    bits = lax.bitcast_convert_type(ub.astype(F32), U32)
    return (bits[:, :c] >> 16) | (bits[:, c:] & jnp.uint32(0xFFFF0000))


def _unpack_bf16_pairs(w):
    lo = lax.bitcast_convert_type(w << 16, F32).astype(BF16)
    hi = lax.bitcast_convert_type(w & jnp.uint32(0xFFFF0000), F32).astype(BF16)
    return lo, hi


def _merge_kernel(x_ref, ya_ref, yb_ref, ga_ref, gb_ref, wa_ref, wb_ref, wo_ref, g2_ref,
                  wrt_ref, brt_ref, h_ref, u2p_ref, lgt_ref):
    bra = jnp.dot(ya_ref[...], wa_ref[...], preferred_element_type=F32)
    brb = jnp.dot(yb_ref[...], wb_ref[...], preferred_element_type=F32)
    merged = _sigmoid(ga_ref[...]) * bra + _sigmoid(gb_ref[...]) * brb
    hmid = x_ref[...] + jnp.dot(merged.astype(BF16), wo_ref[...], preferred_element_type=F32)
    h_ref[...] = hmid
    ub = _rms(hmid, g2_ref[...]).astype(BF16)
    lgt_ref[...] = lax.dot_general(wrt_ref[...], ub, NT_DIMS, preferred_element_type=F32) + brt_ref[...]
    u2p_ref[...] = _pack_bf16_pairs(ub)


def _merge(x, ya, yb, z, wa, wb, wo, g2, wrt, brt, *, tm, colblk):
    n, d = x.shape
    e = wrt.shape[0]
    full = lambda a: pl.BlockSpec(a.shape, lambda i: (0,) * a.ndim, pipeline_mode=pl.Buffered(1))
    return pl.pallas_call(
        _merge_kernel,
        grid=(n // tm,),
        in_specs=[
            pl.BlockSpec((tm, d), lambda i: (i, 0)),
            pl.BlockSpec((tm, ya.shape[1]), lambda i: (i, 0)),
            pl.BlockSpec((tm, yb.shape[1]), lambda i: (i, 0)),
            pl.BlockSpec((tm, d), lambda i: (i, colblk["ga"])),
            pl.BlockSpec((tm, d), lambda i: (i, colblk["gb"])),
            full(wa), full(wb), full(wo), full(g2), full(wrt), full(brt),
        ],
        out_specs=[
            pl.BlockSpec((tm, d), lambda i: (i, 0)),
            pl.BlockSpec((tm, d // 2), lambda i: (i, 0)),
            pl.BlockSpec((e, tm), lambda i: (0, i)),
        ],
        out_shape=[
            jax.ShapeDtypeStruct((n, d), F32),
            jax.ShapeDtypeStruct((n, d // 2), U32),
            jax.ShapeDtypeStruct((e, n), F32),
        ],
        compiler_params=_cparams(("parallel",)),
        name="merge",
    )(x, ya, yb, z, z, wa, wb, wo, g2, wrt, brt)


def _route_kernel(lg_ref, ti_ref, tp_ref):
    l = lg_ref[...]
    e = l.shape[0]
    eid = lax.broadcasted_iota(I32, l.shape, 0)
    vals, idxs = [], []
    for _ in range(TOP_K):
        mx = jnp.max(l, axis=0, keepdims=True)
        ix = jnp.min(jnp.where(l == mx, eid, e), axis=0, keepdims=True)
        vals.append(mx)
        idxs.append(ix)
        l = jnp.where(eid == ix, -jnp.inf, l)
    ex = [jnp.exp(v - vals[0]) for v in vals]
    tot = ex[0] + ex[1] + ex[2] + ex[3]
    for kk in range(TOP_K):
        ti_ref[kk:kk + 1, :] = idxs[kk]
        tp_ref[kk:kk + 1, :] = ex[kk] / tot


def _route(lgt, *, tn):
    e, n = lgt.shape
    return pl.pallas_call(
        _route_kernel,
        grid=(n // tn,),
        in_specs=[pl.BlockSpec((e, tn), lambda i: (0, i))],
        out_specs=[pl.BlockSpec((TOP_K, tn), lambda i: (0, i)),
                   pl.BlockSpec((TOP_K, tn), lambda i: (0, i))],
        out_shape=[jax.ShapeDtypeStruct((TOP_K, n), I32),
                   jax.ShapeDtypeStruct((TOP_K, n), F32)],
        compiler_params=_cparams(("parallel",)),
        name="route",
    )(lgt)


def _dispatch_kernel(pos_ref, u_ref, xs_in_ref, xs_ref, sem):
    del xs_in_ref
    tm = u_ref.shape[0]

    def row_copy(r, kk):
        return pltpu.make_async_copy(u_ref.at[pl.ds(r, 1)], xs_ref.at[pl.ds(pos_ref[kk, r], 1)], sem)

    def start(r, carry):
        for kk in range(TOP_K):
            row_copy(r, kk).start()
        return carry

    lax.fori_loop(0, tm, start, 0)
    for kk in range(TOP_K):
        pltpu.make_async_copy(u_ref, xs_ref.at[pl.ds(0, tm)], sem).wait()


def _dispatch(pos, u2p, xs0, *, tm):
    n, c = u2p.shape
    n_slots = xs0.shape[0]
    return pl.pallas_call(
        _dispatch_kernel,
        grid=(n // tm,),
        in_specs=[
            pl.BlockSpec((TOP_K, tm), lambda i: (0, i), memory_space=pltpu.SMEM),
            pl.BlockSpec((tm, c), lambda i: (i, 0)),
            pl.BlockSpec(memory_space=pl.ANY),
        ],
        out_specs=pl.BlockSpec(memory_space=pl.ANY),
        out_shape=jax.ShapeDtypeStruct((n_slots, c), U32),
        scratch_shapes=[pltpu.SemaphoreType.DMA(())],
        input_output_aliases={2: 0},
        compiler_params=_cparams(("arbitrary",)),
        name="dispatch",
    )(pos, u2p, xs0)


def _stage_weights(first, te_ref, gn_ref, w_hbm, w_stage, w_bf16, sems):
    j, i = pl.program_id(0), pl.program_id(1)
    tw = w_bf16[0].shape[1]

    def copies(e, col):
        cols = pl.ds(pl.multiple_of(col * tw, tw), tw)
        return [pltpu.make_async_copy(w.at[e, :, cols], st, sems.at[k])
                for k, (w, st) in enumerate(zip(w_hbm, w_stage))]

    @pl.when(first)
    def _():
        e = te_ref[i]

        @pl.when(jnp.logical_and(i == 0, j == 0))
        def _():
            for cp in copies(e, j):
                cp.start()

        for cp in copies(e, j):
            cp.wait()
        for st, bf in zip(w_stage, w_bf16):
            bf[...] = st[...].astype(BF16)

        nxt = gn_ref[i]

        @pl.when(nxt >= 0)
        def _():
            for cp in copies(nxt, j):
                cp.start()

        @pl.when(jnp.logical_and(nxt < 0, j + 1 < pl.num_programs(0)))
        def _():
            for cp in copies(te_ref[0], j + 1):
                cp.start()


def _expert_up_kernel(te_ref, nu_ref, gf_ref, gn_ref, xs_ref, wg_hbm, wu_hbm, bg_ref, bu_ref, h_ref,
                      wg_st, wu_st, wg_scr, wu_scr, sems):
    i = pl.program_id(1)
    valid = i < nu_ref[0]
    _stage_weights(jnp.logical_and(valid, gf_ref[i] == 1), te_ref, gn_ref,
                   (wg_hbm, wu_hbm), (wg_st, wu_st), (wg_scr, wu_scr), sems)

    @pl.when(valid)
    def _():
        lo, hi = _unpack_bf16_pairs(xs_ref[...])
        c = lo.shape[1]
        g = (jnp.dot(lo, wg_scr[:c, :], preferred_element_type=F32)
             + jnp.dot(hi, wg_scr[c:, :], preferred_element_type=F32) + bg_ref[0])
        up = (jnp.dot(lo, wu_scr[:c, :], preferred_element_type=F32)
              + jnp.dot(hi, wu_scr[c:, :], preferred_element_type=F32) + bu_ref[0])
        g = jnp.minimum(g, SWIGLU_LIMIT)
        up = jnp.clip(up, -SWIGLU_LIMIT, SWIGLU_LIMIT)
        h_ref[...] = (g * _sigmoid(SWIGLU_ALPHA * g) * (up + 1.0)).astype(BF16)

    @pl.when(jnp.logical_not(valid))
    def _():
        h_ref[...] = jnp.zeros_like(h_ref)


def _expert_up(tiles, xs, w_gate, w_up, b_gate, b_up, *, tr, tf):
    p, c = xs.shape
    e, d, f = w_gate.shape
    row = lambda j, i, te_, nu_, gf_, gn_: (jnp.minimum(i, nu_[0] - 1), 0)
    bmap = lambda j, i, te_, nu_, gf_, gn_: (te_[i], 0, j)
    return pl.pallas_call(
        _expert_up_kernel,
        grid_spec=pltpu.PrefetchScalarGridSpec(
            num_scalar_prefetch=4,
            grid=(f // tf, p // tr),
            in_specs=[
                pl.BlockSpec((tr, c), row),
                pl.BlockSpec(memory_space=pl.ANY),
                pl.BlockSpec(memory_space=pl.ANY),
                pl.BlockSpec((1, 1, tf), bmap),
                pl.BlockSpec((1, 1, tf), bmap),
            ],
            out_specs=pl.BlockSpec((tr, tf), lambda j, i, te_, nu_, gf_, gn_: (i, j)),
            scratch_shapes=[pltpu.VMEM((d, tf), F32), pltpu.VMEM((d, tf), F32),
                            pltpu.VMEM((d, tf), BF16), pltpu.VMEM((d, tf), BF16),
                            pltpu.SemaphoreType.DMA((2,))],
        ),
        out_shape=jax.ShapeDtypeStruct((p, f), BF16),
        compiler_params=_cparams(("arbitrary", "arbitrary")),
        name="expert_up",
    )(*tiles, xs, w_gate, w_up, b_gate.reshape(e, 1, f), b_up.reshape(e, 1, f))


def _expert_down_kernel(te_ref, nu_ref, gf_ref, gn_ref, h_ref, wd_hbm, bd_ref, y_ref, wd_st, wd_scr, sems):
    i = pl.program_id(1)
    valid = i < nu_ref[0]
    _stage_weights(jnp.logical_and(valid, gf_ref[i] == 1), te_ref, gn_ref, (wd_hbm,), (wd_st,), (wd_scr,), sems)

    @pl.when(valid)
    def _():
        y_ref[...] = jnp.dot(h_ref[...], wd_scr[...], preferred_element_type=F32) + bd_ref[0]

    @pl.when(jnp.logical_not(valid))
    def _():
        y_ref[...] = jnp.zeros_like(y_ref)


def _expert_down(tiles, hmid, w_down, b_down, *, tr, tn):
    p, f = hmid.shape
    e, _, d = w_down.shape
    return pl.pallas_call(
        _expert_down_kernel,
        grid_spec=pltpu.PrefetchScalarGridSpec(
            num_scalar_prefetch=4,
            grid=(d // tn, p // tr),
            in_specs=[
                pl.BlockSpec((tr, f), lambda j, i, te_, nu_, gf_, gn_: (jnp.minimum(i, nu_[0] - 1), 0)),
                pl.BlockSpec(memory_space=pl.ANY),
                pl.BlockSpec((1, 1, tn), lambda j, i, te_, nu_, gf_, gn_: (te_[i], 0, j)),
            ],
            out_specs=pl.BlockSpec((tr, tn), lambda j, i, te_, nu_, gf_, gn_: (i, j)),
            scratch_shapes=[pltpu.VMEM((f, tn), F32), pltpu.VMEM((f, tn), BF16),
                            pltpu.SemaphoreType.DMA((1,))],
        ),
        out_shape=jax.ShapeDtypeStruct((p, d), F32),
        compiler_params=_cparams(("arbitrary", "arbitrary")),
        name="expert_down",
    )(*tiles, hmid, w_down, b_down.reshape(e, 1, d))


def _combine_kernel(pos_ref, h_ref, p_ref, gf_ref, ys_ref, o_ref, buf, sem):
    tm = h_ref.shape[0]

    def row_copy(r, kk):
        return pltpu.make_async_copy(ys_ref.at[pl.ds(pos_ref[kk, r], 1)], buf.at[kk, pl.ds(r, 1)], sem)

    def start(r, carry):
        for kk in range(TOP_K):
            row_copy(r, kk).start()
        return carry

    lax.fori_loop(0, tm, start, 0)
    for kk in range(TOP_K):
        pltpu.make_async_copy(ys_ref.at[pl.ds(0, tm)], buf.at[kk], sem).wait()
    p = p_ref[...]
    acc = h_ref[...]
    for kk in range(TOP_K):
        acc = acc + p[:, kk:kk + 1] * buf[kk]
    o_ref[...] = _rms(acc, gf_ref[...])


def _combine(pos, h, probs_t, g_final, ys, *, tm):
    n, d = h.shape
    return pl.pallas_call(
        _combine_kernel,
        grid=(n // tm,),
        in_specs=[
            pl.BlockSpec((TOP_K, tm), lambda i: (0, i), memory_space=pltpu.SMEM),
            pl.BlockSpec((tm, d), lambda i: (i, 0)),
            pl.BlockSpec((tm, TOP_K), lambda i: (i, 0)),
            pl.BlockSpec((1, d), lambda i: (0, 0)),
            pl.BlockSpec(memory_space=pl.ANY),
        ],
        out_specs=pl.BlockSpec((tm, d), lambda i: (i, 0)),
        out_shape=jax.ShapeDtypeStruct((n, d), F32),
        scratch_shapes=[pltpu.VMEM((TOP_K, tm, d), F32), pltpu.SemaphoreType.DMA(())],
        compiler_params=_cparams(("arbitrary",)),
        name="combine",
    )(pos, h, probs_t, g_final, ys)


def _slot_layout(top_i, n_experts, tr, n_tiles):
    onehot = jnp.sum((top_i[:, :, None] == jnp.arange(n_experts, dtype=I32)[None, None, :]).astype(I32), axis=0)
    rank = jnp.cumsum(onehot, axis=0) - onehot
    counts = jnp.sum(onehot, axis=0)
    cpad = ((counts + tr - 1) // tr) * tr
    cend = jnp.cumsum(cpad)
    off = cend - cpad
    pos = jnp.take_along_axis((off[None, :] + rank).T, top_i, axis=0)
    nused = (cend[-1] // tr).astype(I32)
    tile_row = jnp.arange(n_tiles, dtype=I32) * tr
    te = jnp.sum((cend[None, :] <= tile_row[:, None]).astype(I32), axis=1)
    te = jnp.minimum(te, n_experts - 1)
    used = jnp.arange(n_tiles) < nused
    te = jnp.where(used, te, te[nused - 1])
    gfirst = jnp.logical_and(used, jnp.concatenate([jnp.ones((1,), bool), te[1:] != te[:-1]])).astype(I32)
    eid = jnp.arange(n_experts, dtype=I32)
    later = jnp.logical_and(eid[None, :] > eid[:, None], counts[None, :] > 0)
    nxt = jnp.min(jnp.where(later, eid[None, :], n_experts), axis=1)
    gnext = jnp.where(nxt < n_experts, nxt, -1).astype(I32)[te]
    return pos.astype(I32), (te, nused.reshape(1), gfirst, gnext)


def kernel(x_prompt, x_sample, state_ret_S, state_mlstm_C, state_mlstm_n, state_mlstm_m, g_norm1, w_in, b_if,
           g_ret_gn, g_ml_gn, w_ret_br, w_ml_br, w_out, g_norm2, w_router, b_router, w_gate, b_gate, w_up, b_up,
           w_down, b_down, g_final):
    bp, tp, d = x_prompt.shape
    bs, ts, _ = x_sample.shape
    depth = w_in.shape[0]
    n_experts = w_router.shape[-1]
    n_p, n_s = bp * tp, bs * ts
    n = n_p + n_s

    ret_qk, ret_v = N_HEADS * DK_RET, N_HEADS * DV_RET
    ml_qk, ml_v = N_HEADS * DK_ML, N_HEADS * DV_ML
    o_mi = 2 * ret_qk + 2 * ret_v + 2 * ml_qk + 2 * ml_v
    o_ga = o_mi + 2 * N_HEADS
    colblk = {"ga": 0, "gb": 1}
    base = 2 * d
    colblk.update(rq=base // ret_qk, rk=(base + ret_qk) // ret_qk, rv=(base + 2 * ret_qk) // ret_v,
                  rg=(base + 2 * ret_qk + ret_v) // ret_v)
    mbase = base + 2 * ret_qk + 2 * ret_v
    colblk.update(mq=mbase // ml_qk, mk=(mbase + ml_qk) // ml_qk, mv=(mbase + 2 * ml_qk) // ml_v,
                  mo=(mbase + 2 * ml_qk + ml_v) // ml_v)

    assert depth == 1, "one layer: the combine kernel applies the final norm"
    tr = 256
    n_slots = TOP_K * n + n_experts * tr
    n_tiles = n_slots // tr
    assert n_p % tr == 0 and n_s % tr == 0

    wl = w_in[0]
    w_gates = wl[:, o_ga:]
    w_if = wl[:, o_mi:o_ga]
    wif = jnp.pad(w_if, ((0, 0), (0, LANES - 2 * N_HEADS))).astype(BF16)
    wift = w_if.T.astype(BF16)
    bflat = b_if[0].reshape(1, 2 * N_HEADS)
    bif = jnp.pad(bflat, ((0, 0), (0, LANES - 2 * N_HEADS)))
    bift = bflat.reshape(2 * N_HEADS, 1)
    g1 = g_norm1[0].reshape(1, d)
    g_r = g_ret_gn[0].reshape(1, ret_v)
    g_m = g_ml_gn[0].reshape(1, ml_v)
    wa, wb, wo = w_ret_br[0].astype(BF16), w_ml_br[0].astype(BF16), w_out[0].astype(BF16)
    g2 = g_norm2[0].reshape(1, d)
    wrt = w_router[0].T.astype(BF16)
    brt = b_router[0].reshape(n_experts, 1)

    def mixers(x, n_seq, t_len, L, pos0, s0, c0, n0, m0):
        assert t_len % L == 0
        z, gl, glt = _inproj(x, g1, w_gates, wl, o_mi, wif, wift, bif, bift, tm=min(1024, x.shape[0]), tn=512)
        cos, sin = _rotary_tables(pos0, t_len)
        ya, s_new = _retention(z, 0, n_seq, t_len, L, cos, sin, s0, g_r, _retention_tables(L), colblk)
        yb, c_new, n_new, m_new = _mlstm(z, gl, glt, 0, n_seq, t_len, L, c0, n0, m0, g_m, colblk)
        h, u2p, lgt = _merge(x, ya, yb, z, wa, wb, wo, g2, wrt, brt, tm=256, colblk=colblk)
        return h, u2p, lgt, (s_new, c_new, n_new, m_new)

    zeros = lambda *s: jnp.zeros(s, F32)
    h_p, u2p_p, lgt_p, st_p = mixers(
        x_prompt.reshape(n_p, d), bp, tp, min(256, tp), 0,
        zeros(bp, N_HEADS, DK_RET, DV_RET), zeros(bp, N_HEADS, DV_ML, DK_ML), zeros(bp, N_HEADS, DK_ML),
        zeros(bp, N_HEADS))
    h_s, u2p_s, lgt_s, st_s = mixers(
        x_sample.reshape(n_s, d), bs, ts, min(64, ts), PAST_LEN,
        state_ret_S[0].astype(F32), state_mlstm_C[0].astype(F32), state_mlstm_n[0].astype(F32),
        state_mlstm_m[0].astype(F32))

    top_i, top_p = _route(jnp.concatenate([lgt_p, lgt_s], axis=1), tn=min(2048, n_s))
    pos, tiles = _slot_layout(top_i, n_experts, tr, n_tiles)
    probs_t = top_p.T

    xs = _dispatch(pos[:, :n_p], u2p_p, jnp.zeros((n_slots, d // 2), U32), tm=256)
    xs = _dispatch(pos[:, n_p:], u2p_s, xs, tm=256)
    hmid = _expert_up(tiles, xs, w_gate[0], w_up[0], b_gate[0], b_up[0], tr=tr, tf=1024)
    ys = _expert_down(tiles, hmid, w_down[0], b_down[0], tr=tr, tn=d)

    gf = g_final.reshape(1, d)
    y_p = _combine(pos[:, :n_p], h_p, probs_t[:n_p], gf, ys, tm=256)
    y_s = _combine(pos[:, n_p:], h_s, probs_t[n_p:], gf, ys, tm=256)

    return (y_p.reshape(bp, tp, d), y_s.reshape(bs, ts, d),
            st_p[0][None], st_p[1][None], st_p[2][None], st_p[3][None],
            st_s[0][None], st_s[1][None], st_s[2][None], st_s[3][None])
```

```python
import functools

import jax
import jax.numpy as jnp
import numpy as np
from jax import lax
from jax.experimental import pallas as pl
from jax.experimental.pallas import tpu as pltpu

F32 = jnp.float32
BF16 = jnp.bfloat16
U32 = jnp.uint32
I32 = jnp.int32

N_HEADS = 8
DK_RET = 128
DV_RET = 128
DK_ML = 64
DV_ML = 128
TOP_K = 4
PAST_LEN = 1024
ROPE_BASE = 10000.0
NORM_EPS = 1e-6
SWIGLU_ALPHA = 1.702
SWIGLU_LIMIT = 7.0

LANES = 128
VMEM_LIMIT = 56 * 1024 * 1024

DMA_ISSUE_UNROLL = 8

NT_DIMS = (((1,), (1,)), ((), ()))


def _cparams(sem):
    return pltpu.CompilerParams(dimension_semantics=sem, vmem_limit_bytes=VMEM_LIMIT)


def _sigmoid(x):
    return 1.0 / (1.0 + jnp.exp(-x))


def _log_sigmoid(x):
    return jnp.minimum(x, 0.0) - jnp.log(1.0 + jnp.exp(-jnp.abs(x)))


def _rms(x, g):
    return x * lax.rsqrt(jnp.mean(x * x, axis=-1, keepdims=True) + NORM_EPS) * g


def _head_norm(o, g):
    mu = jnp.mean(o, axis=-1, keepdims=True)
    d = o - mu
    var = jnp.mean(d * d, axis=-1, keepdims=True)
    return d * lax.rsqrt(var + NORM_EPS) * g


def _inproj_kernel(x_ref, g_ref, wg_ref, wm_ref, wif_ref, wift_ref, bif_ref, bift_ref,
                   z_ref, gl_ref, glt_ref, u_scr, *, n_gate_tiles):
    j = pl.program_id(1)

    @pl.when(j == 0)
    def _():
        ub = _rms(x_ref[...], g_ref[...]).astype(BF16)
        u_scr[...] = ub
        gp = jnp.dot(ub, wif_ref[...], preferred_element_type=F32) + bif_ref[...]
        col = lax.broadcasted_iota(I32, gp.shape, 1)
        gl_ref[...] = jnp.where(col < N_HEADS, gp, _log_sigmoid(gp))
        gpt = lax.dot_general(wift_ref[...], ub, NT_DIMS, preferred_element_type=F32) + bift_ref[...]
        row = lax.broadcasted_iota(I32, gpt.shape, 0)
        glt_ref[...] = jnp.where(row < N_HEADS, gpt, _log_sigmoid(gpt))

    @pl.when(j < n_gate_tiles)
    def _():
        z_ref[...] = lax.dot_general(u_scr[...], wg_ref[...].astype(BF16), NT_DIMS, preferred_element_type=F32)

    @pl.when(j >= n_gate_tiles)
    def _():
        z_ref[...] = lax.dot_general(u_scr[...], wm_ref[...].astype(BF16), NT_DIMS, preferred_element_type=F32)


def _inproj(x, g1, w_gates_t, w_in_t, n_main, wif, wift, bif, bift, *, tm, tn):
    n, d = x.shape
    ng = w_gates_t.shape[0] // tn
    c = w_gates_t.shape[0] + n_main
    assert n_main % tn == 0
    return pl.pallas_call(
        functools.partial(_inproj_kernel, n_gate_tiles=ng),
        grid=(n // tm, c // tn),
        in_specs=[
            pl.BlockSpec((tm, d), lambda i, j: (i, 0)),
            pl.BlockSpec((1, d), lambda i, j: (0, 0)),
            pl.BlockSpec((tn, d), lambda i, j: (jnp.minimum(j, ng - 1), 0)),
            pl.BlockSpec((tn, d), lambda i, j: (jnp.maximum(j - ng, 0), 0)),
            pl.BlockSpec((d, LANES), lambda i, j: (0, 0)),
            pl.BlockSpec((2 * N_HEADS, d), lambda i, j: (0, 0)),
            pl.BlockSpec((1, LANES), lambda i, j: (0, 0)),
            pl.BlockSpec((2 * N_HEADS, 1), lambda i, j: (0, 0)),
        ],
        out_specs=[
            pl.BlockSpec((tm, tn), lambda i, j: (i, j)),
            pl.BlockSpec((tm, LANES), lambda i, j: (i, 0)),
            pl.BlockSpec((2 * N_HEADS, tm), lambda i, j: (0, i)),
        ],
        out_shape=[
            jax.ShapeDtypeStruct((n, c), F32),
            jax.ShapeDtypeStruct((n, LANES), F32),
            jax.ShapeDtypeStruct((2 * N_HEADS, n), F32),
        ],
        scratch_shapes=[pltpu.VMEM((tm, d), BF16)],
        compiler_params=_cparams(("parallel", "arbitrary")),
        name="inproj",
    )(x, g1, w_gates_t, w_in_t, wif, wift, bif, bift)


def _ret_kernel(cd_ref, q_ref, k_ref, v_ref, rg_ref, cos_ref, sin_ref, s0_ref, g_ref,
                din_ref, qd_ref, kd_ref, y_ref, sout_ref, s_scr):
    @pl.when(pl.program_id(1) == 0)
    def _():
        s_scr[...] = s0_ref[0]

    cos = cos_ref[...]
    sin = sin_ref[...]
    for h in range(N_HEADS):
        sl = slice(h * DK_RET, (h + 1) * DK_RET)
        q = q_ref[:, sl]
        k = k_ref[:, sl]
        q = q * cos + pltpu.roll(q, DK_RET // 2, 1) * sin
        k = (k * cos + pltpu.roll(k, DK_RET // 2, 1) * sin) * (DK_RET ** -0.5)
        v = v_ref[:, sl].astype(BF16)
        s = s_scr[h]
        att = lax.dot_general(q.astype(BF16), k.astype(BF16), NT_DIMS,
                              preferred_element_type=F32) * din_ref[h]
        o = (jnp.dot(att.astype(BF16), v, preferred_element_type=F32)
             + jnp.dot((q * qd_ref[h]).astype(BF16), s.astype(BF16), preferred_element_type=F32))
        s_new = cd_ref[h] * s + jnp.dot((k * kd_ref[h]).T.astype(BF16), v, preferred_element_type=F32)
        s_scr[h] = s_new
        sout_ref[0, h] = s_new
        rg = rg_ref[:, sl]
        y = _head_norm(o, g_ref[:, sl]) * (rg * _sigmoid(rg))
        y_ref[:, sl] = y.astype(BF16)


def _retention(z, row0, n_seq, t_len, L, cos, sin, s0, g_gn, tabs, colblk):
    din, qd, kd, cd = tabs
    nch = t_len // L
    rb0 = row0 // L
    w = N_HEADS * DK_RET
    zspec = lambda cb: pl.BlockSpec((L, w), lambda s, c: (rb0 + s * nch + c, cb))
    tspec = pl.BlockSpec((L, DK_RET), lambda s, c: (c, 0))
    const3 = lambda shape: pl.BlockSpec(shape, lambda s, c: (0, 0, 0))
    y, s_out = pl.pallas_call(
        _ret_kernel,
        grid=(n_seq, nch),
        in_specs=[
            pl.BlockSpec(memory_space=pltpu.SMEM),
            zspec(colblk["rq"]), zspec(colblk["rk"]), zspec(colblk["rv"]), zspec(colblk["rg"]),
            tspec, tspec,
            pl.BlockSpec((1, N_HEADS, DK_RET, DV_RET), lambda s, c: (s, 0, 0, 0)),
            pl.BlockSpec((1, w), lambda s, c: (0, 0)),
            const3((N_HEADS, L, L)), const3((N_HEADS, L, DK_RET)), const3((N_HEADS, L, DK_RET)),
        ],
        out_specs=[
            pl.BlockSpec((L, w), lambda s, c: (s * nch + c, 0)),
            pl.BlockSpec((1, N_HEADS, DK_RET, DV_RET), lambda s, c: (s, 0, 0, 0)),
        ],
        scratch_shapes=[pltpu.VMEM((N_HEADS, DK_RET, DV_RET), F32)],
        out_shape=[
            jax.ShapeDtypeStruct((n_seq * t_len, w), BF16),
            jax.ShapeDtypeStruct((n_seq, N_HEADS, DK_RET, DV_RET), F32),
        ],
        compiler_params=_cparams(("parallel", "arbitrary")),
        name="retention",
    )(cd, z, z, z, z, cos, sin, s0, g_gn, din, qd, kd)
    return y, s_out


def _retention_tables(L):
    log_g = np.log(1.0 - 2.0 ** (-5.0 - np.arange(N_HEADS, dtype=np.float64)))
    idx = np.arange(L, dtype=np.float64)
    rel = idx[:, None] - idx[None, :]
    din = np.where(rel >= 0, np.exp(rel[None] * log_g[:, None, None]), 0.0)
    qd = np.exp((idx[None, :] + 1.0) * log_g[:, None])[..., None] * np.ones((1, 1, DK_RET))
    kd = np.exp((L - 1.0 - idx[None, :]) * log_g[:, None])[..., None] * np.ones((1, 1, DK_RET))
    cd = np.exp(L * log_g)
    return (jnp.asarray(din, F32), jnp.asarray(qd, F32), jnp.asarray(kd, F32), jnp.asarray(cd, F32))


def _rotary_tables(pos0, t_len):
    half = DK_RET // 2
    inv = ROPE_BASE ** (-np.arange(half, dtype=np.float64) / half)
    ang = (pos0 + np.arange(t_len, dtype=np.float64))[:, None] * inv[None, :]
    c, s = np.cos(ang), np.sin(ang)
    return (jnp.asarray(np.concatenate([c, c], axis=-1), F32),
            jnp.asarray(np.concatenate([-s, s], axis=-1), F32))


def _ml_kernel(q_ref, k_ref, v_ref, mo_ref, gl_ref, glt_ref, c0_ref, n0_ref, m0_ref, g_ref,
               y_ref, cout_ref, nout_ref, mout_ref, c_scr, n_scr, m_scr):
    @pl.when(pl.program_id(1) == 0)
    def _():
        c_scr[...] = c0_ref[0]
        n_scr[...] = n0_ref[0]
        m_scr[...] = m0_ref[0]

    L = q_ref.shape[0]
    ii = lax.broadcasted_iota(I32, (L, L), 0)
    jj = lax.broadcasted_iota(I32, (L, L), 1)
    causal = jj <= ii
    gl = gl_ref[...]
    glt = glt_ref[0]
    m_all = m_scr[...]
    n_all = n_scr[...]
    m_rows, n_rows = [], []
    for h in range(N_HEADS):
        qs = slice(h * DK_ML, (h + 1) * DK_ML)
        vs = slice(h * DV_ML, (h + 1) * DV_ML)
        qf = q_ref[:, qs] * (DK_ML ** -0.5)
        kf = k_ref[:, qs]
        vf = v_ref[:, vs]
        q = qf.astype(BF16)
        k = kf.astype(BF16)
        ig_c = gl[:, h:h + 1]
        lf_c = gl[:, N_HEADS + h:N_HEADS + h + 1]
        ig_r = glt[h:h + 1, :]
        lf_r = glt[N_HEADS + h:N_HEADS + h + 1, :]
        b_c = jnp.sum(jnp.where(causal, jnp.broadcast_to(lf_r, (L, L)), 0.0), axis=1, keepdims=True)
        b_r = jnp.sum(jnp.where(ii <= jj, jnp.broadcast_to(lf_c, (L, L)), 0.0), axis=0, keepdims=True)
        logw = jnp.where(causal, b_c - b_r + ig_r, -jnp.inf)
        m_prev = m_all[h:h + 1, 0:1]
        inter = b_c + m_prev
        m_row = jnp.maximum(inter, jnp.max(logw, axis=1, keepdims=True))
        qk = lax.dot_general(q, k, NT_DIMS, preferred_element_type=F32)
        w = jnp.exp(logw - m_row) * qk
        w_inter = jnp.exp(inter - m_row)
        c_old = c_scr[h]
        n_old = n_all[h:h + 1, :]
        num = (jnp.dot(w.astype(BF16), vf.astype(BF16), preferred_element_type=F32)
               + w_inter * lax.dot_general(q, c_old.astype(BF16), NT_DIMS, preferred_element_type=F32))
        den = (jnp.sum(w, axis=1, keepdims=True)
               + w_inter * jnp.sum(qf * n_old, axis=1, keepdims=True))
        hh = num / jnp.maximum(jnp.abs(den), jnp.exp(-m_row))
        m_new = m_row[L - 1:L, :]
        b_last = b_c[L - 1:L, :]
        decay = jnp.exp(b_last + m_prev - m_new)
        wk_c = jnp.exp(b_last - b_c + ig_c - m_new)
        c_new = decay * c_old + jnp.dot((vf * wk_c).T.astype(BF16), k, preferred_element_type=F32)
        n_new = decay * n_old + jnp.sum(wk_c * kf, axis=0, keepdims=True)
        m_rows.append(jnp.broadcast_to(m_new, (1, LANES)))
        n_rows.append(n_new)
        c_scr[h] = c_new
        cout_ref[0, h] = c_new
        y = _sigmoid(mo_ref[:, vs]) * _head_norm(hh, g_ref[:, vs])
        y_ref[:, vs] = y.astype(BF16)
    m_out = jnp.concatenate(m_rows, axis=0)
    n_out = jnp.concatenate(n_rows, axis=0)
    m_scr[...] = m_out
    n_scr[...] = n_out
    mout_ref[0] = m_out
    nout_ref[0] = n_out


def _mlstm(z, gl, glt, row0, n_seq, t_len, L, c0, n0, m0, g_gn, colblk):
    nch = t_len // L
    rb0 = row0 // L
    wqk = N_HEADS * DK_ML
    wv = N_HEADS * DV_ML
    qspec = lambda cb: pl.BlockSpec((L, wqk), lambda s, c: (rb0 + s * nch + c, cb))
    vspec = lambda cb: pl.BlockSpec((L, wv), lambda s, c: (rb0 + s * nch + c, cb))
    m0b = jnp.broadcast_to(m0[..., None], (n_seq, N_HEADS, LANES))
    glt = glt[:, row0:row0 + n_seq * t_len].reshape(2 * N_HEADS, n_seq * nch, L).transpose(1, 0, 2)
    y, c_out, n_out, m_out = pl.pallas_call(
        _ml_kernel,
        grid=(n_seq, nch),
        in_specs=[
            qspec(colblk["mq"]), qspec(colblk["mk"]), vspec(colblk["mv"]), vspec(colblk["mo"]),
            pl.BlockSpec((L, LANES), lambda s, c: (rb0 + s * nch + c, 0)),
            pl.BlockSpec((1, 2 * N_HEADS, L), lambda s, c: (s * nch + c, 0, 0)),
            pl.BlockSpec((1, N_HEADS, DV_ML, DK_ML), lambda s, c: (s, 0, 0, 0)),
            pl.BlockSpec((1, N_HEADS, DK_ML), lambda s, c: (s, 0, 0)),
            pl.BlockSpec((1, N_HEADS, LANES), lambda s, c: (s, 0, 0)),
            pl.BlockSpec((1, wv), lambda s, c: (0, 0)),
        ],
        out_specs=[
            pl.BlockSpec((L, wv), lambda s, c: (s * nch + c, 0)),
            pl.BlockSpec((1, N_HEADS, DV_ML, DK_ML), lambda s, c: (s, 0, 0, 0)),
            pl.BlockSpec((1, N_HEADS, DK_ML), lambda s, c: (s, 0, 0)),
            pl.BlockSpec((1, N_HEADS, LANES), lambda s, c: (s, 0, 0)),
        ],
        out_shape=[
            jax.ShapeDtypeStruct((n_seq * t_len, wv), BF16),
            jax.ShapeDtypeStruct((n_seq, N_HEADS, DV_ML, DK_ML), F32),
            jax.ShapeDtypeStruct((n_seq, N_HEADS, DK_ML), F32),
            jax.ShapeDtypeStruct((n_seq, N_HEADS, LANES), F32),
        ],
        scratch_shapes=[
            pltpu.VMEM((N_HEADS, DV_ML, DK_ML), F32),
            pltpu.VMEM((N_HEADS, DK_ML), F32),
            pltpu.VMEM((N_HEADS, LANES), F32),
        ],
        compiler_params=_cparams(("parallel", "arbitrary")),
        name="mlstm",
    )(z, z, z, z, gl, glt, c0, n0, m0b, g_gn)
    return y, c_out, n_out, m_out[..., 0]


def _pack_bf16_pairs(ub):
    c = ub.shape[1] // 2
    bits = lax.bitcast_convert_type(ub.astype(F32), U32)
    return (bits[:, :c] >> 16) | (bits[:, c:] & jnp.uint32(0xFFFF0000))


def _unpack_bf16_pairs(w):
    lo = lax.bitcast_convert_type(w << 16, F32).astype(BF16)
    hi = lax.bitcast_convert_type(w & jnp.uint32(0xFFFF0000), F32).astype(BF16)
    return lo, hi


def _merge_kernel(x_ref, ya_ref, yb_ref, ga_ref, gb_ref, wa_ref, wb_ref, wo_ref, g2_ref,
                  wrt_ref, brt_ref, h_ref, u2p_ref, lgt_ref):
    bra = jnp.dot(ya_ref[...], wa_ref[...], preferred_element_type=F32)
    brb = jnp.dot(yb_ref[...], wb_ref[...], preferred_element_type=F32)
    merged = _sigmoid(ga_ref[...]) * bra + _sigmoid(gb_ref[...]) * brb
    hmid = x_ref[...] + jnp.dot(merged.astype(BF16), wo_ref[...], preferred_element_type=F32)
    h_ref[...] = hmid
    ub = _rms(hmid, g2_ref[...]).astype(BF16)
    lgt_ref[...] = lax.dot_general(wrt_ref[...], ub, NT_DIMS, preferred_element_type=F32) + brt_ref[...]
    u2p_ref[...] = _pack_bf16_pairs(ub)


def _merge(x, ya, yb, z, wa, wb, wo, g2, wrt, brt, *, tm, colblk):
    n, d = x.shape
    e = wrt.shape[0]
    full = lambda a: pl.BlockSpec(a.shape, lambda i: (0,) * a.ndim, pipeline_mode=pl.Buffered(1))
    return pl.pallas_call(
        _merge_kernel,
        grid=(n // tm,),
        in_specs=[
            pl.BlockSpec((tm, d), lambda i: (i, 0)),
            pl.BlockSpec((tm, ya.shape[1]), lambda i: (i, 0)),
            pl.BlockSpec((tm, yb.shape[1]), lambda i: (i, 0)),
            pl.BlockSpec((tm, d), lambda i: (i, colblk["ga"])),
            pl.BlockSpec((tm, d), lambda i: (i, colblk["gb"])),
            full(wa), full(wb), full(wo), full(g2), full(wrt), full(brt),
        ],
        out_specs=[
            pl.BlockSpec((tm, d), lambda i: (i, 0)),
            pl.BlockSpec((tm, d // 2), lambda i: (i, 0)),
            pl.BlockSpec((e, tm), lambda i: (0, i)),
        ],
        out_shape=[
            jax.ShapeDtypeStruct((n, d), F32),
            jax.ShapeDtypeStruct((n, d // 2), U32),
            jax.ShapeDtypeStruct((e, n), F32),
        ],
        compiler_params=_cparams(("parallel",)),
        name="merge",
    )(x, ya, yb, z, z, wa, wb, wo, g2, wrt, brt)


def _route_kernel(lg_ref, ti_ref, tp_ref):
    l = lg_ref[...]
    e = l.shape[0]
    eid = lax.broadcasted_iota(I32, l.shape, 0)
    vals, idxs = [], []
    for _ in range(TOP_K):
        mx = jnp.max(l, axis=0, keepdims=True)
        ix = jnp.min(jnp.where(l == mx, eid, e), axis=0, keepdims=True)
        vals.append(mx)
        idxs.append(ix)
        l = jnp.where(eid == ix, -jnp.inf, l)
    ex = [jnp.exp(v - vals[0]) for v in vals]
    tot = ex[0] + ex[1] + ex[2] + ex[3]
    for kk in range(TOP_K):
        ti_ref[kk:kk + 1, :] = idxs[kk]
        tp_ref[kk:kk + 1, :] = ex[kk] / tot


def _route(lgt, *, tn):
    e, n = lgt.shape
    return pl.pallas_call(
        _route_kernel,
        grid=(n // tn,),
        in_specs=[pl.BlockSpec((e, tn), lambda i: (0, i))],
        out_specs=[pl.BlockSpec((TOP_K, tn), lambda i: (0, i)),
                   pl.BlockSpec((TOP_K, tn), lambda i: (0, i))],
        out_shape=[jax.ShapeDtypeStruct((TOP_K, n), I32),
                   jax.ShapeDtypeStruct((TOP_K, n), F32)],
        compiler_params=_cparams(("parallel",)),
        name="route",
    )(lgt)


def _dispatch_kernel(pos_ref, u_ref, xs_in_ref, xs_ref, sem):
    del xs_in_ref
    tm = u_ref.shape[0]

    def row_copy(r, kk):
        return pltpu.make_async_copy(u_ref.at[pl.ds(r, 1)], xs_ref.at[pl.ds(pos_ref[kk, r], 1)], sem)

    def start(r, carry):
        for kk in range(TOP_K):
            row_copy(r, kk).start(priority=kk % 2)
        return carry

    lax.fori_loop(0, tm, start, 0, unroll=DMA_ISSUE_UNROLL)
    for kk in range(TOP_K):
        pltpu.make_async_copy(u_ref, xs_ref.at[pl.ds(0, tm)], sem).wait()


def _dispatch(pos, u2p, xs0, *, tm):
    n, c = u2p.shape
    n_slots = xs0.shape[0]
    return pl.pallas_call(
        _dispatch_kernel,
        grid=(n // tm,),
        in_specs=[
            pl.BlockSpec((TOP_K, tm), lambda i: (0, i), memory_space=pltpu.SMEM),
            pl.BlockSpec((tm, c), lambda i: (i, 0)),
            pl.BlockSpec(memory_space=pl.ANY),
        ],
        out_specs=pl.BlockSpec(memory_space=pl.ANY),
        out_shape=jax.ShapeDtypeStruct((n_slots, c), U32),
        scratch_shapes=[pltpu.SemaphoreType.DMA(())],
        input_output_aliases={2: 0},
        compiler_params=_cparams(("arbitrary",)),
        name="dispatch",
    )(pos, u2p, xs0)


def _stage_weights(first, te_ref, gn_ref, w_hbm, w_stage, w_bf16, sems):
    j, i = pl.program_id(0), pl.program_id(1)
    tw = w_bf16[0].shape[1]

    def copies(e, col):
        cols = pl.ds(pl.multiple_of(col * tw, tw), tw)
        return [pltpu.make_async_copy(w.at[e, :, cols], st, sems.at[k])
                for k, (w, st) in enumerate(zip(w_hbm, w_stage))]

    @pl.when(first)
    def _():
        e = te_ref[i]

        @pl.when(jnp.logical_and(i == 0, j == 0))
        def _():
            for cp in copies(e, j):
                cp.start()

        for cp in copies(e, j):
            cp.wait()
        for st, bf in zip(w_stage, w_bf16):
            bf[...] = st[...].astype(BF16)

        nxt = gn_ref[i]

        @pl.when(nxt >= 0)
        def _():
            for cp in copies(nxt, j):
                cp.start()

        @pl.when(jnp.logical_and(nxt < 0, j + 1 < pl.num_programs(0)))
        def _():
            for cp in copies(te_ref[0], j + 1):
                cp.start()


def _expert_up_kernel(te_ref, nu_ref, gf_ref, gn_ref, xs_ref, wg_hbm, wu_hbm, bg_ref, bu_ref, h_ref,
                      wg_st, wu_st, wg_scr, wu_scr, sems):
    i = pl.program_id(1)
    valid = i < nu_ref[0]
    _stage_weights(jnp.logical_and(valid, gf_ref[i] == 1), te_ref, gn_ref,
                   (wg_hbm, wu_hbm), (wg_st, wu_st), (wg_scr, wu_scr), sems)

    @pl.when(valid)
    def _():
        lo, hi = _unpack_bf16_pairs(xs_ref[...])
        c = lo.shape[1]
        g = (jnp.dot(lo, wg_scr[:c, :], preferred_element_type=F32)
             + jnp.dot(hi, wg_scr[c:, :], preferred_element_type=F32) + bg_ref[0])
        up = (jnp.dot(lo, wu_scr[:c, :], preferred_element_type=F32)
              + jnp.dot(hi, wu_scr[c:, :], preferred_element_type=F32) + bu_ref[0])
        g = jnp.minimum(g, SWIGLU_LIMIT)
        up = jnp.clip(up, -SWIGLU_LIMIT, SWIGLU_LIMIT)
        h_ref[...] = (g * _sigmoid(SWIGLU_ALPHA * g) * (up + 1.0)).astype(BF16)

    @pl.when(jnp.logical_not(valid))
    def _():
        h_ref[...] = jnp.zeros_like(h_ref)


def _expert_up(tiles, xs, w_gate, w_up, b_gate, b_up, *, tr, tf):
    p, c = xs.shape
    e, d, f = w_gate.shape
    row = lambda j, i, te_, nu_, gf_, gn_: (jnp.minimum(i, nu_[0] - 1), 0)
    bmap = lambda j, i, te_, nu_, gf_, gn_: (te_[i], 0, j)
    return pl.pallas_call(
        _expert_up_kernel,
        grid_spec=pltpu.PrefetchScalarGridSpec(
            num_scalar_prefetch=4,
            grid=(f // tf, p // tr),
            in_specs=[
                pl.BlockSpec((tr, c), row),
                pl.BlockSpec(memory_space=pl.ANY),
                pl.BlockSpec(memory_space=pl.ANY),
                pl.BlockSpec((1, 1, tf), bmap),
                pl.BlockSpec((1, 1, tf), bmap),
            ],
            out_specs=pl.BlockSpec((tr, tf), lambda j, i, te_, nu_, gf_, gn_: (i, j)),
            scratch_shapes=[pltpu.VMEM((d, tf), F32), pltpu.VMEM((d, tf), F32),
                            pltpu.VMEM((d, tf), BF16), pltpu.VMEM((d, tf), BF16),
                            pltpu.SemaphoreType.DMA((2,))],
        ),
        out_shape=jax.ShapeDtypeStruct((p, f), BF16),
        compiler_params=_cparams(("arbitrary", "arbitrary")),
        name="expert_up",
    )(*tiles, xs, w_gate, w_up, b_gate.reshape(e, 1, f), b_up.reshape(e, 1, f))


def _expert_down_kernel(te_ref, nu_ref, gf_ref, gn_ref, h_ref, wd_hbm, bd_ref, y_ref, wd_st, wd_scr, sems):
    i = pl.program_id(1)
    valid = i < nu_ref[0]
    _stage_weights(jnp.logical_and(valid, gf_ref[i] == 1), te_ref, gn_ref, (wd_hbm,), (wd_st,), (wd_scr,), sems)

    @pl.when(valid)
    def _():
        y_ref[...] = jnp.dot(h_ref[...], wd_scr[...], preferred_element_type=F32) + bd_ref[0]

    @pl.when(jnp.logical_not(valid))
    def _():
        y_ref[...] = jnp.zeros_like(y_ref)


def _expert_down(tiles, hmid, w_down, b_down, *, tr, tn):
    p, f = hmid.shape
    e, _, d = w_down.shape
    return pl.pallas_call(
        _expert_down_kernel,
        grid_spec=pltpu.PrefetchScalarGridSpec(
            num_scalar_prefetch=4,
            grid=(d // tn, p // tr),
            in_specs=[
                pl.BlockSpec((tr, f), lambda j, i, te_, nu_, gf_, gn_: (jnp.minimum(i, nu_[0] - 1), 0)),
                pl.BlockSpec(memory_space=pl.ANY),
                pl.BlockSpec((1, 1, tn), lambda j, i, te_, nu_, gf_, gn_: (te_[i], 0, j)),
            ],
            out_specs=pl.BlockSpec((tr, tn), lambda j, i, te_, nu_, gf_, gn_: (i, j)),
            scratch_shapes=[pltpu.VMEM((f, tn), F32), pltpu.VMEM((f, tn), BF16),
                            pltpu.SemaphoreType.DMA((1,))],
        ),
        out_shape=jax.ShapeDtypeStruct((p, d), F32),
        compiler_params=_cparams(("arbitrary", "arbitrary")),
        name="expert_down",
    )(*tiles, hmid, w_down, b_down.reshape(e, 1, d))


def _combine_kernel(pos_ref, h_ref, p_ref, gf_ref, ys_ref, o_ref, buf, sem):
    tm = h_ref.shape[0]

    def row_copy(r, kk):
        return pltpu.make_async_copy(ys_ref.at[pl.ds(pos_ref[kk, r], 1)], buf.at[kk, pl.ds(r, 1)], sem)

    def start(r, carry):
        for kk in range(TOP_K):
            row_copy(r, kk).start(priority=kk % 2)
        return carry

    lax.fori_loop(0, tm, start, 0, unroll=DMA_ISSUE_UNROLL)
    for kk in range(TOP_K):
        pltpu.make_async_copy(ys_ref.at[pl.ds(0, tm)], buf.at[kk], sem).wait()
    p = p_ref[...]
    acc = h_ref[...]
    for kk in range(TOP_K):
        acc = acc + p[:, kk:kk + 1] * buf[kk]
    o_ref[...] = _rms(acc, gf_ref[...])


def _combine(pos, h, probs_t, g_final, ys, *, tm):
    n, d = h.shape
    return pl.pallas_call(
        _combine_kernel,
        grid=(n // tm,),
        in_specs=[
            pl.BlockSpec((TOP_K, tm), lambda i: (0, i), memory_space=pltpu.SMEM),
            pl.BlockSpec((tm, d), lambda i: (i, 0)),
            pl.BlockSpec((tm, TOP_K), lambda i: (i, 0)),
            pl.BlockSpec((1, d), lambda i: (0, 0)),
            pl.BlockSpec(memory_space=pl.ANY),
        ],
        out_specs=pl.BlockSpec((tm, d), lambda i: (i, 0)),
        out_shape=jax.ShapeDtypeStruct((n, d), F32),
        scratch_shapes=[pltpu.VMEM((TOP_K, tm, d), F32), pltpu.SemaphoreType.DMA(())],
        compiler_params=_cparams(("arbitrary",)),
        name="combine",
    )(pos, h, probs_t, g_final, ys)


def _slot_layout(top_i, n_experts, tr, n_tiles):
    onehot = jnp.sum((top_i[:, :, None] == jnp.arange(n_experts, dtype=I32)[None, None, :]).astype(I32), axis=0)
    rank = jnp.cumsum(onehot, axis=0) - onehot
    counts = jnp.sum(onehot, axis=0)
    cpad = ((counts + tr - 1) // tr) * tr
    cend = jnp.cumsum(cpad)
    off = cend - cpad
    pos = jnp.take_along_axis((off[None, :] + rank).T, top_i, axis=0)
    nused = (cend[-1] // tr).astype(I32)
    tile_row = jnp.arange(n_tiles, dtype=I32) * tr
    te = jnp.sum((cend[None, :] <= tile_row[:, None]).astype(I32), axis=1)
    te = jnp.minimum(te, n_experts - 1)
    used = jnp.arange(n_tiles) < nused
    te = jnp.where(used, te, te[nused - 1])
    gfirst = jnp.logical_and(used, jnp.concatenate([jnp.ones((1,), bool), te[1:] != te[:-1]])).astype(I32)
    eid = jnp.arange(n_experts, dtype=I32)
    later = jnp.logical_and(eid[None, :] > eid[:, None], counts[None, :] > 0)
    nxt = jnp.min(jnp.where(later, eid[None, :], n_experts), axis=1)
    gnext = jnp.where(nxt < n_experts, nxt, -1).astype(I32)[te]
    return pos.astype(I32), (te, nused.reshape(1), gfirst, gnext)


def kernel(x_prompt, x_sample, state_ret_S, state_mlstm_C, state_mlstm_n, state_mlstm_m, g_norm1, w_in, b_if,
           g_ret_gn, g_ml_gn, w_ret_br, w_ml_br, w_out, g_norm2, w_router, b_router, w_gate, b_gate, w_up, b_up,
           w_down, b_down, g_final):
    bp, tp, d = x_prompt.shape
    bs, ts, _ = x_sample.shape
    depth = w_in.shape[0]
    n_experts = w_router.shape[-1]
    n_p, n_s = bp * tp, bs * ts
    n = n_p + n_s

    ret_qk, ret_v = N_HEADS * DK_RET, N_HEADS * DV_RET
    ml_qk, ml_v = N_HEADS * DK_ML, N_HEADS * DV_ML
    o_mi = 2 * ret_qk + 2 * ret_v + 2 * ml_qk + 2 * ml_v
    o_ga = o_mi + 2 * N_HEADS
    colblk = {"ga": 0, "gb": 1}
    base = 2 * d
    colblk.update(rq=base // ret_qk, rk=(base + ret_qk) // ret_qk, rv=(base + 2 * ret_qk) // ret_v,
                  rg=(base + 2 * ret_qk + ret_v) // ret_v)
    mbase = base + 2 * ret_qk + 2 * ret_v
    colblk.update(mq=mbase // ml_qk, mk=(mbase + ml_qk) // ml_qk, mv=(mbase + 2 * ml_qk) // ml_v,
                  mo=(mbase + 2 * ml_qk + ml_v) // ml_v)

    assert depth == 1, "one layer: the combine kernel applies the final norm"
    tr = 256
    n_slots = TOP_K * n + n_experts * tr
    n_tiles = n_slots // tr
    assert n_p % tr == 0 and n_s % tr == 0

    wl_t = w_in[0].T
    w_gates_t = wl_t[o_ga:]
    wift = wl_t[o_mi:o_ga].astype(BF16)
    wif = jnp.pad(wift.T, ((0, 0), (0, LANES - 2 * N_HEADS)))
    bflat = b_if[0].reshape(1, 2 * N_HEADS)
    bif = jnp.pad(bflat, ((0, 0), (0, LANES - 2 * N_HEADS)))
    bift = bflat.reshape(2 * N_HEADS, 1)
    g1 = g_norm1[0].reshape(1, d)
    g_r = g_ret_gn[0].reshape(1, ret_v)
    g_m = g_ml_gn[0].reshape(1, ml_v)
    wa, wb, wo = w_ret_br[0].astype(BF16), w_ml_br[0].astype(BF16), w_out[0].astype(BF16)
    g2 = g_norm2[0].reshape(1, d)
    wrt = w_router[0].T.astype(BF16)
    brt = b_router[0].reshape(n_experts, 1)

    def mixers(x, n_seq, t_len, L, L_ml, pos0, s0, c0, n0, m0):
        assert t_len % L == 0 and t_len % L_ml == 0
        z, gl, glt = _inproj(x, g1, w_gates_t, wl_t, o_mi, wif, wift, bif, bift, tm=min(1024, x.shape[0]), tn=512)
        cos, sin = _rotary_tables(pos0, t_len)
        ya, s_new = _retention(z, 0, n_seq, t_len, L, cos, sin, s0, g_r, _retention_tables(L), colblk)
        yb, c_new, n_new, m_new = _mlstm(z, gl, glt, 0, n_seq, t_len, L_ml, c0, n0, m0, g_m, colblk)
        h, u2p, lgt = _merge(x, ya, yb, z, wa, wb, wo, g2, wrt, brt, tm=256, colblk=colblk)
        return h, u2p, lgt, (s_new, c_new, n_new, m_new)

    zeros = lambda *s: jnp.zeros(s, F32)
    h_p, u2p_p, lgt_p, st_p = mixers(
        x_prompt.reshape(n_p, d), bp, tp, min(512, tp), min(512, tp), 0,
        zeros(bp, N_HEADS, DK_RET, DV_RET), zeros(bp, N_HEADS, DV_ML, DK_ML), zeros(bp, N_HEADS, DK_ML),
        zeros(bp, N_HEADS))
    h_s, u2p_s, lgt_s, st_s = mixers(
        x_sample.reshape(n_s, d), bs, ts, min(64, ts), min(64, ts), PAST_LEN,
        state_ret_S[0].astype(F32), state_mlstm_C[0].astype(F32), state_mlstm_n[0].astype(F32),
        state_mlstm_m[0].astype(F32))

    top_i, top_p = _route(jnp.concatenate([lgt_p, lgt_s], axis=1), tn=min(2048, n_s))
    pos, tiles = _slot_layout(top_i, n_experts, tr, n_tiles)
    probs_t = top_p.T

    xs = _dispatch(pos[:, :n_p], u2p_p, jnp.zeros((n_slots, d // 2), U32), tm=256)
    xs = _dispatch(pos[:, n_p:], u2p_s, xs, tm=256)
    hmid = _expert_up(tiles, xs, w_gate[0], w_up[0], b_gate[0], b_up[0], tr=tr, tf=1024)
    ys = _expert_down(tiles, hmid, w_down[0], b_down[0], tr=tr, tn=d)

    gf = g_final.reshape(1, d)
    y_p = _combine(pos[:, :n_p], h_p, probs_t[:n_p], gf, ys, tm=256)
    y_s = _combine(pos[:, n_p:], h_s, probs_t[n_p:], gf, ys, tm=256)

    return (y_p.reshape(bp, tp, d), y_s.reshape(bs, ts, d),
            st_p[0][None], st_p[1][None], st_p[2][None], st_p[3][None],
            st_s[0][None], st_s[1][None], st_s[2][None], st_s[3][None])
```

```python
import functools

import jax
import jax.numpy as jnp
import numpy as np
from jax import lax
from jax.experimental import pallas as pl
from jax.experimental.pallas import tpu as pltpu

F32 = jnp.float32
BF16 = jnp.bfloat16
U32 = jnp.uint32
I32 = jnp.int32

N_HEADS = 8
DK_RET = 128
DV_RET = 128
DK_ML = 64
DV_ML = 128
TOP_K = 4
PAST_LEN = 1024
ROPE_BASE = 10000.0
NORM_EPS = 1e-6
SWIGLU_ALPHA = 1.702
SWIGLU_LIMIT = 7.0

LANES = 128
VMEM_LIMIT = 56 * 1024 * 1024

DMA_ISSUE_UNROLL = 8

NT_DIMS = (((1,), (1,)), ((), ()))


def _cparams(sem):
    return pltpu.CompilerParams(dimension_semantics=sem, vmem_limit_bytes=VMEM_LIMIT)


def _sigmoid(x):
    return 1.0 / (1.0 + jnp.exp(-x))


def _log_sigmoid(x):
    return jnp.minimum(x, 0.0) - jnp.log(1.0 + jnp.exp(-jnp.abs(x)))


def _rms(x, g):
    return x * lax.rsqrt(jnp.mean(x * x, axis=-1, keepdims=True) + NORM_EPS) * g


def _head_norm(o, g):
    mu = jnp.mean(o, axis=-1, keepdims=True)
    d = o - mu
    var = jnp.mean(d * d, axis=-1, keepdims=True)
    return d * lax.rsqrt(var + NORM_EPS) * g


def _inproj_kernel(x_ref, g_ref, w_ref, wif_ref, wift_ref, bif_ref, bift_ref,
                   z_ref, gl_ref, glt_ref, u_scr):
    @pl.when(pl.program_id(1) == 0)
    def _():
        ub = _rms(x_ref[...], g_ref[...]).astype(BF16)
        u_scr[...] = ub
        gp = jnp.dot(ub, wif_ref[...], preferred_element_type=F32) + bif_ref[...]
        col = lax.broadcasted_iota(I32, gp.shape, 1)
        gl_ref[...] = jnp.where(col < N_HEADS, gp, _log_sigmoid(gp))
        gpt = lax.dot_general(wift_ref[...], ub, NT_DIMS, preferred_element_type=F32) + bift_ref[...]
        row = lax.broadcasted_iota(I32, gpt.shape, 0)
        glt_ref[...] = jnp.where(row < N_HEADS, gpt, _log_sigmoid(gpt))

    z_ref[...] = lax.dot_general(u_scr[...], w_ref[...].astype(BF16), NT_DIMS, preferred_element_type=F32)


def _inproj(x, g1, w_in_t, n_main, gate_row0, n_gate, wif, wift, bif, bift, *, tm, tn):
    n, d = x.shape
    ng = n_gate // tn
    c = n_gate + n_main
    assert n_main % tn == 0 and n_gate % tn == 0
    q = 16
    assert gate_row0 % q == 0 and tn % q == 0
    w_row = lambda i, j: (q * jnp.where(j < ng, gate_row0 // q + j * (tn // q), (j - ng) * (tn // q)), 0)
    return pl.pallas_call(
        _inproj_kernel,
        grid=(n // tm, c // tn),
        in_specs=[
            pl.BlockSpec((tm, d), lambda i, j: (i, 0)),
            pl.BlockSpec((1, d), lambda i, j: (0, 0)),
            pl.BlockSpec((pl.Element(tn), pl.Element(d)), w_row),
            pl.BlockSpec((d, LANES), lambda i, j: (0, 0)),
            pl.BlockSpec((2 * N_HEADS, d), lambda i, j: (0, 0)),
            pl.BlockSpec((1, LANES), lambda i, j: (0, 0)),
            pl.BlockSpec((2 * N_HEADS, 1), lambda i, j: (0, 0)),
        ],
        out_specs=[
            pl.BlockSpec((tm, tn), lambda i, j: (i, j)),
            pl.BlockSpec((tm, LANES), lambda i, j: (i, 0)),
            pl.BlockSpec((2 * N_HEADS, tm), lambda i, j: (0, i)),
        ],
        out_shape=[
            jax.ShapeDtypeStruct((n, c), F32),
            jax.ShapeDtypeStruct((n, LANES), F32),
            jax.ShapeDtypeStruct((2 * N_HEADS, n), F32),
        ],
        scratch_shapes=[pltpu.VMEM((tm, d), BF16)],
        compiler_params=_cparams(("parallel", "arbitrary")),
        name="inproj",
    )(x, g1, w_in_t, wif, wift, bif, bift)


def _ret_kernel(cd_ref, q_ref, k_ref, v_ref, rg_ref, cos_ref, sin_ref, s0_ref, g_ref,
                din_ref, qd_ref, kd_ref, y_ref, sout_ref, s_scr):
    @pl.when(pl.program_id(1) == 0)
    def _():
        s_scr[...] = s0_ref[0]

    cos = cos_ref[...]
    sin = sin_ref[...]
    for h in range(N_HEADS):
        sl = slice(h * DK_RET, (h + 1) * DK_RET)
        q = q_ref[:, sl]
        k = k_ref[:, sl]
        q = q * cos + pltpu.roll(q, DK_RET // 2, 1) * sin
        k = (k * cos + pltpu.roll(k, DK_RET // 2, 1) * sin) * (DK_RET ** -0.5)
        v = v_ref[:, sl].astype(BF16)
        s = s_scr[h]
        att = lax.dot_general(q.astype(BF16), k.astype(BF16), NT_DIMS,
                              preferred_element_type=F32) * din_ref[h]
        o = (jnp.dot(att.astype(BF16), v, preferred_element_type=F32)
             + jnp.dot((q * qd_ref[h]).astype(BF16), s.astype(BF16), preferred_element_type=F32))
        s_new = cd_ref[h] * s + jnp.dot((k * kd_ref[h]).T.astype(BF16), v, preferred_element_type=F32)
        s_scr[h] = s_new
        sout_ref[0, h] = s_new
        rg = rg_ref[:, sl]
        y = _head_norm(o, g_ref[:, sl]) * (rg * _sigmoid(rg))
        y_ref[:, sl] = y.astype(BF16)


def _retention(z, row0, n_seq, t_len, L, cos, sin, s0, g_gn, tabs, colblk):
    din, qd, kd, cd = tabs
    nch = t_len // L
    rb0 = row0 // L
    w = N_HEADS * DK_RET
    zspec = lambda cb: pl.BlockSpec((L, w), lambda s, c: (rb0 + s * nch + c, cb))
    tspec = pl.BlockSpec((L, DK_RET), lambda s, c: (c, 0))
    const3 = lambda shape: pl.BlockSpec(shape, lambda s, c: (0, 0, 0))
    y, s_out = pl.pallas_call(
        _ret_kernel,
        grid=(n_seq, nch),
        in_specs=[
            pl.BlockSpec(memory_space=pltpu.SMEM),
            zspec(colblk["rq"]), zspec(colblk["rk"]), zspec(colblk["rv"]), zspec(colblk["rg"]),
            tspec, tspec,
            pl.BlockSpec((1, N_HEADS, DK_RET, DV_RET), lambda s, c: (s, 0, 0, 0)),
            pl.BlockSpec((1, w), lambda s, c: (0, 0)),
            const3((N_HEADS, L, L)), const3((N_HEADS, L, DK_RET)), const3((N_HEADS, L, DK_RET)),
        ],
        out_specs=[
            pl.BlockSpec((L, w), lambda s, c: (s * nch + c, 0)),
            pl.BlockSpec((1, N_HEADS, DK_RET, DV_RET), lambda s, c: (s, 0, 0, 0)),
        ],
        scratch_shapes=[pltpu.VMEM((N_HEADS, DK_RET, DV_RET), F32)],
        out_shape=[
            jax.ShapeDtypeStruct((n_seq * t_len, w), BF16),
            jax.ShapeDtypeStruct((n_seq, N_HEADS, DK_RET, DV_RET), F32),
        ],
        compiler_params=_cparams(("parallel", "arbitrary")),
        name="retention",
    )(cd, z, z, z, z, cos, sin, s0, g_gn, din, qd, kd)
    return y, s_out


def _retention_tables(L):
    log_g = np.log(1.0 - 2.0 ** (-5.0 - np.arange(N_HEADS, dtype=np.float64)))
    idx = np.arange(L, dtype=np.float64)
    rel = idx[:, None] - idx[None, :]
    din = np.where(rel >= 0, np.exp(rel[None] * log_g[:, None, None]), 0.0)
    qd = np.exp((idx[None, :] + 1.0) * log_g[:, None])[..., None] * np.ones((1, 1, DK_RET))
    kd = np.exp((L - 1.0 - idx[None, :]) * log_g[:, None])[..., None] * np.ones((1, 1, DK_RET))
    cd = np.exp(L * log_g)
    return (jnp.asarray(din, F32), jnp.asarray(qd, F32), jnp.asarray(kd, F32), jnp.asarray(cd, F32))


def _rotary_tables(pos0, t_len):
    half = DK_RET // 2
    inv = ROPE_BASE ** (-np.arange(half, dtype=np.float64) / half)
    ang = (pos0 + np.arange(t_len, dtype=np.float64))[:, None] * inv[None, :]
    c, s = np.cos(ang), np.sin(ang)
    return (jnp.asarray(np.concatenate([c, c], axis=-1), F32),
            jnp.asarray(np.concatenate([-s, s], axis=-1), F32))


def _ml_kernel(q_ref, k_ref, v_ref, mo_ref, gl_ref, glt_ref, c0_ref, n0_ref, m0_ref, g_ref,
               y_ref, cout_ref, nout_ref, mout_ref, c_scr, n_scr, m_scr):
    @pl.when(pl.program_id(1) == 0)
    def _():
        c_scr[...] = c0_ref[0]
        n_scr[...] = n0_ref[0]
        m_scr[...] = m0_ref[0]

    L = q_ref.shape[0]
    ii = lax.broadcasted_iota(I32, (L, L), 0)
    jj = lax.broadcasted_iota(I32, (L, L), 1)
    causal = jj <= ii
    gl = gl_ref[...]
    glt = glt_ref[0]
    m_all = m_scr[...]
    n_all = n_scr[...]
    m_rows, n_rows = [], []
    for h in range(N_HEADS):
        qs = slice(h * DK_ML, (h + 1) * DK_ML)
        vs = slice(h * DV_ML, (h + 1) * DV_ML)
        qf = q_ref[:, qs] * (DK_ML ** -0.5)
        kf = k_ref[:, qs]
        vf = v_ref[:, vs]
        q = qf.astype(BF16)
        k = kf.astype(BF16)
        ig_c = gl[:, h:h + 1]
        lf_c = gl[:, N_HEADS + h:N_HEADS + h + 1]
        ig_r = glt[h:h + 1, :]
        lf_r = glt[N_HEADS + h:N_HEADS + h + 1, :]
        b_c = jnp.sum(jnp.where(causal, jnp.broadcast_to(lf_r, (L, L)), 0.0), axis=1, keepdims=True)
        b_r = jnp.sum(jnp.where(ii <= jj, jnp.broadcast_to(lf_c, (L, L)), 0.0), axis=0, keepdims=True)
        logw = jnp.where(causal, b_c - b_r + ig_r, -jnp.inf)
        m_prev = m_all[h:h + 1, 0:1]
        inter = b_c + m_prev
        m_row = jnp.maximum(inter, jnp.max(logw, axis=1, keepdims=True))
        qk = lax.dot_general(q, k, NT_DIMS, preferred_element_type=F32)
        w = jnp.exp(logw - m_row) * qk
        w_inter = jnp.exp(inter - m_row)
        c_old = c_scr[h]
        n_old = n_all[h:h + 1, :]
        num = (jnp.dot(w.astype(BF16), vf.astype(BF16), preferred_element_type=F32)
               + w_inter * lax.dot_general(q, c_old.astype(BF16), NT_DIMS, preferred_element_type=F32))
        den = (jnp.sum(w, axis=1, keepdims=True)
               + w_inter * jnp.sum(qf * n_old, axis=1, keepdims=True))
        hh = num / jnp.maximum(jnp.abs(den), jnp.exp(-m_row))
        m_new = m_row[L - 1:L, :]
        b_last = b_c[L - 1:L, :]
        decay = jnp.exp(b_last + m_prev - m_new)
        wk_c = jnp.exp(b_last - b_c + ig_c - m_new)
        c_new = decay * c_old + jnp.dot((vf * wk_c).T.astype(BF16), k, preferred_element_type=F32)
        n_new = decay * n_old + jnp.sum(wk_c * kf, axis=0, keepdims=True)
        m_rows.append(jnp.broadcast_to(m_new, (1, LANES)))
        n_rows.append(n_new)
        c_scr[h] = c_new
        cout_ref[0, h] = c_new
        y = _sigmoid(mo_ref[:, vs]) * _head_norm(hh, g_ref[:, vs])
        y_ref[:, vs] = y.astype(BF16)
    m_out = jnp.concatenate(m_rows, axis=0)
    n_out = jnp.concatenate(n_rows, axis=0)
    m_scr[...] = m_out
    n_scr[...] = n_out
    mout_ref[0] = m_out
    nout_ref[0] = n_out


def _mlstm(z, gl, glt, row0, n_seq, t_len, L, c0, n0, m0, g_gn, colblk):
    nch = t_len // L
    rb0 = row0 // L
    wqk = N_HEADS * DK_ML
    wv = N_HEADS * DV_ML
    qspec = lambda cb: pl.BlockSpec((L, wqk), lambda s, c: (rb0 + s * nch + c, cb))
    vspec = lambda cb: pl.BlockSpec((L, wv), lambda s, c: (rb0 + s * nch + c, cb))
    m0b = jnp.broadcast_to(m0[..., None], (n_seq, N_HEADS, LANES))
    glt = glt[:, row0:row0 + n_seq * t_len].reshape(2 * N_HEADS, n_seq * nch, L).transpose(1, 0, 2)
    y, c_out, n_out, m_out = pl.pallas_call(
        _ml_kernel,
        grid=(n_seq, nch),
        in_specs=[
            qspec(colblk["mq"]), qspec(colblk["mk"]), vspec(colblk["mv"]), vspec(colblk["mo"]),
            pl.BlockSpec((L, LANES), lambda s, c: (rb0 + s * nch + c, 0)),
            pl.BlockSpec((1, 2 * N_HEADS, L), lambda s, c: (s * nch + c, 0, 0)),
            pl.BlockSpec((1, N_HEADS, DV_ML, DK_ML), lambda s, c: (s, 0, 0, 0)),
            pl.BlockSpec((1, N_HEADS, DK_ML), lambda s, c: (s, 0, 0)),
            pl.BlockSpec((1, N_HEADS, LANES), lambda s, c: (s, 0, 0)),
            pl.BlockSpec((1, wv), lambda s, c: (0, 0)),
        ],
        out_specs=[
            pl.BlockSpec((L, wv), lambda s, c: (s * nch + c, 0)),
            pl.BlockSpec((1, N_HEADS, DV_ML, DK_ML), lambda s, c: (s, 0, 0, 0)),
            pl.BlockSpec((1, N_HEADS, DK_ML), lambda s, c: (s, 0, 0)),
            pl.BlockSpec((1, N_HEADS, LANES), lambda s, c: (s, 0, 0)),
        ],
        out_shape=[
            jax.ShapeDtypeStruct((n_seq * t_len, wv), BF16),
            jax.ShapeDtypeStruct((n_seq, N_HEADS, DV_ML, DK_ML), F32),
            jax.ShapeDtypeStruct((n_seq, N_HEADS, DK_ML), F32),
            jax.ShapeDtypeStruct((n_seq, N_HEADS, LANES), F32),
        ],
        scratch_shapes=[
            pltpu.VMEM((N_HEADS, DV_ML, DK_ML), F32),
            pltpu.VMEM((N_HEADS, DK_ML), F32),
            pltpu.VMEM((N_HEADS, LANES), F32),
        ],
        compiler_params=_cparams(("parallel", "arbitrary")),
        name="mlstm",
    )(z, z, z, z, gl, glt, c0, n0, m0b, g_gn)
    return y, c_out, n_out, m_out[..., 0]


def _pack_bf16_pairs(ub):
    c = ub.shape[1] // 2
    bits = lax.bitcast_convert_type(ub.astype(F32), U32)
    return (bits[:, :c] >> 16) | (bits[:, c:] & jnp.uint32(0xFFFF0000))


def _unpack_bf16_pairs(w):
    lo = lax.bitcast_convert_type(w << 16, F32).astype(BF16)
    hi = lax.bitcast_convert_type(w & jnp.uint32(0xFFFF0000), F32).astype(BF16)
    return lo, hi


def _merge_kernel(x_ref, ya_ref, yb_ref, ga_ref, gb_ref, wa_ref, wb_ref, wo_ref, g2_ref,
                  wrt_ref, brt_ref, h_ref, u2p_ref, lgt_ref):
    bra = jnp.dot(ya_ref[...], wa_ref[...], preferred_element_type=F32)
    brb = jnp.dot(yb_ref[...], wb_ref[...], preferred_element_type=F32)
    merged = _sigmoid(ga_ref[...]) * bra + _sigmoid(gb_ref[...]) * brb
    hmid = x_ref[...] + jnp.dot(merged.astype(BF16), wo_ref[...], preferred_element_type=F32)
    h_ref[...] = hmid
    ub = _rms(hmid, g2_ref[...]).astype(BF16)
    lgt_ref[...] = lax.dot_general(wrt_ref[...], ub, NT_DIMS, preferred_element_type=F32) + brt_ref[...]
    u2p_ref[...] = _pack_bf16_pairs(ub)


def _merge(x, ya, yb, z, wa, wb, wo, g2, wrt, brt, *, tm, colblk):
    n, d = x.shape
    e = wrt.shape[0]
    full = lambda a: pl.BlockSpec(a.shape, lambda i: (0,) * a.ndim, pipeline_mode=pl.Buffered(1))
    return pl.pallas_call(
        _merge_kernel,
        grid=(n // tm,),
        in_specs=[
            pl.BlockSpec((tm, d), lambda i: (i, 0)),
            pl.BlockSpec((tm, ya.shape[1]), lambda i: (i, 0)),
            pl.BlockSpec((tm, yb.shape[1]), lambda i: (i, 0)),
            pl.BlockSpec((tm, d), lambda i: (i, colblk["ga"])),
            pl.BlockSpec((tm, d), lambda i: (i, colblk["gb"])),
            full(wa), full(wb), full(wo), full(g2), full(wrt), full(brt),
        ],
        out_specs=[
            pl.BlockSpec((tm, d), lambda i: (i, 0)),
            pl.BlockSpec((tm, d // 2), lambda i: (i, 0)),
            pl.BlockSpec((e, tm), lambda i: (0, i)),
        ],
        out_shape=[
            jax.ShapeDtypeStruct((n, d), F32),
            jax.ShapeDtypeStruct((n, d // 2), U32),
            jax.ShapeDtypeStruct((e, n), F32),
        ],
        compiler_params=_cparams(("parallel",)),
        name="merge",
    )(x, ya, yb, z, z, wa, wb, wo, g2, wrt, brt)


def _route_kernel(lg_ref, ti_ref, tp_ref):
    l = lg_ref[...]
    e = l.shape[0]
    eid = lax.broadcasted_iota(I32, l.shape, 0)
    vals, idxs = [], []
    for _ in range(TOP_K):
        mx = jnp.max(l, axis=0, keepdims=True)
        ix = jnp.min(jnp.where(l == mx, eid, e), axis=0, keepdims=True)
        vals.append(mx)
        idxs.append(ix)
        l = jnp.where(eid == ix, -jnp.inf, l)
    ex = [jnp.exp(v - vals[0]) for v in vals]
    tot = ex[0] + ex[1] + ex[2] + ex[3]
    for kk in range(TOP_K):
        ti_ref[kk:kk + 1, :] = idxs[kk]
        tp_ref[kk:kk + 1, :] = ex[kk] / tot


def _route(lgt, *, tn):
    e, n = lgt.shape
    return pl.pallas_call(
        _route_kernel,
        grid=(n // tn,),
        in_specs=[pl.BlockSpec((e, tn), lambda i: (0, i))],
        out_specs=[pl.BlockSpec((TOP_K, tn), lambda i: (0, i)),
                   pl.BlockSpec((TOP_K, tn), lambda i: (0, i))],
        out_shape=[jax.ShapeDtypeStruct((TOP_K, n), I32),
                   jax.ShapeDtypeStruct((TOP_K, n), F32)],
        compiler_params=_cparams(("parallel",)),
        name="route",
    )(lgt)


def _dispatch_kernel(pos_ref, u_ref, xs_in_ref, xs_ref, sem):
    del xs_in_ref
    tm = u_ref.shape[0]

    def row_copy(r, kk):
        return pltpu.make_async_copy(u_ref.at[pl.ds(r, 1)], xs_ref.at[pl.ds(pos_ref[kk, r], 1)], sem)

    def start(r, carry):
        for kk in range(TOP_K):
            row_copy(r, kk).start(priority=kk % 2)
        return carry

    lax.fori_loop(0, tm, start, 0, unroll=DMA_ISSUE_UNROLL)
    for kk in range(TOP_K):
        pltpu.make_async_copy(u_ref, xs_ref.at[pl.ds(0, tm)], sem).wait()


def _dispatch(pos, u2p, xs0, *, tm):
    n, c = u2p.shape
    n_slots = xs0.shape[0]
    return pl.pallas_call(
        _dispatch_kernel,
        grid=(n // tm,),
        in_specs=[
            pl.BlockSpec((TOP_K, tm), lambda i: (0, i), memory_space=pltpu.SMEM),
            pl.BlockSpec((tm, c), lambda i: (i, 0)),
            pl.BlockSpec(memory_space=pl.ANY),
        ],
        out_specs=pl.BlockSpec(memory_space=pl.ANY),
        out_shape=jax.ShapeDtypeStruct((n_slots, c), U32),
        scratch_shapes=[pltpu.SemaphoreType.DMA(())],
        input_output_aliases={2: 0},
        compiler_params=_cparams(("arbitrary",)),
        name="dispatch",
    )(pos, u2p, xs0)


def _stage_weights(first, te_ref, gn_ref, w_hbm, w_stage, w_bf16, sems):
    j, i = pl.program_id(0), pl.program_id(1)
    tw = w_bf16[0].shape[1]

    def copies(e, col):
        cols = pl.ds(pl.multiple_of(col * tw, tw), tw)
        return [pltpu.make_async_copy(w.at[e, :, cols], st, sems.at[k])
                for k, (w, st) in enumerate(zip(w_hbm, w_stage))]

    @pl.when(first)
    def _():
        e = te_ref[i]

        @pl.when(jnp.logical_and(i == 0, j == 0))
        def _():
            for cp in copies(e, j):
                cp.start()

        for cp in copies(e, j):
            cp.wait()
        for st, bf in zip(w_stage, w_bf16):
            bf[...] = st[...].astype(BF16)

        nxt = gn_ref[i]

        @pl.when(nxt >= 0)
        def _():
            for cp in copies(nxt, j):
                cp.start()

        @pl.when(jnp.logical_and(nxt < 0, j + 1 < pl.num_programs(0)))
        def _():
            for cp in copies(te_ref[0], j + 1):
                cp.start()


def _for_row_count(nv, out_ref, body):
    tr = out_ref.shape[0]
    half = tr // 2

    @pl.when(nv > half)
    def _():
        body(tr)

    @pl.when(nv <= half)
    def _():
        @pl.when(nv > 0)
        def _():
            body(half)

        @pl.when(nv <= 0)
        def _():
            out_ref[:half, :] = jnp.zeros((half, out_ref.shape[1]), out_ref.dtype)

        out_ref[half:, :] = jnp.zeros((tr - half, out_ref.shape[1]), out_ref.dtype)


def _expert_up_kernel(te_ref, nu_ref, gf_ref, gn_ref, nv_ref, xs_ref, wg_hbm, wu_hbm, bg_ref, bu_ref, h_ref,
                      wg_st, wu_st, wg_scr, wu_scr, sems):
    del nu_ref
    i = pl.program_id(1)
    _stage_weights(gf_ref[i] == 1, te_ref, gn_ref, (wg_hbm, wu_hbm), (wg_st, wu_st), (wg_scr, wu_scr), sems)

    def body(rows):
        lo, hi = _unpack_bf16_pairs(xs_ref[:rows, :])
        c = lo.shape[1]
        g = (jnp.dot(lo, wg_scr[:c, :], preferred_element_type=F32)
             + jnp.dot(hi, wg_scr[c:, :], preferred_element_type=F32) + bg_ref[0])
        up = (jnp.dot(lo, wu_scr[:c, :], preferred_element_type=F32)
              + jnp.dot(hi, wu_scr[c:, :], preferred_element_type=F32) + bu_ref[0])
        g = jnp.minimum(g, SWIGLU_LIMIT)
        up = jnp.clip(up, -SWIGLU_LIMIT, SWIGLU_LIMIT)
        h_ref[:rows, :] = (g * _sigmoid(SWIGLU_ALPHA * g) * (up + 1.0)).astype(BF16)

    _for_row_count(nv_ref[i], h_ref, body)


def _expert_up(tiles, xs, w_gate, w_up, b_gate, b_up, *, tr, tf):
    p, c = xs.shape
    e, d, f = w_gate.shape
    row = lambda j, i, te_, nu_, *_: (jnp.minimum(i, nu_[0] - 1), 0)
    bmap = lambda j, i, te_, *_: (te_[i], 0, j)
    return pl.pallas_call(
        _expert_up_kernel,
        grid_spec=pltpu.PrefetchScalarGridSpec(
            num_scalar_prefetch=len(tiles),
            grid=(f // tf, p // tr),
            in_specs=[
                pl.BlockSpec((tr, c), row),
                pl.BlockSpec(memory_space=pl.ANY),
                pl.BlockSpec(memory_space=pl.ANY),
                pl.BlockSpec((1, 1, tf), bmap),
                pl.BlockSpec((1, 1, tf), bmap),
            ],
            out_specs=pl.BlockSpec((tr, tf), lambda j, i, *_: (i, j)),
            scratch_shapes=[pltpu.VMEM((d, tf), F32), pltpu.VMEM((d, tf), F32),
                            pltpu.VMEM((d, tf), BF16), pltpu.VMEM((d, tf), BF16),
                            pltpu.SemaphoreType.DMA((2,))],
        ),
        out_shape=jax.ShapeDtypeStruct((p, f), BF16),
        compiler_params=_cparams(("arbitrary", "arbitrary")),
        name="expert_up",
    )(*tiles, xs, w_gate, w_up, b_gate.reshape(e, 1, f), b_up.reshape(e, 1, f))


def _expert_down_kernel(te_ref, nu_ref, gf_ref, gn_ref, nv_ref, h_ref, wd_hbm, bd_ref, y_ref,
                        wd_st, wd_scr, sems):
    del nu_ref
    i = pl.program_id(1)
    _stage_weights(gf_ref[i] == 1, te_ref, gn_ref, (wd_hbm,), (wd_st,), (wd_scr,), sems)

    def body(rows):
        y_ref[:rows, :] = jnp.dot(h_ref[:rows, :], wd_scr[...], preferred_element_type=F32) + bd_ref[0]

    _for_row_count(nv_ref[i], y_ref, body)


def _expert_down(tiles, hmid, w_down, b_down, *, tr, tn):
    p, f = hmid.shape
    e, _, d = w_down.shape
    return pl.pallas_call(
        _expert_down_kernel,
        grid_spec=pltpu.PrefetchScalarGridSpec(
            num_scalar_prefetch=len(tiles),
            grid=(d // tn, p // tr),
            in_specs=[
                pl.BlockSpec((tr, f), lambda j, i, te_, nu_, *_: (jnp.minimum(i, nu_[0] - 1), 0)),
                pl.BlockSpec(memory_space=pl.ANY),
                pl.BlockSpec((1, 1, tn), lambda j, i, te_, *_: (te_[i], 0, j)),
            ],
            out_specs=pl.BlockSpec((tr, tn), lambda j, i, *_: (i, j)),
            scratch_shapes=[pltpu.VMEM((f, tn), F32), pltpu.VMEM((f, tn), BF16),
                            pltpu.SemaphoreType.DMA((1,))],
        ),
        out_shape=jax.ShapeDtypeStruct((p, d), F32),
        compiler_params=_cparams(("arbitrary", "arbitrary")),
        name="expert_down",
    )(*tiles, hmid, w_down, b_down.reshape(e, 1, d))


def _combine_kernel(pos_ref, h_ref, p_ref, gf_ref, ys_ref, o_ref, buf, sem):
    tm = h_ref.shape[0]

    def row_copy(r, kk):
        return pltpu.make_async_copy(ys_ref.at[pl.ds(pos_ref[kk, r], 1)], buf.at[kk, pl.ds(r, 1)], sem)

    def start(r, carry):
        for kk in range(TOP_K):
            row_copy(r, kk).start(priority=kk % 2)
        return carry

    lax.fori_loop(0, tm, start, 0, unroll=DMA_ISSUE_UNROLL)
    for kk in range(TOP_K):
        pltpu.make_async_copy(ys_ref.at[pl.ds(0, tm)], buf.at[kk], sem).wait()
    p = p_ref[...]
    acc = h_ref[...]
    for kk in range(TOP_K):
        acc = acc + p[:, kk:kk + 1] * buf[kk]
    o_ref[...] = _rms(acc, gf_ref[...])


def _combine(pos, h, probs_t, g_final, ys, *, tm):
    n, d = h.shape
    return pl.pallas_call(
        _combine_kernel,
        grid=(n // tm,),
        in_specs=[
            pl.BlockSpec((TOP_K, tm), lambda i: (0, i), memory_space=pltpu.SMEM),
            pl.BlockSpec((tm, d), lambda i: (i, 0)),
            pl.BlockSpec((tm, TOP_K), lambda i: (i, 0)),
            pl.BlockSpec((1, d), lambda i: (0, 0)),
            pl.BlockSpec(memory_space=pl.ANY),
        ],
        out_specs=pl.BlockSpec((tm, d), lambda i: (i, 0)),
        out_shape=jax.ShapeDtypeStruct((n, d), F32),
        scratch_shapes=[pltpu.VMEM((TOP_K, tm, d), F32), pltpu.SemaphoreType.DMA(())],
        compiler_params=_cparams(("arbitrary",)),
        name="combine",
    )(pos, h, probs_t, g_final, ys)


def _slot_layout(top_i, n_experts, tr, n_tiles):
    onehot = jnp.sum((top_i[:, :, None] == jnp.arange(n_experts, dtype=I32)[None, None, :]).astype(I32), axis=0)
    rank = jnp.cumsum(onehot, axis=0) - onehot
    counts = jnp.sum(onehot, axis=0)
    cpad = ((counts + tr - 1) // tr) * tr
    cend = jnp.cumsum(cpad)
    off = cend - cpad
    pos = jnp.take_along_axis((off[None, :] + rank).T, top_i, axis=0)
    nused = (cend[-1] // tr).astype(I32)
    tile_row = jnp.arange(n_tiles, dtype=I32) * tr
    te = jnp.sum((cend[None, :] <= tile_row[:, None]).astype(I32), axis=1)
    te = jnp.minimum(te, n_experts - 1)
    used = jnp.arange(n_tiles) < nused
    te = jnp.where(used, te, te[nused - 1])
    gfirst = jnp.logical_and(used, jnp.concatenate([jnp.ones((1,), bool), te[1:] != te[:-1]])).astype(I32)
    eid = jnp.arange(n_experts, dtype=I32)
    later = jnp.logical_and(eid[None, :] > eid[:, None], counts[None, :] > 0)
    nxt = jnp.min(jnp.where(later, eid[None, :], n_experts), axis=1)
    gnext = jnp.where(nxt < n_experts, nxt, -1).astype(I32)[te]
    nvalid = jnp.where(used, jnp.clip(counts[te] - (tile_row - off[te]), 0, tr), 0).astype(I32)
    return pos.astype(I32), (te, nused.reshape(1), gfirst, gnext, nvalid)


def kernel(x_prompt, x_sample, state_ret_S, state_mlstm_C, state_mlstm_n, state_mlstm_m, g_norm1, w_in, b_if,
           g_ret_gn, g_ml_gn, w_ret_br, w_ml_br, w_out, g_norm2, w_router, b_router, w_gate, b_gate, w_up, b_up,
           w_down, b_down, g_final):
    bp, tp, d = x_prompt.shape
    bs, ts, _ = x_sample.shape
    depth = w_in.shape[0]
    n_experts = w_router.shape[-1]
    n_p, n_s = bp * tp, bs * ts
    n = n_p + n_s

    ret_qk, ret_v = N_HEADS * DK_RET, N_HEADS * DV_RET
    ml_qk, ml_v = N_HEADS * DK_ML, N_HEADS * DV_ML
    o_mi = 2 * ret_qk + 2 * ret_v + 2 * ml_qk + 2 * ml_v
    o_ga = o_mi + 2 * N_HEADS
    colblk = {"ga": 0, "gb": 1}
    base = 2 * d
    colblk.update(rq=base // ret_qk, rk=(base + ret_qk) // ret_qk, rv=(base + 2 * ret_qk) // ret_v,
                  rg=(base + 2 * ret_qk + ret_v) // ret_v)
    mbase = base + 2 * ret_qk + 2 * ret_v
    colblk.update(mq=mbase // ml_qk, mk=(mbase + ml_qk) // ml_qk, mv=(mbase + 2 * ml_qk) // ml_v,
                  mo=(mbase + 2 * ml_qk + ml_v) // ml_v)

    assert depth == 1, "one layer: the combine kernel applies the final norm"
    tr = 512
    n_slots = TOP_K * n + n_experts * tr
    n_tiles = n_slots // tr
    tm = 256
    assert n_p % tm == 0 and n_s % tm == 0

    wl_t = w_in[0].T
    wift = wl_t[o_mi:o_ga].astype(BF16)
    wif = jnp.pad(wift.T, ((0, 0), (0, LANES - 2 * N_HEADS)))
    bflat = b_if[0].reshape(1, 2 * N_HEADS)
    bif = jnp.pad(bflat, ((0, 0), (0, LANES - 2 * N_HEADS)))
    bift = bflat.reshape(2 * N_HEADS, 1)
    g1 = g_norm1[0].reshape(1, d)
    g_r = g_ret_gn[0].reshape(1, ret_v)
    g_m = g_ml_gn[0].reshape(1, ml_v)
    wa, wb, wo = w_ret_br[0].astype(BF16), w_ml_br[0].astype(BF16), w_out[0].astype(BF16)
    g2 = g_norm2[0].reshape(1, d)
    wrt = w_router[0].T.astype(BF16)
    brt = b_router[0].reshape(n_experts, 1)

    def mixers(x, n_seq, t_len, L, L_ml, pos0, s0, c0, n0, m0):
        assert t_len % L == 0 and t_len % L_ml == 0
        z, gl, glt = _inproj(x, g1, wl_t, o_mi, o_ga, 2 * d, wif, wift, bif, bift,
                             tm=min(1024, x.shape[0]), tn=1024)
        cos, sin = _rotary_tables(pos0, t_len)
        ya, s_new = _retention(z, 0, n_seq, t_len, L, cos, sin, s0, g_r, _retention_tables(L), colblk)
        yb, c_new, n_new, m_new = _mlstm(z, gl, glt, 0, n_seq, t_len, L_ml, c0, n0, m0, g_m, colblk)
        h, u2p, lgt = _merge(x, ya, yb, z, wa, wb, wo, g2, wrt, brt, tm=tm, colblk=colblk)
        return h, u2p, lgt, (s_new, c_new, n_new, m_new)

    zeros = lambda *s: jnp.zeros(s, F32)
    h_p, u2p_p, lgt_p, st_p = mixers(
        x_prompt.reshape(n_p, d), bp, tp, min(512, tp), min(512, tp), 0,
        zeros(bp, N_HEADS, DK_RET, DV_RET), zeros(bp, N_HEADS, DV_ML, DK_ML), zeros(bp, N_HEADS, DK_ML),
        zeros(bp, N_HEADS))
    h_s, u2p_s, lgt_s, st_s = mixers(
        x_sample.reshape(n_s, d), bs, ts, min(64, ts), min(64, ts), PAST_LEN,
        state_ret_S[0].astype(F32), state_mlstm_C[0].astype(F32), state_mlstm_n[0].astype(F32),
        state_mlstm_m[0].astype(F32))

    top_i, top_p = _route(jnp.concatenate([lgt_p, lgt_s], axis=1), tn=min(2048, n_s))
    pos, tiles = _slot_layout(top_i, n_experts, tr, n_tiles)
    probs_t = top_p.T

    xs = _dispatch(pos[:, :n_p], u2p_p, jnp.zeros((n_slots, d // 2), U32), tm=tm)
    xs = _dispatch(pos[:, n_p:], u2p_s, xs, tm=tm)
    hmid = _expert_up(tiles, xs, w_gate[0], w_up[0], b_gate[0], b_up[0], tr=tr, tf=1024)
    ys = _expert_down(tiles, hmid, w_down[0], b_down[0], tr=tr, tn=d)

    gf = g_final.reshape(1, d)
    y_p = _combine(pos[:, :n_p], h_p, probs_t[:n_p], gf, ys, tm=tm)
    y_s = _combine(pos[:, n_p:], h_s, probs_t[n_p:], gf, ys, tm=tm)

    return (y_p.reshape(bp, tp, d), y_s.reshape(bs, ts, d),
            st_p[0][None], st_p[1][None], st_p[2][None], st_p[3][None],
            st_s[0][None], st_s[1][None], st_s[2][None], st_s[3][None])
```

```python
import functools

import jax
import jax.numpy as jnp
import numpy as np
from jax import lax
from jax.experimental import pallas as pl
from jax.experimental.pallas import tpu as pltpu

F32 = jnp.float32
BF16 = jnp.bfloat16
U32 = jnp.uint32
I32 = jnp.int32

N_HEADS = 8
DK_RET = 128
DV_RET = 128
DK_ML = 64
DV_ML = 128
TOP_K = 4
PAST_LEN = 1024
ROPE_BASE = 10000.0
NORM_EPS = 1e-6
SWIGLU_ALPHA = 1.702
SWIGLU_LIMIT = 7.0

LANES = 128
VMEM_LIMIT = 56 * 1024 * 1024

DMA_ISSUE_UNROLL = 8

NT_DIMS = (((1,), (1,)), ((), ()))


def _cparams(sem):
    return pltpu.CompilerParams(dimension_semantics=sem, vmem_limit_bytes=VMEM_LIMIT)


def _sigmoid(x):
    return 1.0 / (1.0 + jnp.exp(-x))


def _log_sigmoid(x):
    return jnp.minimum(x, 0.0) - jnp.log(1.0 + jnp.exp(-jnp.abs(x)))


def _rms(x, g):
    return x * lax.rsqrt(jnp.mean(x * x, axis=-1, keepdims=True) + NORM_EPS) * g


def _head_norm(o, g):
    mu = jnp.mean(o, axis=-1, keepdims=True)
    d = o - mu
    var = jnp.mean(d * d, axis=-1, keepdims=True)
    return d * lax.rsqrt(var + NORM_EPS) * g


def _inproj_kernel(x_ref, g_ref, w_ref, wif_ref, wift_ref, bif_ref, bift_ref,
                   z_ref, gl_ref, glt_ref, u_scr):
    @pl.when(pl.program_id(1) == 0)
    def _():
        ub = _rms(x_ref[...], g_ref[...]).astype(BF16)
        u_scr[...] = ub
        gp = jnp.dot(ub, wif_ref[...], preferred_element_type=F32) + bif_ref[...]
        col = lax.broadcasted_iota(I32, gp.shape, 1)
        gl_ref[...] = jnp.where(col < N_HEADS, gp, _log_sigmoid(gp))
        gpt = lax.dot_general(wift_ref[...], ub, NT_DIMS, preferred_element_type=F32) + bift_ref[...]
        row = lax.broadcasted_iota(I32, gpt.shape, 0)
        glt_ref[...] = jnp.where(row < N_HEADS, gpt, _log_sigmoid(gpt))

    z_ref[...] = lax.dot_general(u_scr[...], w_ref[...].astype(BF16), NT_DIMS, preferred_element_type=F32)


def _inproj(x, g1, w_in_t, n_main, gate_row0, n_gate, wif, wift, bif, bift, *, tm, tn):
    n, d = x.shape
    ng = n_gate // tn
    c = n_gate + n_main
    assert n_main % tn == 0 and n_gate % tn == 0
    q = 16
    assert gate_row0 % q == 0 and tn % q == 0
    w_row = lambda i, j: (q * jnp.where(j < ng, gate_row0 // q + j * (tn // q), (j - ng) * (tn // q)), 0)
    return pl.pallas_call(
        _inproj_kernel,
        grid=(n // tm, c // tn),
        in_specs=[
            pl.BlockSpec((tm, d), lambda i, j: (i, 0)),
            pl.BlockSpec((1, d), lambda i, j: (0, 0)),
            pl.BlockSpec((pl.Element(tn), pl.Element(d)), w_row),
            pl.BlockSpec((d, LANES), lambda i, j: (0, 0)),
            pl.BlockSpec((2 * N_HEADS, d), lambda i, j: (0, 0)),
            pl.BlockSpec((1, LANES), lambda i, j: (0, 0)),
            pl.BlockSpec((2 * N_HEADS, 1), lambda i, j: (0, 0)),
        ],
        out_specs=[
            pl.BlockSpec((tm, tn), lambda i, j: (i, j)),
            pl.BlockSpec((tm, LANES), lambda i, j: (i, 0)),
            pl.BlockSpec((2 * N_HEADS, tm), lambda i, j: (0, i)),
        ],
        out_shape=[
            jax.ShapeDtypeStruct((n, c), F32),
            jax.ShapeDtypeStruct((n, LANES), F32),
            jax.ShapeDtypeStruct((2 * N_HEADS, n), F32),
        ],
        scratch_shapes=[pltpu.VMEM((tm, d), BF16)],
        compiler_params=_cparams(("parallel", "arbitrary")),
        name="inproj",
    )(x, g1, w_in_t, wif, wift, bif, bift)


def _ret_kernel(cd_ref, q_ref, k_ref, v_ref, rg_ref, cos_ref, sin_ref, s0_ref, g_ref,
                din_ref, qd_ref, kd_ref, y_ref, sout_ref, s_scr):
    @pl.when(pl.program_id(1) == 0)
    def _():
        s_scr[...] = s0_ref[0]

    cos = cos_ref[...]
    sin = sin_ref[...]
    heads = range(N_HEADS)
    sl = [slice(h * DK_RET, (h + 1) * DK_RET) for h in heads]
    q = [q_ref[:, sl[h]] for h in heads]
    k = [k_ref[:, sl[h]] for h in heads]
    q = [x * cos + pltpu.roll(x, DK_RET // 2, 1) * sin for x in q]
    k = [(x * cos + pltpu.roll(x, DK_RET // 2, 1) * sin) * (DK_RET ** -0.5) for x in k]
    v = [v_ref[:, sl[h]].astype(BF16) for h in heads]
    s = [s_scr[h] for h in heads]
    att = [lax.dot_general(q[h].astype(BF16), k[h].astype(BF16), NT_DIMS, preferred_element_type=F32) * din_ref[h]
           for h in heads]
    o = [jnp.dot(att[h].astype(BF16), v[h], preferred_element_type=F32)
         + jnp.dot((q[h] * qd_ref[h]).astype(BF16), s[h].astype(BF16), preferred_element_type=F32)
         for h in heads]
    s_new = [cd_ref[h] * s[h] + jnp.dot((k[h] * kd_ref[h]).T.astype(BF16), v[h], preferred_element_type=F32)
             for h in heads]
    for h in heads:
        s_scr[h] = s_new[h]
        sout_ref[0, h] = s_new[h]
        rg = rg_ref[:, sl[h]]
        y = _head_norm(o[h], g_ref[:, sl[h]]) * (rg * _sigmoid(rg))
        y_ref[:, sl[h]] = y.astype(BF16)


def _retention(z, row0, n_seq, t_len, L, cos, sin, s0, g_gn, tabs, colblk):
    din, qd, kd, cd = tabs
    nch = t_len // L
    rb0 = row0 // L
    w = N_HEADS * DK_RET
    zspec = lambda cb: pl.BlockSpec((L, w), lambda s, c: (rb0 + s * nch + c, cb))
    tspec = pl.BlockSpec((L, DK_RET), lambda s, c: (c, 0))
    const3 = lambda shape: pl.BlockSpec(shape, lambda s, c: (0, 0, 0))
    y, s_out = pl.pallas_call(
        _ret_kernel,
        grid=(n_seq, nch),
        in_specs=[
            pl.BlockSpec(memory_space=pltpu.SMEM),
            zspec(colblk["rq"]), zspec(colblk["rk"]), zspec(colblk["rv"]), zspec(colblk["rg"]),
            tspec, tspec,
            pl.BlockSpec((1, N_HEADS, DK_RET, DV_RET), lambda s, c: (s, 0, 0, 0)),
            pl.BlockSpec((1, w), lambda s, c: (0, 0)),
            const3((N_HEADS, L, L)), const3((N_HEADS, L, DK_RET)), const3((N_HEADS, L, DK_RET)),
        ],
        out_specs=[
            pl.BlockSpec((L, w), lambda s, c: (s * nch + c, 0)),
            pl.BlockSpec((1, N_HEADS, DK_RET, DV_RET), lambda s, c: (s, 0, 0, 0)),
        ],
        scratch_shapes=[pltpu.VMEM((N_HEADS, DK_RET, DV_RET), F32)],
        out_shape=[
            jax.ShapeDtypeStruct((n_seq * t_len, w), BF16),
            jax.ShapeDtypeStruct((n_seq, N_HEADS, DK_RET, DV_RET), F32),
        ],
        compiler_params=_cparams(("parallel", "arbitrary")),
        name="retention",
    )(cd, z, z, z, z, cos, sin, s0, g_gn, din, qd, kd)
    return y, s_out


def _retention_tables(L):
    log_g = np.log(1.0 - 2.0 ** (-5.0 - np.arange(N_HEADS, dtype=np.float64)))
    idx = np.arange(L, dtype=np.float64)
    rel = idx[:, None] - idx[None, :]
    din = np.where(rel >= 0, np.exp(rel[None] * log_g[:, None, None]), 0.0)
    qd = np.exp((idx[None, :] + 1.0) * log_g[:, None])[..., None] * np.ones((1, 1, DK_RET))
    kd = np.exp((L - 1.0 - idx[None, :]) * log_g[:, None])[..., None] * np.ones((1, 1, DK_RET))
    cd = np.exp(L * log_g)
    return (jnp.asarray(din, F32), jnp.asarray(qd, F32), jnp.asarray(kd, F32), jnp.asarray(cd, F32))


def _rotary_tables(pos0, t_len):
    half = DK_RET // 2
    inv = ROPE_BASE ** (-np.arange(half, dtype=np.float64) / half)
    ang = (pos0 + np.arange(t_len, dtype=np.float64))[:, None] * inv[None, :]
    c, s = np.cos(ang), np.sin(ang)
    return (jnp.asarray(np.concatenate([c, c], axis=-1), F32),
            jnp.asarray(np.concatenate([-s, s], axis=-1), F32))


def _ml_kernel(q_ref, k_ref, v_ref, mo_ref, gl_ref, glt_ref, c0_ref, n0_ref, m0_ref, g_ref,
               y_ref, cout_ref, nout_ref, mout_ref, c_scr, n_scr, m_scr):
    @pl.when(pl.program_id(1) == 0)
    def _():
        c_scr[...] = c0_ref[0]
        n_scr[...] = n0_ref[0]
        m_scr[...] = m0_ref[0]

    L = q_ref.shape[0]
    ii = lax.broadcasted_iota(I32, (L, L), 0)
    jj = lax.broadcasted_iota(I32, (L, L), 1)
    causal = jj <= ii
    gl = gl_ref[...]
    glt = glt_ref[0]
    m_all = m_scr[...]
    n_all = n_scr[...]
    heads = range(N_HEADS)
    qs = [slice(h * DK_ML, (h + 1) * DK_ML) for h in heads]
    vs = [slice(h * DV_ML, (h + 1) * DV_ML) for h in heads]
    qf = [q_ref[:, qs[h]] * (DK_ML ** -0.5) for h in heads]
    kf = [k_ref[:, qs[h]] for h in heads]
    vf = [v_ref[:, vs[h]] for h in heads]
    q = [x.astype(BF16) for x in qf]
    k = [x.astype(BF16) for x in kf]
    ig_c = [gl[:, h:h + 1] for h in heads]
    ig_r = [glt[h:h + 1, :] for h in heads]
    b_c = [jnp.sum(jnp.where(causal, jnp.broadcast_to(glt[N_HEADS + h:N_HEADS + h + 1, :], (L, L)), 0.0),
                   axis=1, keepdims=True) for h in heads]
    b_r = [jnp.sum(jnp.where(ii <= jj, jnp.broadcast_to(gl[:, N_HEADS + h:N_HEADS + h + 1], (L, L)), 0.0),
                   axis=0, keepdims=True) for h in heads]
    logw = [jnp.where(causal, b_c[h] - b_r[h] + ig_r[h], -jnp.inf) for h in heads]
    m_prev = [m_all[h:h + 1, 0:1] for h in heads]
    inter = [b_c[h] + m_prev[h] for h in heads]
    m_row = [jnp.maximum(inter[h], jnp.max(logw[h], axis=1, keepdims=True)) for h in heads]
    qk = [lax.dot_general(q[h], k[h], NT_DIMS, preferred_element_type=F32) for h in heads]
    w = [jnp.exp(logw[h] - m_row[h]) * qk[h] for h in heads]
    w_inter = [jnp.exp(inter[h] - m_row[h]) for h in heads]
    c_old = [c_scr[h] for h in heads]
    n_old = [n_all[h:h + 1, :] for h in heads]
    num = [jnp.dot(w[h].astype(BF16), vf[h].astype(BF16), preferred_element_type=F32)
           + w_inter[h] * lax.dot_general(q[h], c_old[h].astype(BF16), NT_DIMS, preferred_element_type=F32)
           for h in heads]
    den = [jnp.sum(w[h], axis=1, keepdims=True) + w_inter[h] * jnp.sum(qf[h] * n_old[h], axis=1, keepdims=True)
           for h in heads]
    hh = [num[h] / jnp.maximum(jnp.abs(den[h]), jnp.exp(-m_row[h])) for h in heads]
    m_new = [m_row[h][L - 1:L, :] for h in heads]
    b_last = [b_c[h][L - 1:L, :] for h in heads]
    decay = [jnp.exp(b_last[h] + m_prev[h] - m_new[h]) for h in heads]
    wk_c = [jnp.exp(b_last[h] - b_c[h] + ig_c[h] - m_new[h]) for h in heads]
    c_new = [decay[h] * c_old[h] + jnp.dot((vf[h] * wk_c[h]).T.astype(BF16), k[h], preferred_element_type=F32)
             for h in heads]
    n_new = [decay[h] * n_old[h] + jnp.sum(wk_c[h] * kf[h], axis=0, keepdims=True) for h in heads]
    for h in heads:
        c_scr[h] = c_new[h]
        cout_ref[0, h] = c_new[h]
        y = _sigmoid(mo_ref[:, vs[h]]) * _head_norm(hh[h], g_ref[:, vs[h]])
        y_ref[:, vs[h]] = y.astype(BF16)
    m_out = jnp.concatenate([jnp.broadcast_to(m, (1, LANES)) for m in m_new], axis=0)
    n_out = jnp.concatenate(n_new, axis=0)
    m_scr[...] = m_out
    n_scr[...] = n_out
    mout_ref[0] = m_out
    nout_ref[0] = n_out


def _mlstm(z, gl, glt, row0, n_seq, t_len, L, c0, n0, m0, g_gn, colblk):
    nch = t_len // L
    rb0 = row0 // L
    wqk = N_HEADS * DK_ML
    wv = N_HEADS * DV_ML
    qspec = lambda cb: pl.BlockSpec((L, wqk), lambda s, c: (rb0 + s * nch + c, cb))
    vspec = lambda cb: pl.BlockSpec((L, wv), lambda s, c: (rb0 + s * nch + c, cb))
    m0b = jnp.broadcast_to(m0[..., None], (n_seq, N_HEADS, LANES))
    glt = glt[:, row0:row0 + n_seq * t_len].reshape(2 * N_HEADS, n_seq * nch, L).transpose(1, 0, 2)
    y, c_out, n_out, m_out = pl.pallas_call(
        _ml_kernel,
        grid=(n_seq, nch),
        in_specs=[
            qspec(colblk["mq"]), qspec(colblk["mk"]), vspec(colblk["mv"]), vspec(colblk["mo"]),
            pl.BlockSpec((L, LANES), lambda s, c: (rb0 + s * nch + c, 0)),
            pl.BlockSpec((1, 2 * N_HEADS, L), lambda s, c: (s * nch + c, 0, 0)),
            pl.BlockSpec((1, N_HEADS, DV_ML, DK_ML), lambda s, c: (s, 0, 0, 0)),
            pl.BlockSpec((1, N_HEADS, DK_ML), lambda s, c: (s, 0, 0)),
            pl.BlockSpec((1, N_HEADS, LANES), lambda s, c: (s, 0, 0)),
            pl.BlockSpec((1, wv), lambda s, c: (0, 0)),
        ],
        out_specs=[
            pl.BlockSpec((L, wv), lambda s, c: (s * nch + c, 0)),
            pl.BlockSpec((1, N_HEADS, DV_ML, DK_ML), lambda s, c: (s, 0, 0, 0)),
            pl.BlockSpec((1, N_HEADS, DK_ML), lambda s, c: (s, 0, 0)),
            pl.BlockSpec((1, N_HEADS, LANES), lambda s, c: (s, 0, 0)),
        ],
        out_shape=[
            jax.ShapeDtypeStruct((n_seq * t_len, wv), BF16),
            jax.ShapeDtypeStruct((n_seq, N_HEADS, DV_ML, DK_ML), F32),
            jax.ShapeDtypeStruct((n_seq, N_HEADS, DK_ML), F32),
            jax.ShapeDtypeStruct((n_seq, N_HEADS, LANES), F32),
        ],
        scratch_shapes=[
            pltpu.VMEM((N_HEADS, DV_ML, DK_ML), F32),
            pltpu.VMEM((N_HEADS, DK_ML), F32),
            pltpu.VMEM((N_HEADS, LANES), F32),
        ],
        compiler_params=_cparams(("parallel", "arbitrary")),
        name="mlstm",
    )(z, z, z, z, gl, glt, c0, n0, m0b, g_gn)
    return y, c_out, n_out, m_out[..., 0]


def _pack_bf16_pairs(ub):
    c = ub.shape[1] // 2
    bits = lax.bitcast_convert_type(ub.astype(F32), U32)
    return (bits[:, :c] >> 16) | (bits[:, c:] & jnp.uint32(0xFFFF0000))


def _unpack_bf16_pairs(w):
    lo = lax.bitcast_convert_type(w << 16, F32).astype(BF16)
    hi = lax.bitcast_convert_type(w & jnp.uint32(0xFFFF0000), F32).astype(BF16)
    return lo, hi


def _merge_kernel(x_ref, ya_ref, yb_ref, ga_ref, gb_ref, wa_ref, wb_ref, wo_ref, g2_ref,
                  wrt_ref, brt_ref, h_ref, u2p_ref, lgt_ref):
    bra = jnp.dot(ya_ref[...], wa_ref[...], preferred_element_type=F32)
    brb = jnp.dot(yb_ref[...], wb_ref[...], preferred_element_type=F32)
    merged = _sigmoid(ga_ref[...]) * bra + _sigmoid(gb_ref[...]) * brb
    hmid = x_ref[...] + jnp.dot(merged.astype(BF16), wo_ref[...], preferred_element_type=F32)
    h_ref[...] = hmid
    ub = _rms(hmid, g2_ref[...]).astype(BF16)
    lgt_ref[...] = lax.dot_general(wrt_ref[...], ub, NT_DIMS, preferred_element_type=F32) + brt_ref[...]
    u2p_ref[...] = _pack_bf16_pairs(ub)


def _merge(x, ya, yb, z, wa, wb, wo, g2, wrt, brt, *, tm, colblk):
    n, d = x.shape
    e = wrt.shape[0]
    full = lambda a: pl.BlockSpec(a.shape, lambda i: (0,) * a.ndim, pipeline_mode=pl.Buffered(1))
    return pl.pallas_call(
        _merge_kernel,
        grid=(n // tm,),
        in_specs=[
            pl.BlockSpec((tm, d), lambda i: (i, 0)),
            pl.BlockSpec((tm, ya.shape[1]), lambda i: (i, 0)),
            pl.BlockSpec((tm, yb.shape[1]), lambda i: (i, 0)),
            pl.BlockSpec((tm, d), lambda i: (i, colblk["ga"])),
            pl.BlockSpec((tm, d), lambda i: (i, colblk["gb"])),
            full(wa), full(wb), full(wo), full(g2), full(wrt), full(brt),
        ],
        out_specs=[
            pl.BlockSpec((tm, d), lambda i: (i, 0)),
            pl.BlockSpec((tm, d // 2), lambda i: (i, 0)),
            pl.BlockSpec((e, tm), lambda i: (0, i)),
        ],
        out_shape=[
            jax.ShapeDtypeStruct((n, d), F32),
            jax.ShapeDtypeStruct((n, d // 2), U32),
            jax.ShapeDtypeStruct((e, n), F32),
        ],
        compiler_params=_cparams(("parallel",)),
        name="merge",
    )(x, ya, yb, z, z, wa, wb, wo, g2, wrt, brt)


def _route_kernel(lg_ref, ti_ref, tp_ref):
    l = lg_ref[...]
    e = l.shape[0]
    eid = lax.broadcasted_iota(I32, l.shape, 0)
    vals, idxs = [], []
    for _ in range(TOP_K):
        mx = jnp.max(l, axis=0, keepdims=True)
        ix = jnp.min(jnp.where(l == mx, eid, e), axis=0, keepdims=True)
        vals.append(mx)
        idxs.append(ix)
        l = jnp.where(eid == ix, -jnp.inf, l)
    ex = [jnp.exp(v - vals[0]) for v in vals]
    tot = ex[0] + ex[1] + ex[2] + ex[3]
    for kk in range(TOP_K):
        ti_ref[kk:kk + 1, :] = idxs[kk]
        tp_ref[kk:kk + 1, :] = ex[kk] / tot


def _route(lgt, *, tn):
    e, n = lgt.shape
    return pl.pallas_call(
        _route_kernel,
        grid=(n // tn,),
        in_specs=[pl.BlockSpec((e, tn), lambda i: (0, i))],
        out_specs=[pl.BlockSpec((TOP_K, tn), lambda i: (0, i)),
                   pl.BlockSpec((TOP_K, tn), lambda i: (0, i))],
        out_shape=[jax.ShapeDtypeStruct((TOP_K, n), I32),
                   jax.ShapeDtypeStruct((TOP_K, n), F32)],
        compiler_params=_cparams(("parallel",)),
        name="route",
    )(lgt)


def _dispatch_kernel(pos_ref, u_ref, xs_in_ref, xs_ref, sem):
    del xs_in_ref
    tm = u_ref.shape[0]

    def row_copy(r, kk):
        return pltpu.make_async_copy(u_ref.at[pl.ds(r, 1)], xs_ref.at[pl.ds(pos_ref[kk, r], 1)], sem)

    def start(r, carry):
        for kk in range(TOP_K):
            row_copy(r, kk).start(priority=kk % 2)
        return carry

    lax.fori_loop(0, tm, start, 0, unroll=DMA_ISSUE_UNROLL)
    for kk in range(TOP_K):
        pltpu.make_async_copy(u_ref, xs_ref.at[pl.ds(0, tm)], sem).wait()


def _dispatch(pos, u2p, xs0, *, tm):
    n, c = u2p.shape
    n_slots = xs0.shape[0]
    return pl.pallas_call(
        _dispatch_kernel,
        grid=(n // tm,),
        in_specs=[
            pl.BlockSpec((TOP_K, tm), lambda i: (0, i), memory_space=pltpu.SMEM),
            pl.BlockSpec((tm, c), lambda i: (i, 0)),
            pl.BlockSpec(memory_space=pl.ANY),
        ],
        out_specs=pl.BlockSpec(memory_space=pl.ANY),
        out_shape=jax.ShapeDtypeStruct((n_slots, c), U32),
        scratch_shapes=[pltpu.SemaphoreType.DMA(())],
        input_output_aliases={2: 0},
        compiler_params=_cparams(("arbitrary",)),
        name="dispatch",
    )(pos, u2p, xs0)


def _stage_weights(first, te_ref, gn_ref, w_hbm, w_stage, w_bf16, sems):
    j, i = pl.program_id(0), pl.program_id(1)
    tw = w_bf16[0].shape[1]

    def copies(e, col):
        cols = pl.ds(pl.multiple_of(col * tw, tw), tw)
        return [pltpu.make_async_copy(w.at[e, :, cols], st, sems.at[k])
                for k, (w, st) in enumerate(zip(w_hbm, w_stage))]

    @pl.when(first)
    def _():
        e = te_ref[i]

        @pl.when(jnp.logical_and(i == 0, j == 0))
        def _():
            for cp in copies(e, j):
                cp.start()

        for cp in copies(e, j):
            cp.wait()
        for st, bf in zip(w_stage, w_bf16):
            bf[...] = st[...].astype(BF16)

        nxt = gn_ref[i]

        @pl.when(nxt >= 0)
        def _():
            for cp in copies(nxt, j):
                cp.start()

        @pl.when(jnp.logical_and(nxt < 0, j + 1 < pl.num_programs(0)))
        def _():
            for cp in copies(te_ref[0], j + 1):
                cp.start()


def _for_row_count(nv, out_ref, body):
    tr = out_ref.shape[0]
    half = tr // 2

    @pl.when(nv > half)
    def _():
        body(tr)

    @pl.when(nv <= half)
    def _():
        @pl.when(nv > 0)
        def _():
            body(half)

        @pl.when(nv <= 0)
        def _():
            out_ref[:half, :] = jnp.zeros((half, out_ref.shape[1]), out_ref.dtype)

        out_ref[half:, :] = jnp.zeros((tr - half, out_ref.shape[1]), out_ref.dtype)


def _expert_up_kernel(te_ref, nu_ref, gf_ref, gn_ref, nv_ref, xs_ref, wg_hbm, wu_hbm, bg_ref, bu_ref, h_ref,
                      wg_st, wu_st, wg_scr, wu_scr, sems):
    del nu_ref
    i = pl.program_id(1)
    _stage_weights(gf_ref[i] == 1, te_ref, gn_ref, (wg_hbm, wu_hbm), (wg_st, wu_st), (wg_scr, wu_scr), sems)

    def body(rows):
        lo, hi = _unpack_bf16_pairs(xs_ref[:rows, :])
        c = lo.shape[1]
        g = (jnp.dot(lo, wg_scr[:c, :], preferred_element_type=F32)
             + jnp.dot(hi, wg_scr[c:, :], preferred_element_type=F32) + bg_ref[0])
        up = (jnp.dot(lo, wu_scr[:c, :], preferred_element_type=F32)
              + jnp.dot(hi, wu_scr[c:, :], preferred_element_type=F32) + bu_ref[0])
        g = jnp.minimum(g, SWIGLU_LIMIT)
        up = jnp.clip(up, -SWIGLU_LIMIT, SWIGLU_LIMIT)
        h_ref[:rows, :] = (g * _sigmoid(SWIGLU_ALPHA * g) * (up + 1.0)).astype(BF16)

    _for_row_count(nv_ref[i], h_ref, body)


def _expert_up(tiles, xs, w_gate, w_up, b_gate, b_up, *, tr, tf):
    p, c = xs.shape
    e, d, f = w_gate.shape
    row = lambda j, i, te_, nu_, *_: (jnp.minimum(i, nu_[0] - 1), 0)
    bmap = lambda j, i, te_, *_: (te_[i], 0, j)
    return pl.pallas_call(
        _expert_up_kernel,
        grid_spec=pltpu.PrefetchScalarGridSpec(
            num_scalar_prefetch=len(tiles),
            grid=(f // tf, p // tr),
            in_specs=[
                pl.BlockSpec((tr, c), row),
                pl.BlockSpec(memory_space=pl.ANY),
                pl.BlockSpec(memory_space=pl.ANY),
                pl.BlockSpec((1, 1, tf), bmap),
                pl.BlockSpec((1, 1, tf), bmap),
            ],
            out_specs=pl.BlockSpec((tr, tf), lambda j, i, *_: (i, j)),
            scratch_shapes=[pltpu.VMEM((d, tf), F32), pltpu.VMEM((d, tf), F32),
                            pltpu.VMEM((d, tf), BF16), pltpu.VMEM((d, tf), BF16),
                            pltpu.SemaphoreType.DMA((2,))],
        ),
        out_shape=jax.ShapeDtypeStruct((p, f), BF16),
        compiler_params=_cparams(("arbitrary", "arbitrary")),
        name="expert_up",
    )(*tiles, xs, w_gate, w_up, b_gate.reshape(e, 1, f), b_up.reshape(e, 1, f))


def _expert_down_kernel(te_ref, nu_ref, gf_ref, gn_ref, nv_ref, h_ref, wd_hbm, bd_ref, y_ref,
                        wd_st, wd_scr, sems):
    del nu_ref
    i = pl.program_id(1)
    _stage_weights(gf_ref[i] == 1, te_ref, gn_ref, (wd_hbm,), (wd_st,), (wd_scr,), sems)

    def body(rows):
        y_ref[:rows, :] = jnp.dot(h_ref[:rows, :], wd_scr[...], preferred_element_type=F32) + bd_ref[0]

    _for_row_count(nv_ref[i], y_ref, body)


def _expert_down(tiles, hmid, w_down, b_down, *, tr, tn):
    p, f = hmid.shape
    e, _, d = w_down.shape
    return pl.pallas_call(
        _expert_down_kernel,
        grid_spec=pltpu.PrefetchScalarGridSpec(
            num_scalar_prefetch=len(tiles),
            grid=(d // tn, p // tr),
            in_specs=[
                pl.BlockSpec((tr, f), lambda j, i, te_, nu_, *_: (jnp.minimum(i, nu_[0] - 1), 0)),
                pl.BlockSpec(memory_space=pl.ANY),
                pl.BlockSpec((1, 1, tn), lambda j, i, te_, *_: (te_[i], 0, j)),
            ],
            out_specs=pl.BlockSpec((tr, tn), lambda j, i, *_: (i, j)),
            scratch_shapes=[pltpu.VMEM((f, tn), F32), pltpu.VMEM((f, tn), BF16),
                            pltpu.SemaphoreType.DMA((1,))],
        ),
        out_shape=jax.ShapeDtypeStruct((p, d), F32),
        compiler_params=_cparams(("arbitrary", "arbitrary")),
        name="expert_down",
    )(*tiles, hmid, w_down, b_down.reshape(e, 1, d))


def _combine_kernel(pos_ref, h_ref, p_ref, gf_ref, ys_ref, o_ref, buf, sem):
    tm = h_ref.shape[0]

    def row_copy(r, kk):
        return pltpu.make_async_copy(ys_ref.at[pl.ds(pos_ref[kk, r], 1)], buf.at[kk, pl.ds(r, 1)], sem)

    def start(r, carry):
        for kk in range(TOP_K):
            row_copy(r, kk).start(priority=kk % 2)
        return carry

    lax.fori_loop(0, tm, start, 0, unroll=DMA_ISSUE_UNROLL)
    for kk in range(TOP_K):
        pltpu.make_async_copy(ys_ref.at[pl.ds(0, tm)], buf.at[kk], sem).wait()
    p = p_ref[...]
    acc = h_ref[...]
    for kk in range(TOP_K):
        acc = acc + p[:, kk:kk + 1] * buf[kk]
    o_ref[...] = _rms(acc, gf_ref[...])


def _combine(pos, h, probs_t, g_final, ys, *, tm):
    n, d = h.shape
    return pl.pallas_call(
        _combine_kernel,
        grid=(n // tm,),
        in_specs=[
            pl.BlockSpec((TOP_K, tm), lambda i: (0, i), memory_space=pltpu.SMEM),
            pl.BlockSpec((tm, d), lambda i: (i, 0)),
            pl.BlockSpec((tm, TOP_K), lambda i: (i, 0)),
            pl.BlockSpec((1, d), lambda i: (0, 0)),
            pl.BlockSpec(memory_space=pl.ANY),
        ],
        out_specs=pl.BlockSpec((tm, d), lambda i: (i, 0)),
        out_shape=jax.ShapeDtypeStruct((n, d), F32),
        scratch_shapes=[pltpu.VMEM((TOP_K, tm, d), F32), pltpu.SemaphoreType.DMA(())],
        compiler_params=_cparams(("arbitrary",)),
        name="combine",
    )(pos, h, probs_t, g_final, ys)


def _slot_layout(top_i, n_experts, tr, n_tiles):
    onehot = jnp.sum((top_i[:, :, None] == jnp.arange(n_experts, dtype=I32)[None, None, :]).astype(I32), axis=0)
    rank = jnp.cumsum(onehot, axis=0) - onehot
    counts = jnp.sum(onehot, axis=0)
    cpad = ((counts + tr - 1) // tr) * tr
    cend = jnp.cumsum(cpad)
    off = cend - cpad
    pos = jnp.take_along_axis((off[None, :] + rank).T, top_i, axis=0)
    nused = (cend[-1] // tr).astype(I32)
    tile_row = jnp.arange(n_tiles, dtype=I32) * tr
    te = jnp.sum((cend[None, :] <= tile_row[:, None]).astype(I32), axis=1)
    te = jnp.minimum(te, n_experts - 1)
    used = jnp.arange(n_tiles) < nused
    te = jnp.where(used, te, te[nused - 1])
    gfirst = jnp.logical_and(used, jnp.concatenate([jnp.ones((1,), bool), te[1:] != te[:-1]])).astype(I32)
    eid = jnp.arange(n_experts, dtype=I32)
    later = jnp.logical_and(eid[None, :] > eid[:, None], counts[None, :] > 0)
    nxt = jnp.min(jnp.where(later, eid[None, :], n_experts), axis=1)
    gnext = jnp.where(nxt < n_experts, nxt, -1).astype(I32)[te]
    nvalid = jnp.where(used, jnp.clip(counts[te] - (tile_row - off[te]), 0, tr), 0).astype(I32)
    return pos.astype(I32), (te, nused.reshape(1), gfirst, gnext, nvalid)


def kernel(x_prompt, x_sample, state_ret_S, state_mlstm_C, state_mlstm_n, state_mlstm_m, g_norm1, w_in, b_if,
           g_ret_gn, g_ml_gn, w_ret_br, w_ml_br, w_out, g_norm2, w_router, b_router, w_gate, b_gate, w_up, b_up,
           w_down, b_down, g_final):
    bp, tp, d = x_prompt.shape
    bs, ts, _ = x_sample.shape
    depth = w_in.shape[0]
    n_experts = w_router.shape[-1]
    n_p, n_s = bp * tp, bs * ts
    n = n_p + n_s

    ret_qk, ret_v = N_HEADS * DK_RET, N_HEADS * DV_RET
    ml_qk, ml_v = N_HEADS * DK_ML, N_HEADS * DV_ML
    o_mi = 2 * ret_qk + 2 * ret_v + 2 * ml_qk + 2 * ml_v
    o_ga = o_mi + 2 * N_HEADS
    colblk = {"ga": 0, "gb": 1}
    base = 2 * d
    colblk.update(rq=base // ret_qk, rk=(base + ret_qk) // ret_qk, rv=(base + 2 * ret_qk) // ret_v,
                  rg=(base + 2 * ret_qk + ret_v) // ret_v)
    mbase = base + 2 * ret_qk + 2 * ret_v
    colblk.update(mq=mbase // ml_qk, mk=(mbase + ml_qk) // ml_qk, mv=(mbase + 2 * ml_qk) // ml_v,
                  mo=(mbase + 2 * ml_qk + ml_v) // ml_v)

    assert depth == 1, "one layer: the combine kernel applies the final norm"
    tr = 512
    n_slots = TOP_K * n + n_experts * tr
    n_tiles = n_slots // tr
    tm = 256
    assert n_p % tm == 0 and n_s % tm == 0
    tmd = 2 * tm if (n_p % (2 * tm) == 0 and n_s % (2 * tm) == 0) else tm

    wl_t = w_in[0].T
    wift = wl_t[o_mi:o_ga].astype(BF16)
    wif = jnp.pad(wift.T, ((0, 0), (0, LANES - 2 * N_HEADS)))
    bflat = b_if[0].reshape(1, 2 * N_HEADS)
    bif = jnp.pad(bflat, ((0, 0), (0, LANES - 2 * N_HEADS)))
    bift = bflat.reshape(2 * N_HEADS, 1)
    g1 = g_norm1[0].reshape(1, d)
    g_r = g_ret_gn[0].reshape(1, ret_v)
    g_m = g_ml_gn[0].reshape(1, ml_v)
    wa, wb, wo = w_ret_br[0].astype(BF16), w_ml_br[0].astype(BF16), w_out[0].astype(BF16)
    g2 = g_norm2[0].reshape(1, d)
    wrt = w_router[0].T.astype(BF16)
    brt = b_router[0].reshape(n_experts, 1)

    def mixers(x, n_seq, t_len, L, L_ml, pos0, s0, c0, n0, m0):
        assert t_len % L == 0 and t_len % L_ml == 0
        z, gl, glt = _inproj(x, g1, wl_t, o_mi, o_ga, 2 * d, wif, wift, bif, bift,
                             tm=min(1024, x.shape[0]), tn=1024)
        cos, sin = _rotary_tables(pos0, t_len)
        ya, s_new = _retention(z, 0, n_seq, t_len, L, cos, sin, s0, g_r, _retention_tables(L), colblk)
        yb, c_new, n_new, m_new = _mlstm(z, gl, glt, 0, n_seq, t_len, L_ml, c0, n0, m0, g_m, colblk)
        h, u2p, lgt = _merge(x, ya, yb, z, wa, wb, wo, g2, wrt, brt, tm=tm, colblk=colblk)
        return h, u2p, lgt, (s_new, c_new, n_new, m_new)

    zeros = lambda *s: jnp.zeros(s, F32)
    h_p, u2p_p, lgt_p, st_p = mixers(
        x_prompt.reshape(n_p, d), bp, tp, min(512, tp), min(512, tp), 0,
        zeros(bp, N_HEADS, DK_RET, DV_RET), zeros(bp, N_HEADS, DV_ML, DK_ML), zeros(bp, N_HEADS, DK_ML),
        zeros(bp, N_HEADS))
    h_s, u2p_s, lgt_s, st_s = mixers(
        x_sample.reshape(n_s, d), bs, ts, min(64, ts), min(64, ts), PAST_LEN,
        state_ret_S[0].astype(F32), state_mlstm_C[0].astype(F32), state_mlstm_n[0].astype(F32),
        state_mlstm_m[0].astype(F32))

    top_i, top_p = _route(jnp.concatenate([lgt_p, lgt_s], axis=1), tn=min(2048, n_s))
    pos, tiles = _slot_layout(top_i, n_experts, tr, n_tiles)
    probs_t = top_p.T

    xs = _dispatch(pos[:, :n_p], u2p_p, jnp.zeros((n_slots, d // 2), U32), tm=tmd)
    xs = _dispatch(pos[:, n_p:], u2p_s, xs, tm=tmd)
    hmid = _expert_up(tiles, xs, w_gate[0], w_up[0], b_gate[0], b_up[0], tr=tr, tf=1024)
    ys = _expert_down(tiles, hmid, w_down[0], b_down[0], tr=tr, tn=d)

    gf = g_final.reshape(1, d)
    y_p = _combine(pos[:, :n_p], h_p, probs_t[:n_p], gf, ys, tm=tmd)
    y_s = _combine(pos[:, n_p:], h_s, probs_t[n_p:], gf, ys, tm=tmd)

    return (y_p.reshape(bp, tp, d), y_s.reshape(bs, ts, d),
            st_p[0][None], st_p[1][None], st_p[2][None], st_p[3][None],
            st_s[0][None], st_s[1][None], st_s[2][None], st_s[3][None])
```

```python
import functools

import jax
import jax.numpy as jnp
import numpy as np
from jax import lax
from jax.experimental import pallas as pl
from jax.experimental.pallas import tpu as pltpu

F32 = jnp.float32
BF16 = jnp.bfloat16
I32 = jnp.int32

N_HEADS = 8
DK_RET = 128
DV_RET = 128
DK_ML = 64
DV_ML = 128
TOP_K = 4
PAST_LEN = 1024
ROPE_BASE = 10000.0
NORM_EPS = 1e-6
SWIGLU_ALPHA = 1.702
SWIGLU_LIMIT = 7.0

LANES = 128
VMEM_LIMIT = 56 * 1024 * 1024

DMA_ISSUE_UNROLL = 8

NT_DIMS = (((1,), (1,)), ((), ()))


def _cparams(sem):
    return pltpu.CompilerParams(dimension_semantics=sem, vmem_limit_bytes=VMEM_LIMIT)


def _sigmoid(x):
    return 1.0 / (1.0 + jnp.exp(-x))


def _log_sigmoid(x):
    return jnp.minimum(x, 0.0) - jnp.log(1.0 + jnp.exp(-jnp.abs(x)))


def _rms(x, g):
    return x * lax.rsqrt(jnp.mean(x * x, axis=-1, keepdims=True) + NORM_EPS) * g


def _head_norm(o, g):
    mu = jnp.mean(o, axis=-1, keepdims=True)
    d = o - mu
    var = jnp.mean(d * d, axis=-1, keepdims=True)
    return d * lax.rsqrt(var + NORM_EPS) * g


def _inproj_kernel(x_ref, g_ref, w_ref, wif_ref, wift_ref, bif_ref, bift_ref,
                   z_ref, gl_ref, glt_ref, u_scr):
    @pl.when(pl.program_id(1) == 0)
    def _():
        ub = _rms(x_ref[...], g_ref[...]).astype(BF16)
        u_scr[...] = ub
        gp = jnp.dot(ub, wif_ref[...], preferred_element_type=F32) + bif_ref[...]
        col = lax.broadcasted_iota(I32, gp.shape, 1)
        gl_ref[...] = jnp.where(col < N_HEADS, gp, _log_sigmoid(gp))
        gpt = lax.dot_general(wift_ref[...], ub, NT_DIMS, preferred_element_type=F32) + bift_ref[...]
        row = lax.broadcasted_iota(I32, gpt.shape, 0)
        glt_ref[...] = jnp.where(row < N_HEADS, gpt, _log_sigmoid(gpt))

    z_ref[...] = lax.dot_general(u_scr[...], w_ref[...].astype(BF16), NT_DIMS, preferred_element_type=F32)


def _inproj(x, g1, w_in_t, n_main, gate_row0, n_gate, wif, wift, bif, bift, *, tm, tn):
    n, d = x.shape
    ng = n_gate // tn
    c = n_gate + n_main
    assert n_main % tn == 0 and n_gate % tn == 0
    q = 16
    assert gate_row0 % q == 0 and tn % q == 0
    w_row = lambda i, j: (q * jnp.where(j < ng, gate_row0 // q + j * (tn // q), (j - ng) * (tn // q)), 0)
    return pl.pallas_call(
        _inproj_kernel,
        grid=(n // tm, c // tn),
        in_specs=[
            pl.BlockSpec((tm, d), lambda i, j: (i, 0)),
            pl.BlockSpec((1, d), lambda i, j: (0, 0)),
            pl.BlockSpec((pl.Element(tn), pl.Element(d)), w_row),
            pl.BlockSpec((d, LANES), lambda i, j: (0, 0)),
            pl.BlockSpec((2 * N_HEADS, d), lambda i, j: (0, 0)),
            pl.BlockSpec((1, LANES), lambda i, j: (0, 0)),
            pl.BlockSpec((2 * N_HEADS, 1), lambda i, j: (0, 0)),
        ],
        out_specs=[
            pl.BlockSpec((tm, tn), lambda i, j: (i, j)),
            pl.BlockSpec((tm, LANES), lambda i, j: (i, 0)),
            pl.BlockSpec((2 * N_HEADS, tm), lambda i, j: (0, i)),
        ],
        out_shape=[
            jax.ShapeDtypeStruct((n, c), F32),
            jax.ShapeDtypeStruct((n, LANES), F32),
            jax.ShapeDtypeStruct((2 * N_HEADS, n), F32),
        ],
        scratch_shapes=[pltpu.VMEM((tm, d), BF16)],
        compiler_params=_cparams(("parallel", "arbitrary")),
        name="inproj",
    )(x, g1, w_in_t, wif, wift, bif, bift)


def _ret_kernel(cd_ref, q_ref, k_ref, v_ref, rg_ref, cos_ref, sin_ref, s0_ref, g_ref,
                din_ref, qd_ref, kd_ref, y_ref, sout_ref, s_scr):
    @pl.when(pl.program_id(1) == 0)
    def _():
        s_scr[...] = s0_ref[0]

    cos = cos_ref[...]
    sin = sin_ref[...]
    heads = range(N_HEADS)
    sl = [slice(h * DK_RET, (h + 1) * DK_RET) for h in heads]
    q = [q_ref[:, sl[h]] for h in heads]
    k = [k_ref[:, sl[h]] for h in heads]
    q = [x * cos + pltpu.roll(x, DK_RET // 2, 1) * sin for x in q]
    k = [(x * cos + pltpu.roll(x, DK_RET // 2, 1) * sin) * (DK_RET ** -0.5) for x in k]
    v = [v_ref[:, sl[h]].astype(BF16) for h in heads]
    s = [s_scr[h] for h in heads]
    att = [lax.dot_general(q[h].astype(BF16), k[h].astype(BF16), NT_DIMS, preferred_element_type=F32) * din_ref[h]
           for h in heads]
    o = [jnp.dot(att[h].astype(BF16), v[h], preferred_element_type=F32)
         + jnp.dot((q[h] * qd_ref[h]).astype(BF16), s[h].astype(BF16), preferred_element_type=F32)
         for h in heads]
    s_new = [cd_ref[h] * s[h] + jnp.dot((k[h] * kd_ref[h]).T.astype(BF16), v[h], preferred_element_type=F32)
             for h in heads]
    for h in heads:
        s_scr[h] = s_new[h]
        sout_ref[0, h] = s_new[h]
        rg = rg_ref[:, sl[h]]
        y = _head_norm(o[h], g_ref[:, sl[h]]) * (rg * _sigmoid(rg))
        y_ref[:, sl[h]] = y.astype(BF16)


def _retention(z, row0, n_seq, t_len, L, cos, sin, s0, g_gn, tabs, colblk):
    din, qd, kd, cd = tabs
    nch = t_len // L
    rb0 = row0 // L
    w = N_HEADS * DK_RET
    zspec = lambda cb: pl.BlockSpec((L, w), lambda s, c: (rb0 + s * nch + c, cb))
    tspec = pl.BlockSpec((L, DK_RET), lambda s, c: (c, 0))
    const3 = lambda shape: pl.BlockSpec(shape, lambda s, c: (0, 0, 0))
    y, s_out = pl.pallas_call(
        _ret_kernel,
        grid=(n_seq, nch),
        in_specs=[
            pl.BlockSpec(memory_space=pltpu.SMEM),
            zspec(colblk["rq"]), zspec(colblk["rk"]), zspec(colblk["rv"]), zspec(colblk["rg"]),
            tspec, tspec,
            pl.BlockSpec((1, N_HEADS, DK_RET, DV_RET), lambda s, c: (s, 0, 0, 0)),
            pl.BlockSpec((1, w), lambda s, c: (0, 0)),
            const3((N_HEADS, L, L)), const3((N_HEADS, L, DK_RET)), const3((N_HEADS, L, DK_RET)),
        ],
        out_specs=[
            pl.BlockSpec((L, w), lambda s, c: (s * nch + c, 0)),
            pl.BlockSpec((1, N_HEADS, DK_RET, DV_RET), lambda s, c: (s, 0, 0, 0)),
        ],
        scratch_shapes=[pltpu.VMEM((N_HEADS, DK_RET, DV_RET), F32)],
        out_shape=[
            jax.ShapeDtypeStruct((n_seq * t_len, w), BF16),
            jax.ShapeDtypeStruct((n_seq, N_HEADS, DK_RET, DV_RET), F32),
        ],
        compiler_params=_cparams(("parallel", "arbitrary")),
        name="retention",
    )(cd, z, z, z, z, cos, sin, s0, g_gn, din, qd, kd)
    return y, s_out


def _retention_tables(L):
    log_g = np.log(1.0 - 2.0 ** (-5.0 - np.arange(N_HEADS, dtype=np.float64)))
    idx = np.arange(L, dtype=np.float64)
    rel = idx[:, None] - idx[None, :]
    din = np.where(rel >= 0, np.exp(rel[None] * log_g[:, None, None]), 0.0)
    qd = np.exp((idx[None, :] + 1.0) * log_g[:, None])[..., None] * np.ones((1, 1, DK_RET))
    kd = np.exp((L - 1.0 - idx[None, :]) * log_g[:, None])[..., None] * np.ones((1, 1, DK_RET))
    cd = np.exp(L * log_g)
    return (jnp.asarray(din, F32), jnp.asarray(qd, F32), jnp.asarray(kd, F32), jnp.asarray(cd, F32))


def _rotary_tables(pos0, t_len):
    half = DK_RET // 2
    inv = ROPE_BASE ** (-np.arange(half, dtype=np.float64) / half)
    ang = (pos0 + np.arange(t_len, dtype=np.float64))[:, None] * inv[None, :]
    c, s = np.cos(ang), np.sin(ang)
    return (jnp.asarray(np.concatenate([c, c], axis=-1), F32),
            jnp.asarray(np.concatenate([-s, s], axis=-1), F32))


def _ml_kernel(q_ref, k_ref, v_ref, mo_ref, gl_ref, glt_ref, c0_ref, n0_ref, m0_ref, g_ref,
               y_ref, cout_ref, nout_ref, mout_ref, c_scr, n_scr, m_scr):
    @pl.when(pl.program_id(1) == 0)
    def _():
        c_scr[...] = c0_ref[0]
        n_scr[...] = n0_ref[0]
        m_scr[...] = m0_ref[0]

    L = q_ref.shape[0]
    ii = lax.broadcasted_iota(I32, (L, L), 0)
    jj = lax.broadcasted_iota(I32, (L, L), 1)
    causal = jj <= ii
    gl = gl_ref[...]
    glt = glt_ref[0]
    m_all = m_scr[...]
    n_all = n_scr[...]
    heads = range(N_HEADS)
    qs = [slice(h * DK_ML, (h + 1) * DK_ML) for h in heads]
    vs = [slice(h * DV_ML, (h + 1) * DV_ML) for h in heads]
    qf = [q_ref[:, qs[h]] * (DK_ML ** -0.5) for h in heads]
    kf = [k_ref[:, qs[h]] for h in heads]
    vf = [v_ref[:, vs[h]] for h in heads]
    q = [x.astype(BF16) for x in qf]
    k = [x.astype(BF16) for x in kf]
    ig_c = [gl[:, h:h + 1] for h in heads]
    ig_r = [glt[h:h + 1, :] for h in heads]
    b_c = [jnp.sum(jnp.where(causal, jnp.broadcast_to(glt[N_HEADS + h:N_HEADS + h + 1, :], (L, L)), 0.0),
                   axis=1, keepdims=True) for h in heads]
    b_r = [jnp.sum(jnp.where(ii <= jj, jnp.broadcast_to(gl[:, N_HEADS + h:N_HEADS + h + 1], (L, L)), 0.0),
                   axis=0, keepdims=True) for h in heads]
    logw = [jnp.where(causal, b_c[h] - b_r[h] + ig_r[h], -jnp.inf) for h in heads]
    m_prev = [m_all[h:h + 1, 0:1] for h in heads]
    inter = [b_c[h] + m_prev[h] for h in heads]
    m_row = [jnp.maximum(inter[h], jnp.max(logw[h], axis=1, keepdims=True)) for h in heads]
    qk = [lax.dot_general(q[h], k[h], NT_DIMS, preferred_element_type=F32) for h in heads]
    w = [jnp.exp(logw[h] - m_row[h]) * qk[h] for h in heads]
    w_inter = [jnp.exp(inter[h] - m_row[h]) for h in heads]
    c_old = [c_scr[h] for h in heads]
    n_old = [n_all[h:h + 1, :] for h in heads]
    num = [jnp.dot(w[h].astype(BF16), vf[h].astype(BF16), preferred_element_type=F32)
           + w_inter[h] * lax.dot_general(q[h], c_old[h].astype(BF16), NT_DIMS, preferred_element_type=F32)
           for h in heads]
    den = [jnp.sum(w[h], axis=1, keepdims=True) + w_inter[h] * jnp.sum(qf[h] * n_old[h], axis=1, keepdims=True)
           for h in heads]
    hh = [num[h] / jnp.maximum(jnp.abs(den[h]), jnp.exp(-m_row[h])) for h in heads]
    m_new = [m_row[h][L - 1:L, :] for h in heads]
    b_last = [b_c[h][L - 1:L, :] for h in heads]
    decay = [jnp.exp(b_last[h] + m_prev[h] - m_new[h]) for h in heads]
    wk_c = [jnp.exp(b_last[h] - b_c[h] + ig_c[h] - m_new[h]) for h in heads]
    c_new = [decay[h] * c_old[h] + jnp.dot((vf[h] * wk_c[h]).T.astype(BF16), k[h], preferred_element_type=F32)
             for h in heads]
    n_new = [decay[h] * n_old[h] + jnp.sum(wk_c[h] * kf[h], axis=0, keepdims=True) for h in heads]
    for h in heads:
        c_scr[h] = c_new[h]
        cout_ref[0, h] = c_new[h]
        y = _sigmoid(mo_ref[:, vs[h]]) * _head_norm(hh[h], g_ref[:, vs[h]])
        y_ref[:, vs[h]] = y.astype(BF16)
    m_out = jnp.concatenate([jnp.broadcast_to(m, (1, LANES)) for m in m_new], axis=0)
    n_out = jnp.concatenate(n_new, axis=0)
    m_scr[...] = m_out
    n_scr[...] = n_out
    mout_ref[0] = m_out
    nout_ref[0] = n_out


def _mlstm(z, gl, glt, row0, n_seq, t_len, L, c0, n0, m0, g_gn, colblk):
    nch = t_len // L
    rb0 = row0 // L
    wqk = N_HEADS * DK_ML
    wv = N_HEADS * DV_ML
    qspec = lambda cb: pl.BlockSpec((L, wqk), lambda s, c: (rb0 + s * nch + c, cb))
    vspec = lambda cb: pl.BlockSpec((L, wv), lambda s, c: (rb0 + s * nch + c, cb))
    m0b = jnp.broadcast_to(m0[..., None], (n_seq, N_HEADS, LANES))
    glt = glt[:, row0:row0 + n_seq * t_len].reshape(2 * N_HEADS, n_seq * nch, L).transpose(1, 0, 2)
    y, c_out, n_out, m_out = pl.pallas_call(
        _ml_kernel,
        grid=(n_seq, nch),
        in_specs=[
            qspec(colblk["mq"]), qspec(colblk["mk"]), vspec(colblk["mv"]), vspec(colblk["mo"]),
            pl.BlockSpec((L, LANES), lambda s, c: (rb0 + s * nch + c, 0)),
            pl.BlockSpec((1, 2 * N_HEADS, L), lambda s, c: (s * nch + c, 0, 0)),
            pl.BlockSpec((1, N_HEADS, DV_ML, DK_ML), lambda s, c: (s, 0, 0, 0)),
            pl.BlockSpec((1, N_HEADS, DK_ML), lambda s, c: (s, 0, 0)),
            pl.BlockSpec((1, N_HEADS, LANES), lambda s, c: (s, 0, 0)),
            pl.BlockSpec((1, wv), lambda s, c: (0, 0)),
        ],
        out_specs=[
            pl.BlockSpec((L, wv), lambda s, c: (s * nch + c, 0)),
            pl.BlockSpec((1, N_HEADS, DV_ML, DK_ML), lambda s, c: (s, 0, 0, 0)),
            pl.BlockSpec((1, N_HEADS, DK_ML), lambda s, c: (s, 0, 0)),
            pl.BlockSpec((1, N_HEADS, LANES), lambda s, c: (s, 0, 0)),
        ],
        out_shape=[
            jax.ShapeDtypeStruct((n_seq * t_len, wv), BF16),
            jax.ShapeDtypeStruct((n_seq, N_HEADS, DV_ML, DK_ML), F32),
            jax.ShapeDtypeStruct((n_seq, N_HEADS, DK_ML), F32),
            jax.ShapeDtypeStruct((n_seq, N_HEADS, LANES), F32),
        ],
        scratch_shapes=[
            pltpu.VMEM((N_HEADS, DV_ML, DK_ML), F32),
            pltpu.VMEM((N_HEADS, DK_ML), F32),
            pltpu.VMEM((N_HEADS, LANES), F32),
        ],
        compiler_params=_cparams(("parallel", "arbitrary")),
        name="mlstm",
    )(z, z, z, z, gl, glt, c0, n0, m0b, g_gn)
    return y, c_out, n_out, m_out[..., 0]


def _merge_kernel(x_ref, ya_ref, yb_ref, ga_ref, gb_ref, wa_ref, wb_ref, wo_ref, g2_ref,
                  wrt_ref, brt_ref, h_ref, u2_ref, lgt_ref):
    bra = jnp.dot(ya_ref[...], wa_ref[...], preferred_element_type=F32)
    brb = jnp.dot(yb_ref[...], wb_ref[...], preferred_element_type=F32)
    merged = _sigmoid(ga_ref[...]) * bra + _sigmoid(gb_ref[...]) * brb
    hmid = x_ref[...] + jnp.dot(merged.astype(BF16), wo_ref[...], preferred_element_type=F32)
    h_ref[...] = hmid
    u2 = _rms(hmid, g2_ref[...])
    u2_ref[...] = u2
    lgt_ref[...] = (lax.dot_general(wrt_ref[...], u2.astype(BF16), NT_DIMS, preferred_element_type=F32)
                    + brt_ref[...])


def _merge(x, ya, yb, z, wa, wb, wo, g2, wrt, brt, *, tm, colblk):
    n, d = x.shape
    e = wrt.shape[0]
    full = lambda a: pl.BlockSpec(a.shape, lambda i: (0,) * a.ndim, pipeline_mode=pl.Buffered(1))
    return pl.pallas_call(
        _merge_kernel,
        grid=(n // tm,),
        in_specs=[
            pl.BlockSpec((tm, d), lambda i: (i, 0)),
            pl.BlockSpec((tm, ya.shape[1]), lambda i: (i, 0)),
            pl.BlockSpec((tm, yb.shape[1]), lambda i: (i, 0)),
            pl.BlockSpec((tm, d), lambda i: (i, colblk["ga"])),
            pl.BlockSpec((tm, d), lambda i: (i, colblk["gb"])),
            full(wa), full(wb), full(wo), full(g2), full(wrt), full(brt),
        ],
        out_specs=[
            pl.BlockSpec((tm, d), lambda i: (i, 0)),
            pl.BlockSpec((tm, d), lambda i: (i, 0)),
            pl.BlockSpec((e, tm), lambda i: (0, i)),
        ],
        out_shape=[
            jax.ShapeDtypeStruct((n, d), F32),
            jax.ShapeDtypeStruct((n, d), F32),
            jax.ShapeDtypeStruct((e, n), F32),
        ],
        compiler_params=_cparams(("parallel",)),
        name="merge",
    )(x, ya, yb, z, z, wa, wb, wo, g2, wrt, brt)


def _route_kernel(lg_ref, ti_ref, tp_ref):
    l = lg_ref[...]
    e = l.shape[0]
    eid = lax.broadcasted_iota(I32, l.shape, 0)
    vals, idxs = [], []
    for _ in range(TOP_K):
        mx = jnp.max(l, axis=0, keepdims=True)
        ix = jnp.min(jnp.where(l == mx, eid, e), axis=0, keepdims=True)
        vals.append(mx)
        idxs.append(ix)
        l = jnp.where(eid == ix, -jnp.inf, l)
    ex = [jnp.exp(v - vals[0]) for v in vals]
    tot = ex[0] + ex[1] + ex[2] + ex[3]
    for kk in range(TOP_K):
        ti_ref[kk:kk + 1, :] = idxs[kk]
        tp_ref[kk:kk + 1, :] = ex[kk] / tot


def _route(lgt, *, tn):
    e, n = lgt.shape
    return pl.pallas_call(
        _route_kernel,
        grid=(n // tn,),
        in_specs=[pl.BlockSpec((e, tn), lambda i: (0, i))],
        out_specs=[pl.BlockSpec((TOP_K, tn), lambda i: (0, i)),
                   pl.BlockSpec((TOP_K, tn), lambda i: (0, i))],
        out_shape=[jax.ShapeDtypeStruct((TOP_K, n), I32),
                   jax.ShapeDtypeStruct((TOP_K, n), F32)],
        compiler_params=_cparams(("parallel",)),
        name="route",
    )(lgt)


def _dispatch_kernel(pos_ref, u_ref, xs_in_ref, xs_ref, sem):
    del xs_in_ref
    tm = u_ref.shape[0]

    def row_copy(r, kk):
        return pltpu.make_async_copy(u_ref.at[pl.ds(r, 1)], xs_ref.at[pl.ds(pos_ref[kk, r], 1)], sem)

    def start(r, carry):
        for kk in range(TOP_K):
            row_copy(r, kk).start(priority=kk % 2)
        return carry

    lax.fori_loop(0, tm, start, 0, unroll=DMA_ISSUE_UNROLL)
    for kk in range(TOP_K):
        pltpu.make_async_copy(u_ref, xs_ref.at[pl.ds(0, tm)], sem).wait()


def _dispatch(pos, u2, xs0, *, tm):
    n, c = u2.shape
    n_slots = xs0.shape[0]
    return pl.pallas_call(
        _dispatch_kernel,
        grid=(n // tm,),
        in_specs=[
            pl.BlockSpec((TOP_K, tm), lambda i: (0, i), memory_space=pltpu.SMEM),
            pl.BlockSpec((tm, c), lambda i: (i, 0)),
            pl.BlockSpec(memory_space=pl.ANY),
        ],
        out_specs=pl.BlockSpec(memory_space=pl.ANY),
        out_shape=jax.ShapeDtypeStruct((n_slots, c), xs0.dtype),
        scratch_shapes=[pltpu.SemaphoreType.DMA(())],
        input_output_aliases={2: 0},
        compiler_params=_cparams(("arbitrary",)),
        name="dispatch",
    )(pos, u2, xs0)


def _stage_weights(first, te_ref, gn_ref, gi_ref, ng_ref, w_hbm, w_stage, sems):
    j, i = pl.program_id(0), pl.program_id(1)
    tw = w_stage[0].shape[2]
    slot = (j * ng_ref[0] + gi_ref[i]) & 1

    def copies(e, col, s):
        cols = pl.ds(pl.multiple_of(col * tw, tw), tw)
        return [pltpu.make_async_copy(w.at[e, :, cols], st.at[s], sems.at[k, s])
                for k, (w, st) in enumerate(zip(w_hbm, w_stage))]

    @pl.when(first)
    def _():
        e = te_ref[i]

        @pl.when(jnp.logical_and(i == 0, j == 0))
        def _():
            for cp in copies(e, j, slot):
                cp.start()

        for cp in copies(e, j, slot):
            cp.wait()

        nxt = gn_ref[i]

        @pl.when(nxt >= 0)
        def _():
            for cp in copies(nxt, j, 1 - slot):
                cp.start()

        @pl.when(jnp.logical_and(nxt < 0, j + 1 < pl.num_programs(0)))
        def _():
            for cp in copies(te_ref[0], j + 1, 1 - slot):
                cp.start()

    return slot


def _for_row_count(nv, out_ref, body):
    tr = out_ref.shape[0]
    half = tr // 2

    @pl.when(nv > half)
    def _():
        body(tr)

    @pl.when(nv <= half)
    def _():
        @pl.when(nv > 0)
        def _():
            body(half)

        @pl.when(nv <= 0)
        def _():
            out_ref[:half, :] = jnp.zeros((half, out_ref.shape[1]), out_ref.dtype)

        out_ref[half:, :] = jnp.zeros((tr - half, out_ref.shape[1]), out_ref.dtype)


def _expert_up_kernel(te_ref, nu_ref, gf_ref, gn_ref, nv_ref, gi_ref, ng_ref, xs_ref, wg_hbm, wu_hbm,
                      bg_ref, bu_ref, h_ref, wg_st, wu_st, sems):
    del nu_ref
    i = pl.program_id(1)
    slot = _stage_weights(gf_ref[i] == 1, te_ref, gn_ref, gi_ref, ng_ref, (wg_hbm, wu_hbm), (wg_st, wu_st), sems)

    def body(rows):
        x = xs_ref[:rows, :].astype(BF16)
        g = jnp.dot(x, wg_st[slot].astype(BF16), preferred_element_type=F32) + bg_ref[0]
        up = jnp.dot(x, wu_st[slot].astype(BF16), preferred_element_type=F32) + bu_ref[0]
        g = jnp.minimum(g, SWIGLU_LIMIT)
        up = jnp.clip(up, -SWIGLU_LIMIT, SWIGLU_LIMIT)
        h_ref[:rows, :] = (g * _sigmoid(SWIGLU_ALPHA * g) * (up + 1.0)).astype(BF16)

    _for_row_count(nv_ref[i], h_ref, body)


def _expert_up(tiles, xs, w_gate, w_up, b_gate, b_up, *, tr, tf):
    p, c = xs.shape
    e, d, f = w_gate.shape
    row = lambda j, i, te_, nu_, *_: (jnp.minimum(i, nu_[0] - 1), 0)
    bmap = lambda j, i, te_, *_: (te_[i], 0, j)
    return pl.pallas_call(
        _expert_up_kernel,
        grid_spec=pltpu.PrefetchScalarGridSpec(
            num_scalar_prefetch=len(tiles),
            grid=(f // tf, p // tr),
            in_specs=[
                pl.BlockSpec((tr, c), row),
                pl.BlockSpec(memory_space=pl.ANY),
                pl.BlockSpec(memory_space=pl.ANY),
                pl.BlockSpec((1, 1, tf), bmap),
                pl.BlockSpec((1, 1, tf), bmap),
            ],
            out_specs=pl.BlockSpec((tr, tf), lambda j, i, *_: (i, j)),
            scratch_shapes=[pltpu.VMEM((2, d, tf), F32), pltpu.VMEM((2, d, tf), F32),
                            pltpu.SemaphoreType.DMA((2, 2))],
        ),
        out_shape=jax.ShapeDtypeStruct((p, f), BF16),
        compiler_params=_cparams(("arbitrary", "arbitrary")),
        name="expert_up",
    )(*tiles, xs, w_gate, w_up, b_gate.reshape(e, 1, f), b_up.reshape(e, 1, f))


def _expert_down_kernel(te_ref, nu_ref, gf_ref, gn_ref, nv_ref, gi_ref, ng_ref, h_ref, wd_hbm, bd_ref, y_ref,
                        wd_st, sems):
    del nu_ref
    i = pl.program_id(1)
    slot = _stage_weights(gf_ref[i] == 1, te_ref, gn_ref, gi_ref, ng_ref, (wd_hbm,), (wd_st,), sems)

    def body(rows):
        y_ref[:rows, :] = (jnp.dot(h_ref[:rows, :], wd_st[slot].astype(BF16), preferred_element_type=F32)
                           + bd_ref[0])

    _for_row_count(nv_ref[i], y_ref, body)


def _expert_down(tiles, hmid, w_down, b_down, *, tr, tn):
    p, f = hmid.shape
    e, _, d = w_down.shape
    return pl.pallas_call(
        _expert_down_kernel,
        grid_spec=pltpu.PrefetchScalarGridSpec(
            num_scalar_prefetch=len(tiles),
            grid=(d // tn, p // tr),
            in_specs=[
                pl.BlockSpec((tr, f), lambda j, i, te_, nu_, *_: (jnp.minimum(i, nu_[0] - 1), 0)),
                pl.BlockSpec(memory_space=pl.ANY),
                pl.BlockSpec((1, 1, tn), lambda j, i, te_, *_: (te_[i], 0, j)),
            ],
            out_specs=pl.BlockSpec((tr, tn), lambda j, i, *_: (i, j)),
            scratch_shapes=[pltpu.VMEM((2, f, tn), F32), pltpu.SemaphoreType.DMA((1, 2))],
        ),
        out_shape=jax.ShapeDtypeStruct((p, d), F32),
        compiler_params=_cparams(("arbitrary", "arbitrary")),
        name="expert_down",
    )(*tiles, hmid, w_down, b_down.reshape(e, 1, d))


def _combine_kernel(pos_ref, h_ref, p_ref, gf_ref, ys_ref, o_ref, buf, sem):
    tm = h_ref.shape[0]

    def row_copy(r, kk):
        return pltpu.make_async_copy(ys_ref.at[pl.ds(pos_ref[kk, r], 1)], buf.at[kk, pl.ds(r, 1)], sem)

    def start(r, carry):
        for kk in range(TOP_K):
            row_copy(r, kk).start(priority=kk % 2)
        return carry

    lax.fori_loop(0, tm, start, 0, unroll=DMA_ISSUE_UNROLL)
    for kk in range(TOP_K):
        pltpu.make_async_copy(ys_ref.at[pl.ds(0, tm)], buf.at[kk], sem).wait()
    p = p_ref[...]
    acc = h_ref[...]
    for kk in range(TOP_K):
        acc = acc + p[:, kk:kk + 1] * buf[kk]
    o_ref[...] = _rms(acc, gf_ref[...])


def _combine(pos, h, probs_t, g_final, ys, *, tm):
    n, d = h.shape
    return pl.pallas_call(
        _combine_kernel,
        grid=(n // tm,),
        in_specs=[
            pl.BlockSpec((TOP_K, tm), lambda i: (0, i), memory_space=pltpu.SMEM),
            pl.BlockSpec((tm, d), lambda i: (i, 0)),
            pl.BlockSpec((tm, TOP_K), lambda i: (i, 0)),
            pl.BlockSpec((1, d), lambda i: (0, 0)),
            pl.BlockSpec(memory_space=pl.ANY),
        ],
        out_specs=pl.BlockSpec((tm, d), lambda i: (i, 0)),
        out_shape=jax.ShapeDtypeStruct((n, d), F32),
        scratch_shapes=[pltpu.VMEM((TOP_K, tm, d), F32), pltpu.SemaphoreType.DMA(())],
        compiler_params=_cparams(("arbitrary",)),
        name="combine",
    )(pos, h, probs_t, g_final, ys)


def _slot_layout(top_i, n_experts, tr, n_tiles):
    onehot = jnp.sum((top_i[:, :, None] == jnp.arange(n_experts, dtype=I32)[None, None, :]).astype(I32), axis=0)
    rank = jnp.cumsum(onehot, axis=0) - onehot
    counts = jnp.sum(onehot, axis=0)
    cpad = ((counts + tr - 1) // tr) * tr
    cend = jnp.cumsum(cpad)
    off = cend - cpad
    pos = jnp.take_along_axis((off[None, :] + rank).T, top_i, axis=0)
    nused = (cend[-1] // tr).astype(I32)
    tile_row = jnp.arange(n_tiles, dtype=I32) * tr
    te = jnp.sum((cend[None, :] <= tile_row[:, None]).astype(I32), axis=1)
    te = jnp.minimum(te, n_experts - 1)
    used = jnp.arange(n_tiles) < nused
    te = jnp.where(used, te, te[nused - 1])
    gfirst = jnp.logical_and(used, jnp.concatenate([jnp.ones((1,), bool), te[1:] != te[:-1]])).astype(I32)
    eid = jnp.arange(n_experts, dtype=I32)
    later = jnp.logical_and(eid[None, :] > eid[:, None], counts[None, :] > 0)
    nxt = jnp.min(jnp.where(later, eid[None, :], n_experts), axis=1)
    gnext = jnp.where(nxt < n_experts, nxt, -1).astype(I32)[te]
    nvalid = jnp.where(used, jnp.clip(counts[te] - (tile_row - off[te]), 0, tr), 0).astype(I32)
    gidx = (jnp.cumsum(gfirst) - 1).astype(I32)
    ngroups = jnp.sum(gfirst).astype(I32).reshape(1)
    return pos.astype(I32), (te, nused.reshape(1), gfirst, gnext, nvalid, gidx, ngroups)


def kernel(x_prompt, x_sample, state_ret_S, state_mlstm_C, state_mlstm_n, state_mlstm_m, g_norm1, w_in, b_if,
           g_ret_gn, g_ml_gn, w_ret_br, w_ml_br, w_out, g_norm2, w_router, b_router, w_gate, b_gate, w_up, b_up,
           w_down, b_down, g_final):
    bp, tp, d = x_prompt.shape
    bs, ts, _ = x_sample.shape
    depth = w_in.shape[0]
    n_experts = w_router.shape[-1]
    n_p, n_s = bp * tp, bs * ts
    n = n_p + n_s

    ret_qk, ret_v = N_HEADS * DK_RET, N_HEADS * DV_RET
    ml_qk, ml_v = N_HEADS * DK_ML, N_HEADS * DV_ML
    o_mi = 2 * ret_qk + 2 * ret_v + 2 * ml_qk + 2 * ml_v
    o_ga = o_mi + 2 * N_HEADS
    colblk = {"ga": 0, "gb": 1}
    base = 2 * d
    colblk.update(rq=base // ret_qk, rk=(base + ret_qk) // ret_qk, rv=(base + 2 * ret_qk) // ret_v,
                  rg=(base + 2 * ret_qk + ret_v) // ret_v)
    mbase = base + 2 * ret_qk + 2 * ret_v
    colblk.update(mq=mbase // ml_qk, mk=(mbase + ml_qk) // ml_qk, mv=(mbase + 2 * ml_qk) // ml_v,
                  mo=(mbase + 2 * ml_qk + ml_v) // ml_v)

    assert depth == 1, "one layer: the combine kernel applies the final norm"
    tr = 512
    n_slots = TOP_K * n + n_experts * tr
    n_tiles = n_slots // tr
    tm = 256
    assert n_p % tm == 0 and n_s % tm == 0
    tmd = 2 * tm if (n_p % (2 * tm) == 0 and n_s % (2 * tm) == 0) else tm

    wl_t = w_in[0].T
    wift = wl_t[o_mi:o_ga].astype(BF16)
    wif = jnp.pad(wift.T, ((0, 0), (0, LANES - 2 * N_HEADS)))
    bflat = b_if[0].reshape(1, 2 * N_HEADS)
    bif = jnp.pad(bflat, ((0, 0), (0, LANES - 2 * N_HEADS)))
    bift = bflat.reshape(2 * N_HEADS, 1)
    g1 = g_norm1[0].reshape(1, d)
    g_r = g_ret_gn[0].reshape(1, ret_v)
    g_m = g_ml_gn[0].reshape(1, ml_v)
    wa, wb, wo = w_ret_br[0].astype(BF16), w_ml_br[0].astype(BF16), w_out[0].astype(BF16)
    g2 = g_norm2[0].reshape(1, d)
    wrt = w_router[0].T.astype(BF16)
    brt = b_router[0].reshape(n_experts, 1)

    def mixers(x, n_seq, t_len, L, L_ml, pos0, s0, c0, n0, m0):
        assert t_len % L == 0 and t_len % L_ml == 0
        z, gl, glt = _inproj(x, g1, wl_t, o_mi, o_ga, 2 * d, wif, wift, bif, bift,
                             tm=min(1024, x.shape[0]), tn=1024)
        cos, sin = _rotary_tables(pos0, t_len)
        ya, s_new = _retention(z, 0, n_seq, t_len, L, cos, sin, s0, g_r, _retention_tables(L), colblk)
        yb, c_new, n_new, m_new = _mlstm(z, gl, glt, 0, n_seq, t_len, L_ml, c0, n0, m0, g_m, colblk)
        h, u2, lgt = _merge(x, ya, yb, z, wa, wb, wo, g2, wrt, brt, tm=tm, colblk=colblk)
        return h, u2, lgt, (s_new, c_new, n_new, m_new)

    zeros = lambda *s: jnp.zeros(s, F32)
    h_p, u2_p, lgt_p, st_p = mixers(
        x_prompt.reshape(n_p, d), bp, tp, min(512, tp), min(512, tp), 0,
        zeros(bp, N_HEADS, DK_RET, DV_RET), zeros(bp, N_HEADS, DV_ML, DK_ML), zeros(bp, N_HEADS, DK_ML),
        zeros(bp, N_HEADS))
    h_s, u2_s, lgt_s, st_s = mixers(
        x_sample.reshape(n_s, d), bs, ts, min(64, ts), min(64, ts), PAST_LEN,
        state_ret_S[0].astype(F32), state_mlstm_C[0].astype(F32), state_mlstm_n[0].astype(F32),
        state_mlstm_m[0].astype(F32))

    top_i, top_p = _route(jnp.concatenate([lgt_p, lgt_s], axis=1), tn=min(2048, n_s))
    pos, tiles = _slot_layout(top_i, n_experts, tr, n_tiles)
    probs_t = top_p.T

    xs = _dispatch(pos[:, :n_p], u2_p, jnp.zeros((n_slots, d), F32), tm=tmd)
    xs = _dispatch(pos[:, n_p:], u2_s, xs, tm=tmd)
    hmid = _expert_up(tiles, xs, w_gate[0], w_up[0], b_gate[0], b_up[0], tr=tr, tf=1024)
    ys = _expert_down(tiles, hmid, w_down[0], b_down[0], tr=tr, tn=d)

    gf = g_final.reshape(1, d)
    y_p = _combine(pos[:, :n_p], h_p, probs_t[:n_p], gf, ys, tm=tmd)
    y_s = _combine(pos[:, n_p:], h_s, probs_t[n_p:], gf, ys, tm=tmd)

    return (y_p.reshape(bp, tp, d), y_s.reshape(bs, ts, d),
            st_p[0][None], st_p[1][None], st_p[2][None], st_p[3][None],
            st_s[0][None], st_s[1][None], st_s[2][None], st_s[3][None])
```

```python
import functools

import jax
import jax.numpy as jnp
import numpy as np
from jax import lax
from jax.experimental import pallas as pl
from jax.experimental.pallas import tpu as pltpu

F32 = jnp.float32
BF16 = jnp.bfloat16
I32 = jnp.int32

N_HEADS = 8
DK_RET = 128
DV_RET = 128
DK_ML = 64
DV_ML = 128
TOP_K = 4
PAST_LEN = 1024
ROPE_BASE = 10000.0
NORM_EPS = 1e-6
SWIGLU_ALPHA = 1.702
SWIGLU_LIMIT = 7.0

LANES = 128
VMEM_LIMIT = 56 * 1024 * 1024

DMA_ISSUE_UNROLL = 8

NT_DIMS = (((1,), (1,)), ((), ()))


def _cparams(sem):
    return pltpu.CompilerParams(dimension_semantics=sem, vmem_limit_bytes=VMEM_LIMIT)


def _sigmoid(x):
    return 1.0 / (1.0 + jnp.exp(-x))


def _log_sigmoid(x):
    return jnp.minimum(x, 0.0) - jnp.log(1.0 + jnp.exp(-jnp.abs(x)))


def _rms(x, g):
    return x * lax.rsqrt(jnp.mean(x * x, axis=-1, keepdims=True) + NORM_EPS) * g


def _head_norm(o, g):
    mu = jnp.mean(o, axis=-1, keepdims=True)
    d = o - mu
    var = jnp.mean(d * d, axis=-1, keepdims=True)
    return d * lax.rsqrt(var + NORM_EPS) * g


def _inproj_kernel(x_ref, g_ref, w_ref, wif_ref, wift_ref, bif_ref, bift_ref,
                   z_ref, gl_ref, glt_ref, u_scr):
    @pl.when(pl.program_id(1) == 0)
    def _():
        ub = _rms(x_ref[...], g_ref[...]).astype(BF16)
        u_scr[...] = ub
        gp = jnp.dot(ub, wif_ref[...], preferred_element_type=F32) + bif_ref[...]
        col = lax.broadcasted_iota(I32, gp.shape, 1)
        gl_ref[...] = jnp.where(col < N_HEADS, gp, _log_sigmoid(gp))
        gpt = lax.dot_general(wift_ref[...], ub, NT_DIMS, preferred_element_type=F32) + bift_ref[...]
        row = lax.broadcasted_iota(I32, gpt.shape, 0)
        glt_ref[...] = jnp.where(row < N_HEADS, gpt, _log_sigmoid(gpt))

    z_ref[...] = lax.dot_general(u_scr[...], w_ref[...].astype(BF16), NT_DIMS, preferred_element_type=F32)


def _inproj(x, g1, w_in_t, n_main, gate_row0, n_gate, wif, wift, bif, bift, *, tm, tn):
    n, d = x.shape
    ng = n_gate // tn
    c = n_gate + n_main
    assert n_main % tn == 0 and n_gate % tn == 0
    q = 16
    assert gate_row0 % q == 0 and tn % q == 0
    w_row = lambda i, j: (q * jnp.where(j < ng, gate_row0 // q + j * (tn // q), (j - ng) * (tn // q)), 0)
    return pl.pallas_call(
        _inproj_kernel,
        grid=(n // tm, c // tn),
        in_specs=[
            pl.BlockSpec((tm, d), lambda i, j: (i, 0)),
            pl.BlockSpec((1, d), lambda i, j: (0, 0)),
            pl.BlockSpec((pl.Element(tn), pl.Element(d)), w_row),
            pl.BlockSpec((d, LANES), lambda i, j: (0, 0)),
            pl.BlockSpec((2 * N_HEADS, d), lambda i, j: (0, 0)),
            pl.BlockSpec((1, LANES), lambda i, j: (0, 0)),
            pl.BlockSpec((2 * N_HEADS, 1), lambda i, j: (0, 0)),
        ],
        out_specs=[
            pl.BlockSpec((tm, tn), lambda i, j: (i, j)),
            pl.BlockSpec((tm, LANES), lambda i, j: (i, 0)),
            pl.BlockSpec((2 * N_HEADS, tm), lambda i, j: (0, i)),
        ],
        out_shape=[
            jax.ShapeDtypeStruct((n, c), F32),
            jax.ShapeDtypeStruct((n, LANES), F32),
            jax.ShapeDtypeStruct((2 * N_HEADS, n), F32),
        ],
        scratch_shapes=[pltpu.VMEM((tm, d), BF16)],
        compiler_params=_cparams(("parallel", "arbitrary")),
        name="inproj",
    )(x, g1, w_in_t, wif, wift, bif, bift)


def _ret_kernel(cd_ref, q_ref, k_ref, v_ref, rg_ref, cos_ref, sin_ref, s0_ref, g_ref,
                din_ref, qd_ref, kd_ref, y_ref, sout_ref, s_scr):
    @pl.when(pl.program_id(1) == 0)
    def _():
        s_scr[...] = s0_ref[0]

    cos = cos_ref[...]
    sin = sin_ref[...]
    heads = range(N_HEADS)
    sl = [slice(h * DK_RET, (h + 1) * DK_RET) for h in heads]
    q = [q_ref[:, sl[h]] for h in heads]
    k = [k_ref[:, sl[h]] for h in heads]
    q = [x * cos + pltpu.roll(x, DK_RET // 2, 1) * sin for x in q]
    k = [(x * cos + pltpu.roll(x, DK_RET // 2, 1) * sin) * (DK_RET ** -0.5) for x in k]
    v = [v_ref[:, sl[h]].astype(BF16) for h in heads]
    s = [s_scr[h] for h in heads]
    att = [lax.dot_general(q[h].astype(BF16), k[h].astype(BF16), NT_DIMS, preferred_element_type=F32) * din_ref[h]
           for h in heads]
    o = [jnp.dot(att[h].astype(BF16), v[h], preferred_element_type=F32)
         + jnp.dot((q[h] * qd_ref[h]).astype(BF16), s[h].astype(BF16), preferred_element_type=F32)
         for h in heads]
    s_new = [cd_ref[h] * s[h] + jnp.dot((k[h] * kd_ref[h]).T.astype(BF16), v[h], preferred_element_type=F32)
             for h in heads]
    for h in heads:
        s_scr[h] = s_new[h]
        sout_ref[0, h] = s_new[h]
        rg = rg_ref[:, sl[h]]
        y = _head_norm(o[h], g_ref[:, sl[h]]) * (rg * _sigmoid(rg))
        y_ref[:, sl[h]] = y.astype(BF16)


def _retention(z, row0, n_seq, t_len, L, cos, sin, s0, g_gn, tabs, colblk):
    din, qd, kd, cd = tabs
    nch = t_len // L
    rb0 = row0 // L
    w = N_HEADS * DK_RET
    zspec = lambda cb: pl.BlockSpec((L, w), lambda s, c: (rb0 + s * nch + c, cb))
    tspec = pl.BlockSpec((L, DK_RET), lambda s, c: (c, 0))
    const3 = lambda shape: pl.BlockSpec(shape, lambda s, c: (0, 0, 0))
    y, s_out = pl.pallas_call(
        _ret_kernel,
        grid=(n_seq, nch),
        in_specs=[
            pl.BlockSpec(memory_space=pltpu.SMEM),
            zspec(colblk["rq"]), zspec(colblk["rk"]), zspec(colblk["rv"]), zspec(colblk["rg"]),
            tspec, tspec,
            pl.BlockSpec((1, N_HEADS, DK_RET, DV_RET), lambda s, c: (s, 0, 0, 0)),
            pl.BlockSpec((1, w), lambda s, c: (0, 0)),
            const3((N_HEADS, L, L)), const3((N_HEADS, L, DK_RET)), const3((N_HEADS, L, DK_RET)),
        ],
        out_specs=[
            pl.BlockSpec((L, w), lambda s, c: (s * nch + c, 0)),
            pl.BlockSpec((1, N_HEADS, DK_RET, DV_RET), lambda s, c: (s, 0, 0, 0)),
        ],
        scratch_shapes=[pltpu.VMEM((N_HEADS, DK_RET, DV_RET), F32)],
        out_shape=[
            jax.ShapeDtypeStruct((n_seq * t_len, w), BF16),
            jax.ShapeDtypeStruct((n_seq, N_HEADS, DK_RET, DV_RET), F32),
        ],
        compiler_params=_cparams(("parallel", "arbitrary")),
        name="retention",
    )(cd, z, z, z, z, cos, sin, s0, g_gn, din, qd, kd)
    return y, s_out


def _retention_tables(L):
    log_g = np.log(1.0 - 2.0 ** (-5.0 - np.arange(N_HEADS, dtype=np.float64)))
    idx = np.arange(L, dtype=np.float64)
    rel = idx[:, None] - idx[None, :]
    din = np.where(rel >= 0, np.exp(rel[None] * log_g[:, None, None]), 0.0)
    qd = np.exp((idx[None, :] + 1.0) * log_g[:, None])[..., None] * np.ones((1, 1, DK_RET))
    kd = np.exp((L - 1.0 - idx[None, :]) * log_g[:, None])[..., None] * np.ones((1, 1, DK_RET))
    cd = np.exp(L * log_g)
    return (jnp.asarray(din, F32), jnp.asarray(qd, F32), jnp.asarray(kd, F32), jnp.asarray(cd, F32))


def _rotary_tables(pos0, t_len):
    half = DK_RET // 2
    inv = ROPE_BASE ** (-np.arange(half, dtype=np.float64) / half)
    ang = (pos0 + np.arange(t_len, dtype=np.float64))[:, None] * inv[None, :]
    c, s = np.cos(ang), np.sin(ang)
    return (jnp.asarray(np.concatenate([c, c], axis=-1), F32),
            jnp.asarray(np.concatenate([-s, s], axis=-1), F32))


def _ml_kernel(q_ref, k_ref, v_ref, mo_ref, gl_ref, glt_ref, c0_ref, n0_ref, m0_ref, g_ref,
               y_ref, cout_ref, nout_ref, mout_ref, c_scr, n_scr, m_scr):
    @pl.when(pl.program_id(1) == 0)
    def _():
        c_scr[...] = c0_ref[0]
        n_scr[...] = n0_ref[0]
        m_scr[...] = m0_ref[0]

    L = q_ref.shape[0]
    ii = lax.broadcasted_iota(I32, (L, L), 0)
    jj = lax.broadcasted_iota(I32, (L, L), 1)
    causal = jj <= ii
    gl = gl_ref[...]
    glt = glt_ref[0]
    m_all = m_scr[...]
    n_all = n_scr[...]
    heads = range(N_HEADS)
    qs = [slice(h * DK_ML, (h + 1) * DK_ML) for h in heads]
    vs = [slice(h * DV_ML, (h + 1) * DV_ML) for h in heads]
    qf = [q_ref[:, qs[h]] * (DK_ML ** -0.5) for h in heads]
    kf = [k_ref[:, qs[h]] for h in heads]
    vf = [v_ref[:, vs[h]] for h in heads]
    q = [x.astype(BF16) for x in qf]
    k = [x.astype(BF16) for x in kf]
    ig_c = [gl[:, h:h + 1] for h in heads]
    ig_r = [glt[h:h + 1, :] for h in heads]
    b_c = [jnp.sum(jnp.where(causal, jnp.broadcast_to(glt[N_HEADS + h:N_HEADS + h + 1, :], (L, L)), 0.0),
                   axis=1, keepdims=True) for h in heads]
    b_r = [jnp.sum(jnp.where(ii <= jj, jnp.broadcast_to(gl[:, N_HEADS + h:N_HEADS + h + 1], (L, L)), 0.0),
                   axis=0, keepdims=True) for h in heads]
    logw = [jnp.where(causal, b_c[h] - b_r[h] + ig_r[h], -jnp.inf) for h in heads]
    m_prev = [m_all[h:h + 1, 0:1] for h in heads]
    inter = [b_c[h] + m_prev[h] for h in heads]
    m_row = [jnp.maximum(inter[h], jnp.max(logw[h], axis=1, keepdims=True)) for h in heads]
    qk = [lax.dot_general(q[h], k[h], NT_DIMS, preferred_element_type=F32) for h in heads]
    w = [jnp.exp(logw[h] - m_row[h]) * qk[h] for h in heads]
    w_inter = [jnp.exp(inter[h] - m_row[h]) for h in heads]
    c_old = [c_scr[h] for h in heads]
    n_old = [n_all[h:h + 1, :] for h in heads]
    num = [jnp.dot(w[h].astype(BF16), vf[h].astype(BF16), preferred_element_type=F32)
           + w_inter[h] * lax.dot_general(q[h], c_old[h].astype(BF16), NT_DIMS, preferred_element_type=F32)
           for h in heads]
    den = [jnp.sum(w[h], axis=1, keepdims=True) + w_inter[h] * jnp.sum(qf[h] * n_old[h], axis=1, keepdims=True)
           for h in heads]
    hh = [num[h] / jnp.maximum(jnp.abs(den[h]), jnp.exp(-m_row[h])) for h in heads]
    m_new = [m_row[h][L - 1:L, :] for h in heads]
    b_last = [b_c[h][L - 1:L, :] for h in heads]
    decay = [jnp.exp(b_last[h] + m_prev[h] - m_new[h]) for h in heads]
    wk_c = [jnp.exp(b_last[h] - b_c[h] + ig_c[h] - m_new[h]) for h in heads]
    c_new = [decay[h] * c_old[h] + jnp.dot((vf[h] * wk_c[h]).T.astype(BF16), k[h], preferred_element_type=F32)
             for h in heads]
    n_new = [decay[h] * n_old[h] + jnp.sum(wk_c[h] * kf[h], axis=0, keepdims=True) for h in heads]
    for h in heads:
        c_scr[h] = c_new[h]
        cout_ref[0, h] = c_new[h]
        y = _sigmoid(mo_ref[:, vs[h]]) * _head_norm(hh[h], g_ref[:, vs[h]])
        y_ref[:, vs[h]] = y.astype(BF16)
    m_out = jnp.concatenate([jnp.broadcast_to(m, (1, LANES)) for m in m_new], axis=0)
    n_out = jnp.concatenate(n_new, axis=0)
    m_scr[...] = m_out
    n_scr[...] = n_out
    mout_ref[0] = m_out
    nout_ref[0] = n_out


def _mlstm(z, gl, glt, row0, n_seq, t_len, L, c0, n0, m0, g_gn, colblk):
    nch = t_len // L
    rb0 = row0 // L
    wqk = N_HEADS * DK_ML
    wv = N_HEADS * DV_ML
    qspec = lambda cb: pl.BlockSpec((L, wqk), lambda s, c: (rb0 + s * nch + c, cb))
    vspec = lambda cb: pl.BlockSpec((L, wv), lambda s, c: (rb0 + s * nch + c, cb))
    m0b = jnp.broadcast_to(m0[..., None], (n_seq, N_HEADS, LANES))
    glt = glt[:, row0:row0 + n_seq * t_len].reshape(2 * N_HEADS, n_seq * nch, L).transpose(1, 0, 2)
    y, c_out, n_out, m_out = pl.pallas_call(
        _ml_kernel,
        grid=(n_seq, nch),
        in_specs=[
            qspec(colblk["mq"]), qspec(colblk["mk"]), vspec(colblk["mv"]), vspec(colblk["mo"]),
            pl.BlockSpec((L, LANES), lambda s, c: (rb0 + s * nch + c, 0)),
            pl.BlockSpec((1, 2 * N_HEADS, L), lambda s, c: (s * nch + c, 0, 0)),
            pl.BlockSpec((1, N_HEADS, DV_ML, DK_ML), lambda s, c: (s, 0, 0, 0)),
            pl.BlockSpec((1, N_HEADS, DK_ML), lambda s, c: (s, 0, 0)),
            pl.BlockSpec((1, N_HEADS, LANES), lambda s, c: (s, 0, 0)),
            pl.BlockSpec((1, wv), lambda s, c: (0, 0)),
        ],
        out_specs=[
            pl.BlockSpec((L, wv), lambda s, c: (s * nch + c, 0)),
            pl.BlockSpec((1, N_HEADS, DV_ML, DK_ML), lambda s, c: (s, 0, 0, 0)),
            pl.BlockSpec((1, N_HEADS, DK_ML), lambda s, c: (s, 0, 0)),
            pl.BlockSpec((1, N_HEADS, LANES), lambda s, c: (s, 0, 0)),
        ],
        out_shape=[
            jax.ShapeDtypeStruct((n_seq * t_len, wv), BF16),
            jax.ShapeDtypeStruct((n_seq, N_HEADS, DV_ML, DK_ML), F32),
            jax.ShapeDtypeStruct((n_seq, N_HEADS, DK_ML), F32),
            jax.ShapeDtypeStruct((n_seq, N_HEADS, LANES), F32),
        ],
        scratch_shapes=[
            pltpu.VMEM((N_HEADS, DV_ML, DK_ML), F32),
            pltpu.VMEM((N_HEADS, DK_ML), F32),
            pltpu.VMEM((N_HEADS, LANES), F32),
        ],
        compiler_params=_cparams(("parallel", "arbitrary")),
        name="mlstm",
    )(z, z, z, z, gl, glt, c0, n0, m0b, g_gn)
    return y, c_out, n_out, m_out[..., 0]


def _merge_kernel(x_ref, ya_ref, yb_ref, ga_ref, gb_ref, wa_ref, wb_ref, wo_ref, g2_ref,
                  wrt_ref, brt_ref, h_ref, u2_ref, lgt_ref):
    bra = jnp.dot(ya_ref[...], wa_ref[...], preferred_element_type=F32)
    brb = jnp.dot(yb_ref[...], wb_ref[...], preferred_element_type=F32)
    merged = _sigmoid(ga_ref[...]) * bra + _sigmoid(gb_ref[...]) * brb
    hmid = x_ref[...] + jnp.dot(merged.astype(BF16), wo_ref[...], preferred_element_type=F32)
    h_ref[...] = hmid
    u2 = _rms(hmid, g2_ref[...])
    u2_ref[...] = u2
    lgt_ref[...] = (lax.dot_general(wrt_ref[...], u2.astype(BF16), NT_DIMS, preferred_element_type=F32)
                    + brt_ref[...])


def _merge(x, ya, yb, z, wa, wb, wo, g2, wrt, brt, *, tm, colblk):
    n, d = x.shape
    e = wrt.shape[0]
    full = lambda a: pl.BlockSpec(a.shape, lambda i: (0,) * a.ndim, pipeline_mode=pl.Buffered(1))
    return pl.pallas_call(
        _merge_kernel,
        grid=(n // tm,),
        in_specs=[
            pl.BlockSpec((tm, d), lambda i: (i, 0)),
            pl.BlockSpec((tm, ya.shape[1]), lambda i: (i, 0)),
            pl.BlockSpec((tm, yb.shape[1]), lambda i: (i, 0)),
            pl.BlockSpec((tm, d), lambda i: (i, colblk["ga"])),
            pl.BlockSpec((tm, d), lambda i: (i, colblk["gb"])),
            full(wa), full(wb), full(wo), full(g2), full(wrt), full(brt),
        ],
        out_specs=[
            pl.BlockSpec((tm, d), lambda i: (i, 0)),
            pl.BlockSpec((tm, d), lambda i: (i, 0)),
            pl.BlockSpec((e, tm), lambda i: (0, i)),
        ],
        out_shape=[
            jax.ShapeDtypeStruct((n, d), F32),
            jax.ShapeDtypeStruct((n, d), F32),
            jax.ShapeDtypeStruct((e, n), F32),
        ],
        compiler_params=_cparams(("parallel",)),
        name="merge",
    )(x, ya, yb, z, z, wa, wb, wo, g2, wrt, brt)


def _route_kernel(lg_ref, ti_ref, tp_ref):
    l = lg_ref[...]
    e = l.shape[0]
    eid = lax.broadcasted_iota(I32, l.shape, 0)
    vals, idxs = [], []
    for _ in range(TOP_K):
        mx = jnp.max(l, axis=0, keepdims=True)
        ix = jnp.min(jnp.where(l == mx, eid, e), axis=0, keepdims=True)
        vals.append(mx)
        idxs.append(ix)
        l = jnp.where(eid == ix, -jnp.inf, l)
    ex = [jnp.exp(v - vals[0]) for v in vals]
    tot = ex[0] + ex[1] + ex[2] + ex[3]
    for kk in range(TOP_K):
        ti_ref[kk:kk + 1, :] = idxs[kk]
        tp_ref[kk:kk + 1, :] = ex[kk] / tot


def _route(lgt, *, tn):
    e, n = lgt.shape
    return pl.pallas_call(
        _route_kernel,
        grid=(n // tn,),
        in_specs=[pl.BlockSpec((e, tn), lambda i: (0, i))],
        out_specs=[pl.BlockSpec((TOP_K, tn), lambda i: (0, i)),
                   pl.BlockSpec((TOP_K, tn), lambda i: (0, i))],
        out_shape=[jax.ShapeDtypeStruct((TOP_K, n), I32),
                   jax.ShapeDtypeStruct((TOP_K, n), F32)],
        compiler_params=_cparams(("parallel",)),
        name="route",
    )(lgt)


def _dispatch_kernel(pos_ref, pad0_ref, padn_ref, nu_ref, up_ref, us_ref, xs_ref, zrow, ztile, sem, zsem,
                     *, np_tiles, tr):
    i = pl.program_id(0)
    tm = up_ref.shape[0]
    n_exp = pad0_ref.shape[0]
    n_tiles = xs_ref.shape[0] // tr

    @pl.when(i == 0)
    def _():
        zrow[...] = jnp.zeros_like(zrow)
        ztile[...] = jnp.zeros_like(ztile)

        def pad_copy(e, r):
            return pltpu.make_async_copy(zrow.at[pl.ds(0, 1)], xs_ref.at[pl.ds(pad0_ref[e] + r, 1)], zsem)

        def tile_copy(t):
            return pltpu.make_async_copy(ztile, xs_ref.at[pl.ds(pl.multiple_of(t * tr, tr), tr)], zsem)

        def for_all_zero_copies(fn):
            def per_expert(e, carry):
                lax.fori_loop(0, padn_ref[e], lambda r, c: (fn(pad_copy(e, r)), c)[1], 0)
                return carry

            lax.fori_loop(0, n_exp, per_expert, 0)
            lax.fori_loop(nu_ref[0], n_tiles, lambda t, c: (fn(tile_copy(t)), c)[1], 0)

        for_all_zero_copies(lambda cp: cp.start())
        for_all_zero_copies(lambda cp: cp.wait())

    def scatter(u_ref):
        def row_copy(r, kk):
            return pltpu.make_async_copy(u_ref.at[pl.ds(r, 1)], xs_ref.at[pl.ds(pos_ref[kk, r], 1)], sem)

        def start(r, carry):
            for kk in range(TOP_K):
                row_copy(r, kk).start(priority=kk % 2)
            return carry

        lax.fori_loop(0, tm, start, 0, unroll=DMA_ISSUE_UNROLL)
        for kk in range(TOP_K):
            pltpu.make_async_copy(u_ref, xs_ref.at[pl.ds(0, tm)], sem).wait()

    @pl.when(i < np_tiles)
    def _():
        scatter(up_ref)

    @pl.when(i >= np_tiles)
    def _():
        scatter(us_ref)


def _dispatch(pos, pads, nused, u2_p, u2_s, n_slots, *, tm, tr):
    pad0, padn = pads
    (n_p, c), n_s = u2_p.shape, u2_s.shape[0]
    npt = n_p // tm
    smem = pl.BlockSpec(memory_space=pltpu.SMEM)
    return pl.pallas_call(
        functools.partial(_dispatch_kernel, np_tiles=npt, tr=tr),
        grid=((n_p + n_s) // tm,),
        in_specs=[
            pl.BlockSpec((TOP_K, tm), lambda i: (0, i), memory_space=pltpu.SMEM),
            smem, smem, smem,
            pl.BlockSpec((tm, c), lambda i: (jnp.minimum(i, npt - 1), 0)),
            pl.BlockSpec((tm, c), lambda i: (jnp.maximum(i - npt, 0), 0)),
        ],
        out_specs=pl.BlockSpec(memory_space=pl.ANY),
        out_shape=jax.ShapeDtypeStruct((n_slots, c), u2_p.dtype),
        scratch_shapes=[pltpu.VMEM((8, c), u2_p.dtype), pltpu.VMEM((tr, c), u2_p.dtype),
                        pltpu.SemaphoreType.DMA(()), pltpu.SemaphoreType.DMA(())],
        compiler_params=_cparams(("arbitrary",)),
        name="dispatch",
    )(pos, pad0, padn, nused, u2_p, u2_s)


def _stage_weights(first, te_ref, gn_ref, gi_ref, ng_ref, w_hbm, w_stage, sems):
    j, i = pl.program_id(0), pl.program_id(1)
    tw = w_stage[0].shape[2]
    slot = (j * ng_ref[0] + gi_ref[i]) & 1

    def copies(e, col, s):
        cols = pl.ds(pl.multiple_of(col * tw, tw), tw)
        return [pltpu.make_async_copy(w.at[e, :, cols], st.at[s], sems.at[k, s])
                for k, (w, st) in enumerate(zip(w_hbm, w_stage))]

    @pl.when(first)
    def _():
        e = te_ref[i]

        @pl.when(jnp.logical_and(i == 0, j == 0))
        def _():
            for cp in copies(e, j, slot):
                cp.start()

        for cp in copies(e, j, slot):
            cp.wait()

        nxt = gn_ref[i]

        @pl.when(nxt >= 0)
        def _():
            for cp in copies(nxt, j, 1 - slot):
                cp.start()

        @pl.when(jnp.logical_and(nxt < 0, j + 1 < pl.num_programs(0)))
        def _():
            for cp in copies(te_ref[0], j + 1, 1 - slot):
                cp.start()

    return slot


def _for_row_count(nv, out_ref, body):
    tr = out_ref.shape[0]
    half = tr // 2

    @pl.when(nv > half)
    def _():
        body(tr)

    @pl.when(nv <= half)
    def _():
        @pl.when(nv > 0)
        def _():
            body(half)

        @pl.when(nv <= 0)
        def _():
            out_ref[:half, :] = jnp.zeros((half, out_ref.shape[1]), out_ref.dtype)

        out_ref[half:, :] = jnp.zeros((tr - half, out_ref.shape[1]), out_ref.dtype)


def _expert_up_kernel(te_ref, nu_ref, gf_ref, gn_ref, nv_ref, gi_ref, ng_ref, xs_ref, wg_hbm, wu_hbm,
                      bg_ref, bu_ref, h_ref, wg_st, wu_st, sems):
    del nu_ref
    i = pl.program_id(1)
    slot = _stage_weights(gf_ref[i] == 1, te_ref, gn_ref, gi_ref, ng_ref, (wg_hbm, wu_hbm), (wg_st, wu_st), sems)

    def body(rows):
        x = xs_ref[:rows, :].astype(BF16)
        g = jnp.dot(x, wg_st[slot].astype(BF16), preferred_element_type=F32) + bg_ref[0]
        up = jnp.dot(x, wu_st[slot].astype(BF16), preferred_element_type=F32) + bu_ref[0]
        g = jnp.minimum(g, SWIGLU_LIMIT)
        up = jnp.clip(up, -SWIGLU_LIMIT, SWIGLU_LIMIT)
        h_ref[:rows, :] = (g * _sigmoid(SWIGLU_ALPHA * g) * (up + 1.0)).astype(BF16)

    _for_row_count(nv_ref[i], h_ref, body)


def _expert_up(tiles, xs, w_gate, w_up, b_gate, b_up, *, tr, tf):
    p, c = xs.shape
    e, d, f = w_gate.shape
    row = lambda j, i, te_, nu_, *_: (jnp.minimum(i, nu_[0] - 1), 0)
    bmap = lambda j, i, te_, *_: (te_[i], 0, j)
    return pl.pallas_call(
        _expert_up_kernel,
        grid_spec=pltpu.PrefetchScalarGridSpec(
            num_scalar_prefetch=len(tiles),
            grid=(f // tf, p // tr),
            in_specs=[
                pl.BlockSpec((tr, c), row),
                pl.BlockSpec(memory_space=pl.ANY),
                pl.BlockSpec(memory_space=pl.ANY),
                pl.BlockSpec((1, 1, tf), bmap),
                pl.BlockSpec((1, 1, tf), bmap),
            ],
            out_specs=pl.BlockSpec((tr, tf), lambda j, i, *_: (i, j)),
            scratch_shapes=[pltpu.VMEM((2, d, tf), F32), pltpu.VMEM((2, d, tf), F32),
                            pltpu.SemaphoreType.DMA((2, 2))],
        ),
        out_shape=jax.ShapeDtypeStruct((p, f), BF16),
        compiler_params=_cparams(("arbitrary", "arbitrary")),
        name="expert_up",
    )(*tiles, xs, w_gate, w_up, b_gate.reshape(e, 1, f), b_up.reshape(e, 1, f))


def _expert_down_kernel(te_ref, nu_ref, gf_ref, gn_ref, nv_ref, gi_ref, ng_ref, h_ref, wd_hbm, bd_ref, y_ref,
                        wd_st, sems):
    del nu_ref
    i = pl.program_id(1)
    slot = _stage_weights(gf_ref[i] == 1, te_ref, gn_ref, gi_ref, ng_ref, (wd_hbm,), (wd_st,), sems)

    def body(rows):
        y_ref[:rows, :] = (jnp.dot(h_ref[:rows, :], wd_st[slot].astype(BF16), preferred_element_type=F32)
                           + bd_ref[0])

    _for_row_count(nv_ref[i], y_ref, body)


def _expert_down(tiles, hmid, w_down, b_down, *, tr, tn):
    p, f = hmid.shape
    e, _, d = w_down.shape
    return pl.pallas_call(
        _expert_down_kernel,
        grid_spec=pltpu.PrefetchScalarGridSpec(
            num_scalar_prefetch=len(tiles),
            grid=(d // tn, p // tr),
            in_specs=[
                pl.BlockSpec((tr, f), lambda j, i, te_, nu_, *_: (jnp.minimum(i, nu_[0] - 1), 0)),
                pl.BlockSpec(memory_space=pl.ANY),
                pl.BlockSpec((1, 1, tn), lambda j, i, te_, *_: (te_[i], 0, j)),
            ],
            out_specs=pl.BlockSpec((tr, tn), lambda j, i, *_: (i, j)),
            scratch_shapes=[pltpu.VMEM((2, f, tn), F32), pltpu.SemaphoreType.DMA((1, 2))],
        ),
        out_shape=jax.ShapeDtypeStruct((p, d), F32),
        compiler_params=_cparams(("arbitrary", "arbitrary")),
        name="expert_down",
    )(*tiles, hmid, w_down, b_down.reshape(e, 1, d))


def _combine_kernel(pos_ref, h_ref, p_ref, gf_ref, ys_ref, o_ref, buf, sem):
    tm = h_ref.shape[0]

    def row_copy(r, kk):
        return pltpu.make_async_copy(ys_ref.at[pl.ds(pos_ref[kk, r], 1)], buf.at[kk, pl.ds(r, 1)], sem)

    def start(r, carry):
        for kk in range(TOP_K):
            row_copy(r, kk).start(priority=kk % 2)
        return carry

    lax.fori_loop(0, tm, start, 0, unroll=DMA_ISSUE_UNROLL)
    for kk in range(TOP_K):
        pltpu.make_async_copy(ys_ref.at[pl.ds(0, tm)], buf.at[kk], sem).wait()
    p = p_ref[...]
    acc = h_ref[...]
    for kk in range(TOP_K):
        acc = acc + p[:, kk:kk + 1] * buf[kk]
    o_ref[...] = _rms(acc, gf_ref[...])


def _combine(pos, h, probs_t, g_final, ys, *, tm):
    n, d = h.shape
    return pl.pallas_call(
        _combine_kernel,
        grid=(n // tm,),
        in_specs=[
            pl.BlockSpec((TOP_K, tm), lambda i: (0, i), memory_space=pltpu.SMEM),
            pl.BlockSpec((tm, d), lambda i: (i, 0)),
            pl.BlockSpec((tm, TOP_K), lambda i: (i, 0)),
            pl.BlockSpec((1, d), lambda i: (0, 0)),
            pl.BlockSpec(memory_space=pl.ANY),
        ],
        out_specs=pl.BlockSpec((tm, d), lambda i: (i, 0)),
        out_shape=jax.ShapeDtypeStruct((n, d), F32),
        scratch_shapes=[pltpu.VMEM((TOP_K, tm, d), F32), pltpu.SemaphoreType.DMA(())],
        compiler_params=_cparams(("arbitrary",)),
        name="combine",
    )(pos, h, probs_t, g_final, ys)


def _slot_layout(top_i, n_experts, tr, n_tiles):
    onehot = jnp.sum((top_i[:, :, None] == jnp.arange(n_experts, dtype=I32)[None, None, :]).astype(I32), axis=0)
    rank = jnp.cumsum(onehot, axis=0) - onehot
    counts = jnp.sum(onehot, axis=0)
    cpad = ((counts + tr - 1) // tr) * tr
    cend = jnp.cumsum(cpad)
    off = cend - cpad
    pos = jnp.take_along_axis((off[None, :] + rank).T, top_i, axis=0)
    nused = (cend[-1] // tr).astype(I32)
    tile_row = jnp.arange(n_tiles, dtype=I32) * tr
    te = jnp.sum((cend[None, :] <= tile_row[:, None]).astype(I32), axis=1)
    te = jnp.minimum(te, n_experts - 1)
    used = jnp.arange(n_tiles) < nused
    te = jnp.where(used, te, te[nused - 1])
    gfirst = jnp.logical_and(used, jnp.concatenate([jnp.ones((1,), bool), te[1:] != te[:-1]])).astype(I32)
    eid = jnp.arange(n_experts, dtype=I32)
    later = jnp.logical_and(eid[None, :] > eid[:, None], counts[None, :] > 0)
    nxt = jnp.min(jnp.where(later, eid[None, :], n_experts), axis=1)
    gnext = jnp.where(nxt < n_experts, nxt, -1).astype(I32)[te]
    nvalid = jnp.where(used, jnp.clip(counts[te] - (tile_row - off[te]), 0, tr), 0).astype(I32)
    gidx = (jnp.cumsum(gfirst) - 1).astype(I32)
    ngroups = jnp.sum(gfirst).astype(I32).reshape(1)
    pads = ((off + counts).astype(I32), (cpad - counts).astype(I32))
    return pos.astype(I32), pads, (te, nused.reshape(1), gfirst, gnext, nvalid, gidx, ngroups)


def kernel(x_prompt, x_sample, state_ret_S, state_mlstm_C, state_mlstm_n, state_mlstm_m, g_norm1, w_in, b_if,
           g_ret_gn, g_ml_gn, w_ret_br, w_ml_br, w_out, g_norm2, w_router, b_router, w_gate, b_gate, w_up, b_up,
           w_down, b_down, g_final):
    bp, tp, d = x_prompt.shape
    bs, ts, _ = x_sample.shape
    depth = w_in.shape[0]
    n_experts = w_router.shape[-1]
    n_p, n_s = bp * tp, bs * ts
    n = n_p + n_s

    ret_qk, ret_v = N_HEADS * DK_RET, N_HEADS * DV_RET
    ml_qk, ml_v = N_HEADS * DK_ML, N_HEADS * DV_ML
    o_mi = 2 * ret_qk + 2 * ret_v + 2 * ml_qk + 2 * ml_v
    o_ga = o_mi + 2 * N_HEADS
    colblk = {"ga": 0, "gb": 1}
    base = 2 * d
    colblk.update(rq=base // ret_qk, rk=(base + ret_qk) // ret_qk, rv=(base + 2 * ret_qk) // ret_v,
                  rg=(base + 2 * ret_qk + ret_v) // ret_v)
    mbase = base + 2 * ret_qk + 2 * ret_v
    colblk.update(mq=mbase // ml_qk, mk=(mbase + ml_qk) // ml_qk, mv=(mbase + 2 * ml_qk) // ml_v,
                  mo=(mbase + 2 * ml_qk + ml_v) // ml_v)

    assert depth == 1, "one layer: the combine kernel applies the final norm"
    tr = 512
    n_slots = TOP_K * n + n_experts * tr
    n_tiles = n_slots // tr
    tm = 256
    assert n_p % tm == 0 and n_s % tm == 0
    tmd = 2 * tm if (n_p % (2 * tm) == 0 and n_s % (2 * tm) == 0) else tm

    wl_t = w_in[0].T
    wift = wl_t[o_mi:o_ga].astype(BF16)
    wif = jnp.pad(wift.T, ((0, 0), (0, LANES - 2 * N_HEADS)))
    bflat = b_if[0].reshape(1, 2 * N_HEADS)
    bif = jnp.pad(bflat, ((0, 0), (0, LANES - 2 * N_HEADS)))
    bift = bflat.reshape(2 * N_HEADS, 1)
    g1 = g_norm1[0].reshape(1, d)
    g_r = g_ret_gn[0].reshape(1, ret_v)
    g_m = g_ml_gn[0].reshape(1, ml_v)
    wa, wb, wo = w_ret_br[0].astype(BF16), w_ml_br[0].astype(BF16), w_out[0].astype(BF16)
    g2 = g_norm2[0].reshape(1, d)
    wrt = w_router[0].T.astype(BF16)
    brt = b_router[0].reshape(n_experts, 1)

    def mixers(x, n_seq, t_len, L, L_ml, pos0, s0, c0, n0, m0):
        assert t_len % L == 0 and t_len % L_ml == 0
        z, gl, glt = _inproj(x, g1, wl_t, o_mi, o_ga, 2 * d, wif, wift, bif, bift,
                             tm=min(1024, x.shape[0]), tn=1024)
        cos, sin = _rotary_tables(pos0, t_len)
        ya, s_new = _retention(z, 0, n_seq, t_len, L, cos, sin, s0, g_r, _retention_tables(L), colblk)
        yb, c_new, n_new, m_new = _mlstm(z, gl, glt, 0, n_seq, t_len, L_ml, c0, n0, m0, g_m, colblk)
        h, u2, lgt = _merge(x, ya, yb, z, wa, wb, wo, g2, wrt, brt, tm=tm, colblk=colblk)
        return h, u2, lgt, (s_new, c_new, n_new, m_new)

    zeros = lambda *s: jnp.zeros(s, F32)
    h_p, u2_p, lgt_p, st_p = mixers(
        x_prompt.reshape(n_p, d), bp, tp, min(512, tp), min(512, tp), 0,
        zeros(bp, N_HEADS, DK_RET, DV_RET), zeros(bp, N_HEADS, DV_ML, DK_ML), zeros(bp, N_HEADS, DK_ML),
        zeros(bp, N_HEADS))
    h_s, u2_s, lgt_s, st_s = mixers(
        x_sample.reshape(n_s, d), bs, ts, min(64, ts), min(64, ts), PAST_LEN,
        state_ret_S[0].astype(F32), state_mlstm_C[0].astype(F32), state_mlstm_n[0].astype(F32),
        state_mlstm_m[0].astype(F32))

    top_i, top_p = _route(jnp.concatenate([lgt_p, lgt_s], axis=1), tn=min(2048, n_s))
    pos, pads, tiles = _slot_layout(top_i, n_experts, tr, n_tiles)
    probs_t = top_p.T

    xs = _dispatch(pos, pads, tiles[1], u2_p, u2_s, n_slots, tm=tmd, tr=tr)
    hmid = _expert_up(tiles, xs, w_gate[0], w_up[0], b_gate[0], b_up[0], tr=tr, tf=1024)
    ys = _expert_down(tiles, hmid, w_down[0], b_down[0], tr=tr, tn=d)

    gf = g_final.reshape(1, d)
    y_p = _combine(pos[:, :n_p], h_p, probs_t[:n_p], gf, ys, tm=tmd)
    y_s = _combine(pos[:, n_p:], h_s, probs_t[n_p:], gf, ys, tm=tmd)

    return (y_p.reshape(bp, tp, d), y_s.reshape(bs, ts, d),
            st_p[0][None], st_p[1][None], st_p[2][None], st_p[3][None],
            st_s[0][None], st_s[1][None], st_s[2][None], st_s[3][None])
```

```python
import functools

import jax
import jax.numpy as jnp
import numpy as np
from jax import lax
from jax.experimental import pallas as pl
from jax.experimental.pallas import tpu as pltpu

F32 = jnp.float32
BF16 = jnp.bfloat16
I32 = jnp.int32

N_HEADS = 8
DK_RET = 128
DV_RET = 128
DK_ML = 64
DV_ML = 128
TOP_K = 4
PAST_LEN = 1024
ROPE_BASE = 10000.0
NORM_EPS = 1e-6
SWIGLU_ALPHA = 1.702
SWIGLU_LIMIT = 7.0

LANES = 128
SUBLANES = 8
VMEM_LIMIT = 56 * 1024 * 1024

DMA_ISSUE_UNROLL = 8

NT_DIMS = (((1,), (1,)), ((), ()))


def _cparams(sem):
    return pltpu.CompilerParams(dimension_semantics=sem, vmem_limit_bytes=VMEM_LIMIT)


def _sigmoid(x):
    return 1.0 / (1.0 + jnp.exp(-x))


def _log_sigmoid(x):
    return jnp.minimum(x, 0.0) - jnp.log(1.0 + jnp.exp(-jnp.abs(x)))


def _rms(x, g):
    return x * lax.rsqrt(jnp.mean(x * x, axis=-1, keepdims=True) + NORM_EPS) * g


def _head_norm(o, g):
    mu = jnp.mean(o, axis=-1, keepdims=True)
    d = o - mu
    var = jnp.mean(d * d, axis=-1, keepdims=True)
    return d * lax.rsqrt(var + NORM_EPS) * g


def _inproj_kernel(x_ref, g_ref, w_ref, wif_ref, wift_ref, bif_ref, bift_ref,
                   z_ref, gl_ref, glt_ref, u_scr):
    @pl.when(pl.program_id(1) == 0)
    def _():
        ub = _rms(x_ref[...], g_ref[...]).astype(BF16)
        u_scr[...] = ub
        gp = jnp.dot(ub, wif_ref[...], preferred_element_type=F32) + bif_ref[...]
        col = lax.broadcasted_iota(I32, gp.shape, 1)
        gl_ref[...] = jnp.where(col < N_HEADS, gp, _log_sigmoid(gp))
        gpt = lax.dot_general(wift_ref[...], ub, NT_DIMS, preferred_element_type=F32) + bift_ref[...]
        row = lax.broadcasted_iota(I32, gpt.shape, 0)
        glt_ref[...] = jnp.where(row < N_HEADS, gpt, _log_sigmoid(gpt))

    z_ref[...] = lax.dot_general(u_scr[...], w_ref[...].astype(BF16), NT_DIMS, preferred_element_type=F32)


def _inproj(x, g1, w_in_t, n_main, gate_row0, n_gate, wif, wift, bif, bift, *, tm, tn):
    n, d = x.shape
    ng = n_gate // tn
    c = n_gate + n_main
    assert n_main % tn == 0 and n_gate % tn == 0
    q = 16
    assert gate_row0 % q == 0 and tn % q == 0
    w_row = lambda i, j: (q * jnp.where(j < ng, gate_row0 // q + j * (tn // q), (j - ng) * (tn // q)), 0)
    return pl.pallas_call(
        _inproj_kernel,
        grid=(n // tm, c // tn),
        in_specs=[
            pl.BlockSpec((tm, d), lambda i, j: (i, 0)),
            pl.BlockSpec((1, d), lambda i, j: (0, 0)),
            pl.BlockSpec((pl.Element(tn), pl.Element(d)), w_row),
            pl.BlockSpec((d, LANES), lambda i, j: (0, 0)),
            pl.BlockSpec((2 * N_HEADS, d), lambda i, j: (0, 0)),
            pl.BlockSpec((1, LANES), lambda i, j: (0, 0)),
            pl.BlockSpec((2 * N_HEADS, 1), lambda i, j: (0, 0)),
        ],
        out_specs=[
            pl.BlockSpec((tm, tn), lambda i, j: (i, j)),
            pl.BlockSpec((tm, LANES), lambda i, j: (i, 0)),
            pl.BlockSpec((2 * N_HEADS, tm), lambda i, j: (0, i)),
        ],
        out_shape=[
            jax.ShapeDtypeStruct((n, c), F32),
            jax.ShapeDtypeStruct((n, LANES), F32),
            jax.ShapeDtypeStruct((2 * N_HEADS, n), F32),
        ],
        scratch_shapes=[pltpu.VMEM((tm, d), BF16)],
        compiler_params=_cparams(("parallel", "arbitrary")),
        name="inproj",
    )(x, g1, w_in_t, wif, wift, bif, bift)


def _ret_kernel(cd_ref, q_ref, k_ref, v_ref, rg_ref, cos_ref, sin_ref, s0_ref, g_ref,
                din_ref, qd_ref, kd_ref, y_ref, sout_ref, s_scr):
    @pl.when(pl.program_id(1) == 0)
    def _():
        s_scr[...] = s0_ref[0]

    cos = cos_ref[...]
    sin = sin_ref[...]
    heads = range(N_HEADS)
    sl = [slice(h * DK_RET, (h + 1) * DK_RET) for h in heads]
    q = [q_ref[:, sl[h]] for h in heads]
    k = [k_ref[:, sl[h]] for h in heads]
    q = [x * cos + pltpu.roll(x, DK_RET // 2, 1) * sin for x in q]
    k = [(x * cos + pltpu.roll(x, DK_RET // 2, 1) * sin) * (DK_RET ** -0.5) for x in k]
    v = [v_ref[:, sl[h]].astype(BF16) for h in heads]
    s = [s_scr[h] for h in heads]
    att = [lax.dot_general(q[h].astype(BF16), k[h].astype(BF16), NT_DIMS, preferred_element_type=F32) * din_ref[h]
           for h in heads]
    o = [jnp.dot(att[h].astype(BF16), v[h], preferred_element_type=F32)
         + jnp.dot((q[h] * qd_ref[h]).astype(BF16), s[h].astype(BF16), preferred_element_type=F32)
         for h in heads]
    s_new = [cd_ref[h] * s[h] + jnp.dot((k[h] * kd_ref[h]).T.astype(BF16), v[h], preferred_element_type=F32)
             for h in heads]
    for h in heads:
        s_scr[h] = s_new[h]
        sout_ref[0, h] = s_new[h]
        rg = rg_ref[:, sl[h]]
        y = _head_norm(o[h], g_ref[:, sl[h]]) * (rg * _sigmoid(rg))
        y_ref[:, sl[h]] = y.astype(BF16)


def _retention(z, row0, n_seq, t_len, L, cos, sin, s0, g_gn, tabs, colblk):
    din, qd, kd, cd = tabs
    nch = t_len // L
    rb0 = row0 // L
    w = N_HEADS * DK_RET
    zspec = lambda cb: pl.BlockSpec((L, w), lambda s, c: (rb0 + s * nch + c, cb))
    tspec = pl.BlockSpec((L, DK_RET), lambda s, c: (c, 0))
    const3 = lambda shape: pl.BlockSpec(shape, lambda s, c: (0, 0, 0))
    y, s_out = pl.pallas_call(
        _ret_kernel,
        grid=(n_seq, nch),
        in_specs=[
            pl.BlockSpec(memory_space=pltpu.SMEM),
            zspec(colblk["rq"]), zspec(colblk["rk"]), zspec(colblk["rv"]), zspec(colblk["rg"]),
            tspec, tspec,
            pl.BlockSpec((1, N_HEADS, DK_RET, DV_RET), lambda s, c: (s, 0, 0, 0)),
            pl.BlockSpec((1, w), lambda s, c: (0, 0)),
            const3((N_HEADS, L, L)), const3((N_HEADS, L, DK_RET)), const3((N_HEADS, L, DK_RET)),
        ],
        out_specs=[
            pl.BlockSpec((L, w), lambda s, c: (s * nch + c, 0)),
            pl.BlockSpec((1, N_HEADS, DK_RET, DV_RET), lambda s, c: (s, 0, 0, 0)),
        ],
        scratch_shapes=[pltpu.VMEM((N_HEADS, DK_RET, DV_RET), F32)],
        out_shape=[
            jax.ShapeDtypeStruct((n_seq * t_len, w), BF16),
            jax.ShapeDtypeStruct((n_seq, N_HEADS, DK_RET, DV_RET), F32),
        ],
        compiler_params=_cparams(("parallel", "arbitrary")),
        name="retention",
    )(cd, z, z, z, z, cos, sin, s0, g_gn, din, qd, kd)
    return y, s_out


def _retention_tables(L):
    log_g = np.log(1.0 - 2.0 ** (-5.0 - np.arange(N_HEADS, dtype=np.float64)))
    idx = np.arange(L, dtype=np.float64)
    rel = idx[:, None] - idx[None, :]
    din = np.where(rel >= 0, np.exp(rel[None] * log_g[:, None, None]), 0.0)
    qd = np.exp((idx[None, :] + 1.0) * log_g[:, None])[..., None] * np.ones((1, 1, DK_RET))
    kd = np.exp((L - 1.0 - idx[None, :]) * log_g[:, None])[..., None] * np.ones((1, 1, DK_RET))
    cd = np.exp(L * log_g)
    return (jnp.asarray(din, F32), jnp.asarray(qd, F32), jnp.asarray(kd, F32), jnp.asarray(cd, F32))


def _rotary_tables(pos0, t_len):
    half = DK_RET // 2
    inv = ROPE_BASE ** (-np.arange(half, dtype=np.float64) / half)
    ang = (pos0 + np.arange(t_len, dtype=np.float64))[:, None] * inv[None, :]
    c, s = np.cos(ang), np.sin(ang)
    return (jnp.asarray(np.concatenate([c, c], axis=-1), F32),
            jnp.asarray(np.concatenate([-s, s], axis=-1), F32))


def _ml_kernel(q_ref, k_ref, v_ref, mo_ref, gl_ref, glt_ref, c0_ref, n0_ref, m0_ref, g_ref,
               y_ref, cout_ref, nout_ref, mout_ref, c_scr, n_scr, m_scr):
    @pl.when(pl.program_id(1) == 0)
    def _():
        c_scr[...] = c0_ref[0]
        n_scr[...] = n0_ref[0]
        m_scr[...] = m0_ref[0]

    L = q_ref.shape[0]
    ii = lax.broadcasted_iota(I32, (L, L), 0)
    jj = lax.broadcasted_iota(I32, (L, L), 1)
    causal = jj <= ii
    gl = gl_ref[...]
    glt = glt_ref[0]
    m_all = m_scr[...]
    n_all = n_scr[...]
    heads = range(N_HEADS)
    qs = [slice(h * DK_ML, (h + 1) * DK_ML) for h in heads]
    vs = [slice(h * DV_ML, (h + 1) * DV_ML) for h in heads]
    qf = [q_ref[:, qs[h]] * (DK_ML ** -0.5) for h in heads]
    kf = [k_ref[:, qs[h]] for h in heads]
    vf = [v_ref[:, vs[h]] for h in heads]
    q = [x.astype(BF16) for x in qf]
    k = [x.astype(BF16) for x in kf]
    ig_c = [gl[:, h:h + 1] for h in heads]
    ig_r = [glt[h:h + 1, :] for h in heads]
    b_c = [jnp.sum(jnp.where(causal, jnp.broadcast_to(glt[N_HEADS + h:N_HEADS + h + 1, :], (L, L)), 0.0),
                   axis=1, keepdims=True) for h in heads]
    b_r = [jnp.sum(jnp.where(ii <= jj, jnp.broadcast_to(gl[:, N_HEADS + h:N_HEADS + h + 1], (L, L)), 0.0),
                   axis=0, keepdims=True) for h in heads]
    logw = [jnp.where(causal, b_c[h] - b_r[h] + ig_r[h], -jnp.inf) for h in heads]
    m_prev = [m_all[h:h + 1, 0:1] for h in heads]
    inter = [b_c[h] + m_prev[h] for h in heads]
    m_row = [jnp.maximum(inter[h], jnp.max(logw[h], axis=1, keepdims=True)) for h in heads]
    qk = [lax.dot_general(q[h], k[h], NT_DIMS, preferred_element_type=F32) for h in heads]
    w = [jnp.exp(logw[h] - m_row[h]) * qk[h] for h in heads]
    w_inter = [jnp.exp(inter[h] - m_row[h]) for h in heads]
    c_old = [c_scr[h] for h in heads]
    n_old = [n_all[h:h + 1, :] for h in heads]
    num = [jnp.dot(w[h].astype(BF16), vf[h].astype(BF16), preferred_element_type=F32)
           + w_inter[h] * lax.dot_general(q[h], c_old[h].astype(BF16), NT_DIMS, preferred_element_type=F32)
           for h in heads]
    den = [jnp.sum(w[h], axis=1, keepdims=True) + w_inter[h] * jnp.sum(qf[h] * n_old[h], axis=1, keepdims=True)
           for h in heads]
    hh = [num[h] / jnp.maximum(jnp.abs(den[h]), jnp.exp(-m_row[h])) for h in heads]
    m_new = [m_row[h][L - 1:L, :] for h in heads]
    b_last = [b_c[h][L - 1:L, :] for h in heads]
    decay = [jnp.exp(b_last[h] + m_prev[h] - m_new[h]) for h in heads]
    wk_c = [jnp.exp(b_last[h] - b_c[h] + ig_c[h] - m_new[h]) for h in heads]
    c_new = [decay[h] * c_old[h] + jnp.dot((vf[h] * wk_c[h]).T.astype(BF16), k[h], preferred_element_type=F32)
             for h in heads]
    n_new = [decay[h] * n_old[h] + jnp.sum(wk_c[h] * kf[h], axis=0, keepdims=True) for h in heads]
    for h in heads:
        c_scr[h] = c_new[h]
        cout_ref[0, h] = c_new[h]
        y = _sigmoid(mo_ref[:, vs[h]]) * _head_norm(hh[h], g_ref[:, vs[h]])
        y_ref[:, vs[h]] = y.astype(BF16)
    m_out = jnp.concatenate([jnp.broadcast_to(m, (1, LANES)) for m in m_new], axis=0)
    n_out = jnp.concatenate(n_new, axis=0)
    m_scr[...] = m_out
    n_scr[...] = n_out
    mout_ref[0] = m_out
    nout_ref[0] = n_out


def _mlstm(z, gl, glt, row0, n_seq, t_len, L, c0, n0, m0, g_gn, colblk):
    nch = t_len // L
    rb0 = row0 // L
    wqk = N_HEADS * DK_ML
    wv = N_HEADS * DV_ML
    qspec = lambda cb: pl.BlockSpec((L, wqk), lambda s, c: (rb0 + s * nch + c, cb))
    vspec = lambda cb: pl.BlockSpec((L, wv), lambda s, c: (rb0 + s * nch + c, cb))
    m0b = jnp.broadcast_to(m0[..., None], (n_seq, N_HEADS, LANES))
    glt = glt[:, row0:row0 + n_seq * t_len].reshape(2 * N_HEADS, n_seq * nch, L).transpose(1, 0, 2)
    y, c_out, n_out, m_out = pl.pallas_call(
        _ml_kernel,
        grid=(n_seq, nch),
        in_specs=[
            qspec(colblk["mq"]), qspec(colblk["mk"]), vspec(colblk["mv"]), vspec(colblk["mo"]),
            pl.BlockSpec((L, LANES), lambda s, c: (rb0 + s * nch + c, 0)),
            pl.BlockSpec((1, 2 * N_HEADS, L), lambda s, c: (s * nch + c, 0, 0)),
            pl.BlockSpec((1, N_HEADS, DV_ML, DK_ML), lambda s, c: (s, 0, 0, 0)),
            pl.BlockSpec((1, N_HEADS, DK_ML), lambda s, c: (s, 0, 0)),
            pl.BlockSpec((1, N_HEADS, LANES), lambda s, c: (s, 0, 0)),
            pl.BlockSpec((1, wv), lambda s, c: (0, 0)),
        ],
        out_specs=[
            pl.BlockSpec((L, wv), lambda s, c: (s * nch + c, 0)),
            pl.BlockSpec((1, N_HEADS, DV_ML, DK_ML), lambda s, c: (s, 0, 0, 0)),
            pl.BlockSpec((1, N_HEADS, DK_ML), lambda s, c: (s, 0, 0)),
            pl.BlockSpec((1, N_HEADS, LANES), lambda s, c: (s, 0, 0)),
        ],
        out_shape=[
            jax.ShapeDtypeStruct((n_seq * t_len, wv), BF16),
            jax.ShapeDtypeStruct((n_seq, N_HEADS, DV_ML, DK_ML), F32),
            jax.ShapeDtypeStruct((n_seq, N_HEADS, DK_ML), F32),
            jax.ShapeDtypeStruct((n_seq, N_HEADS, LANES), F32),
        ],
        scratch_shapes=[
            pltpu.VMEM((N_HEADS, DV_ML, DK_ML), F32),
            pltpu.VMEM((N_HEADS, DK_ML), F32),
            pltpu.VMEM((N_HEADS, LANES), F32),
        ],
        compiler_params=_cparams(("parallel", "arbitrary")),
        name="mlstm",
    )(z, z, z, z, gl, glt, c0, n0, m0b, g_gn)
    return y, c_out, n_out, m_out[..., 0]


def _merge_kernel(x_ref, ya_ref, yb_ref, ga_ref, gb_ref, wa_ref, wb_ref, wo_ref, g2_ref,
                  wrt_ref, brt_ref, h_ref, u2_ref, lgt_ref):
    bra = jnp.dot(ya_ref[...], wa_ref[...], preferred_element_type=F32)
    brb = jnp.dot(yb_ref[...], wb_ref[...], preferred_element_type=F32)
    merged = _sigmoid(ga_ref[...]) * bra + _sigmoid(gb_ref[...]) * brb
    hmid = x_ref[...] + jnp.dot(merged.astype(BF16), wo_ref[...], preferred_element_type=F32)
    h_ref[...] = hmid
    u2 = _rms(hmid, g2_ref[...])
    u2_ref[...] = u2
    lgt_ref[...] = (lax.dot_general(wrt_ref[...], u2.astype(BF16), NT_DIMS, preferred_element_type=F32)
                    + brt_ref[...])


def _merge(x, ya, yb, z, wa, wb, wo, g2, wrt, brt, *, tm, colblk):
    n, d = x.shape
    e = wrt.shape[0]
    full = lambda a: pl.BlockSpec(a.shape, lambda i: (0,) * a.ndim, pipeline_mode=pl.Buffered(1))
    return pl.pallas_call(
        _merge_kernel,
        grid=(n // tm,),
        in_specs=[
            pl.BlockSpec((tm, d), lambda i: (i, 0)),
            pl.BlockSpec((tm, ya.shape[1]), lambda i: (i, 0)),
            pl.BlockSpec((tm, yb.shape[1]), lambda i: (i, 0)),
            pl.BlockSpec((tm, d), lambda i: (i, colblk["ga"])),
            pl.BlockSpec((tm, d), lambda i: (i, colblk["gb"])),
            full(wa), full(wb), full(wo), full(g2), full(wrt), full(brt),
        ],
        out_specs=[
            pl.BlockSpec((tm, d), lambda i: (i, 0)),
            pl.BlockSpec((tm, d), lambda i: (i, 0)),
            pl.BlockSpec((e, tm), lambda i: (0, i)),
        ],
        out_shape=[
            jax.ShapeDtypeStruct((n, d), F32),
            jax.ShapeDtypeStruct((n, d), F32),
            jax.ShapeDtypeStruct((e, n), F32),
        ],
        compiler_params=_cparams(("parallel",)),
        name="merge",
    )(x, ya, yb, z, z, wa, wb, wo, g2, wrt, brt)


def _route_kernel(lg_ref, ti_ref, tp_ref):
    l = lg_ref[...]
    e = l.shape[0]
    eid = lax.broadcasted_iota(I32, l.shape, 0)
    vals, idxs = [], []
    for _ in range(TOP_K):
        mx = jnp.max(l, axis=0, keepdims=True)
        ix = jnp.min(jnp.where(l == mx, eid, e), axis=0, keepdims=True)
        vals.append(mx)
        idxs.append(ix)
        l = jnp.where(eid == ix, -jnp.inf, l)
    ex = [jnp.exp(v - vals[0]) for v in vals]
    tot = ex[0] + ex[1] + ex[2] + ex[3]
    for kk in range(TOP_K):
        ti_ref[kk:kk + 1, :] = idxs[kk]
        tp_ref[kk:kk + 1, :] = ex[kk] / tot


def _route(lgt, *, tn):
    e, n = lgt.shape
    return pl.pallas_call(
        _route_kernel,
        grid=(n // tn,),
        in_specs=[pl.BlockSpec((e, tn), lambda i: (0, i))],
        out_specs=[pl.BlockSpec((TOP_K, tn), lambda i: (0, i)),
                   pl.BlockSpec((TOP_K, tn), lambda i: (0, i))],
        out_shape=[jax.ShapeDtypeStruct((TOP_K, n), I32),
                   jax.ShapeDtypeStruct((TOP_K, n), F32)],
        compiler_params=_cparams(("parallel",)),
        name="route",
    )(lgt)


def _dispatch_kernel(pos_ref, pad0_ref, padn_ref, nu_ref, up_ref, us_ref, xs_ref, ztile, sem, zsem,
                     *, np_tiles, tr):
    i = pl.program_id(0)
    tm = up_ref.shape[0]
    n_exp = pad0_ref.shape[0]
    n_tiles = xs_ref.shape[0] // tr

    @pl.when(i == 0)
    def _():
        ztile[...] = jnp.zeros_like(ztile)

        def zero_copy(row0, rows):
            return pltpu.make_async_copy(ztile.at[pl.ds(0, rows)], xs_ref.at[pl.ds(row0, rows)], zsem)

        def for_all_zero_copies(fn):
            def per_expert(e, carry):
                a, n = pad0_ref[e], padn_ref[e]
                head = jnp.minimum((-a) & (SUBLANES - 1), n)
                lax.fori_loop(0, head, lambda r, c: (fn(zero_copy(a + r, 1)), c)[1], 0)
                a8, rem = a + head, n - head
                k = SUBLANES
                while k < tr:
                    @pl.when((rem & k) != 0)
                    def _(k=k):
                        fn(zero_copy(pl.multiple_of(a8 + (rem & (k - 1)), SUBLANES), k))
                    k *= 2
                return carry

            lax.fori_loop(0, n_exp, per_expert, 0)
            lax.fori_loop(nu_ref[0], n_tiles,
                          lambda t, c: (fn(zero_copy(pl.multiple_of(t * tr, tr), tr)), c)[1], 0)

        for_all_zero_copies(lambda cp: cp.start())
        for_all_zero_copies(lambda cp: cp.wait())

    def scatter(u_ref):
        def row_copy(r, kk):
            return pltpu.make_async_copy(u_ref.at[pl.ds(r, 1)], xs_ref.at[pl.ds(pos_ref[kk, r], 1)], sem)

        def start(r, carry):
            for kk in range(TOP_K):
                row_copy(r, kk).start(priority=kk % 2)
            return carry

        lax.fori_loop(0, tm, start, 0, unroll=DMA_ISSUE_UNROLL)
        for kk in range(TOP_K):
            pltpu.make_async_copy(u_ref, xs_ref.at[pl.ds(0, tm)], sem).wait()

    @pl.when(i < np_tiles)
    def _():
        scatter(up_ref)

    @pl.when(i >= np_tiles)
    def _():
        scatter(us_ref)


def _dispatch(pos, pads, nused, u2_p, u2_s, n_slots, *, tm, tr):
    pad0, padn = pads
    (n_p, c), n_s = u2_p.shape, u2_s.shape[0]
    npt = n_p // tm
    smem = pl.BlockSpec(memory_space=pltpu.SMEM)
    return pl.pallas_call(
        functools.partial(_dispatch_kernel, np_tiles=npt, tr=tr),
        grid=((n_p + n_s) // tm,),
        in_specs=[
            pl.BlockSpec((TOP_K, tm), lambda i: (0, i), memory_space=pltpu.SMEM),
            smem, smem, smem,
            pl.BlockSpec((tm, c), lambda i: (jnp.minimum(i, npt - 1), 0)),
            pl.BlockSpec((tm, c), lambda i: (jnp.maximum(i - npt, 0), 0)),
        ],
        out_specs=pl.BlockSpec(memory_space=pl.ANY),
        out_shape=jax.ShapeDtypeStruct((n_slots, c), u2_p.dtype),
        scratch_shapes=[pltpu.VMEM((tr, c), u2_p.dtype),
                        pltpu.SemaphoreType.DMA(()), pltpu.SemaphoreType.DMA(())],
        compiler_params=_cparams(("arbitrary",)),
        name="dispatch",
    )(pos, pad0, padn, nused, u2_p, u2_s)


def _stage_weights(first, te_ref, gn_ref, gi_ref, ng_ref, w_hbm, w_stage, sems):
    j, i = pl.program_id(0), pl.program_id(1)
    tw = w_stage[0].shape[2]
    slot = (j * ng_ref[0] + gi_ref[i]) & 1

    def copies(e, col, s):
        cols = pl.ds(pl.multiple_of(col * tw, tw), tw)
        return [pltpu.make_async_copy(w.at[e, :, cols], st.at[s], sems.at[k, s])
                for k, (w, st) in enumerate(zip(w_hbm, w_stage))]

    @pl.when(first)
    def _():
        e = te_ref[i]

        @pl.when(jnp.logical_and(i == 0, j == 0))
        def _():
            for cp in copies(e, j, slot):
                cp.start()

        for cp in copies(e, j, slot):
            cp.wait()

        nxt = gn_ref[i]

        @pl.when(nxt >= 0)
        def _():
            for cp in copies(nxt, j, 1 - slot):
                cp.start()

        @pl.when(jnp.logical_and(nxt < 0, j + 1 < pl.num_programs(0)))
        def _():
            for cp in copies(te_ref[0], j + 1, 1 - slot):
                cp.start()

    return slot


def _for_row_count(nv, out_ref, body):
    tr = out_ref.shape[0]
    half = tr // 2

    @pl.when(nv > half)
    def _():
        body(tr)

    @pl.when(nv <= half)
    def _():
        @pl.when(nv > 0)
        def _():
            body(half)

        @pl.when(nv <= 0)
        def _():
            out_ref[:half, :] = jnp.zeros((half, out_ref.shape[1]), out_ref.dtype)

        out_ref[half:, :] = jnp.zeros((tr - half, out_ref.shape[1]), out_ref.dtype)


def _expert_up_kernel(te_ref, nu_ref, gf_ref, gn_ref, nv_ref, gi_ref, ng_ref, xs_ref, wg_hbm, wu_hbm,
                      bg_ref, bu_ref, h_ref, wg_st, wu_st, sems):
    del nu_ref
    i = pl.program_id(1)
    slot = _stage_weights(gf_ref[i] == 1, te_ref, gn_ref, gi_ref, ng_ref, (wg_hbm, wu_hbm), (wg_st, wu_st), sems)

    def body(rows):
        x = xs_ref[:rows, :].astype(BF16)
        g = jnp.dot(x, wg_st[slot].astype(BF16), preferred_element_type=F32) + bg_ref[0]
        up = jnp.dot(x, wu_st[slot].astype(BF16), preferred_element_type=F32) + bu_ref[0]
        g = jnp.minimum(g, SWIGLU_LIMIT)
        up = jnp.clip(up, -SWIGLU_LIMIT, SWIGLU_LIMIT)
        h_ref[:rows, :] = (g * _sigmoid(SWIGLU_ALPHA * g) * (up + 1.0)).astype(BF16)

    _for_row_count(nv_ref[i], h_ref, body)


def _expert_up(tiles, xs, w_gate, w_up, b_gate, b_up, *, tr, tf):
    p, c = xs.shape
    e, d, f = w_gate.shape
    row = lambda j, i, te_, nu_, *_: (jnp.minimum(i, nu_[0] - 1), 0)
    bmap = lambda j, i, te_, *_: (te_[i], 0, j)
    return pl.pallas_call(
        _expert_up_kernel,
        grid_spec=pltpu.PrefetchScalarGridSpec(
            num_scalar_prefetch=len(tiles),
            grid=(f // tf, p // tr),
            in_specs=[
                pl.BlockSpec((tr, c), row),
                pl.BlockSpec(memory_space=pl.ANY),
                pl.BlockSpec(memory_space=pl.ANY),
                pl.BlockSpec((1, 1, tf), bmap),
                pl.BlockSpec((1, 1, tf), bmap),
            ],
            out_specs=pl.BlockSpec((tr, tf), lambda j, i, *_: (i, j)),
            scratch_shapes=[pltpu.VMEM((2, d, tf), F32), pltpu.VMEM((2, d, tf), F32),
                            pltpu.SemaphoreType.DMA((2, 2))],
        ),
        out_shape=jax.ShapeDtypeStruct((p, f), BF16),
        compiler_params=_cparams(("arbitrary", "arbitrary")),
        name="expert_up",
    )(*tiles, xs, w_gate, w_up, b_gate.reshape(e, 1, f), b_up.reshape(e, 1, f))


def _expert_down_kernel(te_ref, nu_ref, gf_ref, gn_ref, nv_ref, gi_ref, ng_ref, h_ref, wd_hbm, bd_ref, y_ref,
                        wd_st, sems):
    del nu_ref
    i = pl.program_id(1)
    slot = _stage_weights(gf_ref[i] == 1, te_ref, gn_ref, gi_ref, ng_ref, (wd_hbm,), (wd_st,), sems)

    def body(rows):
        y_ref[:rows, :] = (jnp.dot(h_ref[:rows, :], wd_st[slot].astype(BF16), preferred_element_type=F32)
                           + bd_ref[0])

    _for_row_count(nv_ref[i], y_ref, body)


def _expert_down(tiles, hmid, w_down, b_down, *, tr, tn):
    p, f = hmid.shape
    e, _, d = w_down.shape
    return pl.pallas_call(
        _expert_down_kernel,
        grid_spec=pltpu.PrefetchScalarGridSpec(
            num_scalar_prefetch=len(tiles),
            grid=(d // tn, p // tr),
            in_specs=[
                pl.BlockSpec((tr, f), lambda j, i, te_, nu_, *_: (jnp.minimum(i, nu_[0] - 1), 0)),
                pl.BlockSpec(memory_space=pl.ANY),
                pl.BlockSpec((1, 1, tn), lambda j, i, te_, *_: (te_[i], 0, j)),
            ],
            out_specs=pl.BlockSpec((tr, tn), lambda j, i, *_: (i, j)),
            scratch_shapes=[pltpu.VMEM((2, f, tn), F32), pltpu.SemaphoreType.DMA((1, 2))],
        ),
        out_shape=jax.ShapeDtypeStruct((p, d), F32),
        compiler_params=_cparams(("arbitrary", "arbitrary")),
        name="expert_down",
    )(*tiles, hmid, w_down, b_down.reshape(e, 1, d))


def _combine_kernel(pos_ref, h_ref, p_ref, gf_ref, ys_ref, o_ref, buf, sem):
    tm = h_ref.shape[0]

    def row_copy(r, kk):
        return pltpu.make_async_copy(ys_ref.at[pl.ds(pos_ref[kk, r], 1)], buf.at[kk, pl.ds(r, 1)], sem)

    def start(r, carry):
        for kk in range(TOP_K):
            row_copy(r, kk).start(priority=kk % 2)
        return carry

    lax.fori_loop(0, tm, start, 0, unroll=DMA_ISSUE_UNROLL)
    for kk in range(TOP_K):
        pltpu.make_async_copy(ys_ref.at[pl.ds(0, tm)], buf.at[kk], sem).wait()
    p = p_ref[...]
    acc = h_ref[...]
    for kk in range(TOP_K):
        acc = acc + p[:, kk:kk + 1] * buf[kk]
    o_ref[...] = _rms(acc, gf_ref[...])


def _combine(pos, h, probs_t, g_final, ys, *, tm):
    n, d = h.shape
    return pl.pallas_call(
        _combine_kernel,
        grid=(n // tm,),
        in_specs=[
            pl.BlockSpec((TOP_K, tm), lambda i: (0, i), memory_space=pltpu.SMEM),
            pl.BlockSpec((tm, d), lambda i: (i, 0)),
            pl.BlockSpec((tm, TOP_K), lambda i: (i, 0)),
            pl.BlockSpec((1, d), lambda i: (0, 0)),
            pl.BlockSpec(memory_space=pl.ANY),
        ],
        out_specs=pl.BlockSpec((tm, d), lambda i: (i, 0)),
        out_shape=jax.ShapeDtypeStruct((n, d), F32),
        scratch_shapes=[pltpu.VMEM((TOP_K, tm, d), F32), pltpu.SemaphoreType.DMA(())],
        compiler_params=_cparams(("arbitrary",)),
        name="combine",
    )(pos, h, probs_t, g_final, ys)


def _slot_layout(top_i, n_experts, tr, n_tiles):
    onehot = jnp.sum((top_i[:, :, None] == jnp.arange(n_experts, dtype=I32)[None, None, :]).astype(I32), axis=0)
    rank = jnp.cumsum(onehot, axis=0) - onehot
    counts = jnp.sum(onehot, axis=0)
    cpad = ((counts + tr - 1) // tr) * tr
    cend = jnp.cumsum(cpad)
    off = cend - cpad
    pos = jnp.take_along_axis((off[None, :] + rank).T, top_i, axis=0)
    nused = (cend[-1] // tr).astype(I32)
    tile_row = jnp.arange(n_tiles, dtype=I32) * tr
    te = jnp.sum((cend[None, :] <= tile_row[:, None]).astype(I32), axis=1)
    te = jnp.minimum(te, n_experts - 1)
    used = jnp.arange(n_tiles) < nused
    te = jnp.where(used, te, te[nused - 1])
    gfirst = jnp.logical_and(used, jnp.concatenate([jnp.ones((1,), bool), te[1:] != te[:-1]])).astype(I32)
    eid = jnp.arange(n_experts, dtype=I32)
    later = jnp.logical_and(eid[None, :] > eid[:, None], counts[None, :] > 0)
    nxt = jnp.min(jnp.where(later, eid[None, :], n_experts), axis=1)
    gnext = jnp.where(nxt < n_experts, nxt, -1).astype(I32)[te]
    nvalid = jnp.where(used, jnp.clip(counts[te] - (tile_row - off[te]), 0, tr), 0).astype(I32)
    gidx = (jnp.cumsum(gfirst) - 1).astype(I32)
    ngroups = jnp.sum(gfirst).astype(I32).reshape(1)
    pads = ((off + counts).astype(I32), (cpad - counts).astype(I32))
    return pos.astype(I32), pads, (te, nused.reshape(1), gfirst, gnext, nvalid, gidx, ngroups)


def kernel(x_prompt, x_sample, state_ret_S, state_mlstm_C, state_mlstm_n, state_mlstm_m, g_norm1, w_in, b_if,
           g_ret_gn, g_ml_gn, w_ret_br, w_ml_br, w_out, g_norm2, w_router, b_router, w_gate, b_gate, w_up, b_up,
           w_down, b_down, g_final):
    bp, tp, d = x_prompt.shape
    bs, ts, _ = x_sample.shape
    depth = w_in.shape[0]
    n_experts = w_router.shape[-1]
    n_p, n_s = bp * tp, bs * ts
    n = n_p + n_s

    ret_qk, ret_v = N_HEADS * DK_RET, N_HEADS * DV_RET
    ml_qk, ml_v = N_HEADS * DK_ML, N_HEADS * DV_ML
    o_mi = 2 * ret_qk + 2 * ret_v + 2 * ml_qk + 2 * ml_v
    o_ga = o_mi + 2 * N_HEADS
    colblk = {"ga": 0, "gb": 1}
    base = 2 * d
    colblk.update(rq=base // ret_qk, rk=(base + ret_qk) // ret_qk, rv=(base + 2 * ret_qk) // ret_v,
                  rg=(base + 2 * ret_qk + ret_v) // ret_v)
    mbase = base + 2 * ret_qk + 2 * ret_v
    colblk.update(mq=mbase // ml_qk, mk=(mbase + ml_qk) // ml_qk, mv=(mbase + 2 * ml_qk) // ml_v,
                  mo=(mbase + 2 * ml_qk + ml_v) // ml_v)

    assert depth == 1, "one layer: the combine kernel applies the final norm"
    tr = 512
    n_slots = TOP_K * n + n_experts * tr
    n_tiles = n_slots // tr
    tm = 256
    assert n_p % tm == 0 and n_s % tm == 0
    tmd = 2 * tm if (n_p % (2 * tm) == 0 and n_s % (2 * tm) == 0) else tm

    wl_t = w_in[0].T
    wift = wl_t[o_mi:o_ga].astype(BF16)
    wif = jnp.pad(wift.T, ((0, 0), (0, LANES - 2 * N_HEADS)))
    bflat = b_if[0].reshape(1, 2 * N_HEADS)
    bif = jnp.pad(bflat, ((0, 0), (0, LANES - 2 * N_HEADS)))
    bift = bflat.reshape(2 * N_HEADS, 1)
    g1 = g_norm1[0].reshape(1, d)
    g_r = g_ret_gn[0].reshape(1, ret_v)
    g_m = g_ml_gn[0].reshape(1, ml_v)
    wa, wb, wo = w_ret_br[0].astype(BF16), w_ml_br[0].astype(BF16), w_out[0].astype(BF16)
    g2 = g_norm2[0].reshape(1, d)
    wrt = w_router[0].T.astype(BF16)
    brt = b_router[0].reshape(n_experts, 1)

    def mixers(x, n_seq, t_len, L, L_ml, pos0, s0, c0, n0, m0):
        assert t_len % L == 0 and t_len % L_ml == 0
        z, gl, glt = _inproj(x, g1, wl_t, o_mi, o_ga, 2 * d, wif, wift, bif, bift,
                             tm=min(1024, x.shape[0]), tn=1024)
        cos, sin = _rotary_tables(pos0, t_len)
        ya, s_new = _retention(z, 0, n_seq, t_len, L, cos, sin, s0, g_r, _retention_tables(L), colblk)
        yb, c_new, n_new, m_new = _mlstm(z, gl, glt, 0, n_seq, t_len, L_ml, c0, n0, m0, g_m, colblk)
        h, u2, lgt = _merge(x, ya, yb, z, wa, wb, wo, g2, wrt, brt, tm=tm, colblk=colblk)
        return h, u2, lgt, (s_new, c_new, n_new, m_new)

    zeros = lambda *s: jnp.zeros(s, F32)
    h_p, u2_p, lgt_p, st_p = mixers(
        x_prompt.reshape(n_p, d), bp, tp, min(512, tp), min(512, tp), 0,
        zeros(bp, N_HEADS, DK_RET, DV_RET), zeros(bp, N_HEADS, DV_ML, DK_ML), zeros(bp, N_HEADS, DK_ML),
        zeros(bp, N_HEADS))
    h_s, u2_s, lgt_s, st_s = mixers(
        x_sample.reshape(n_s, d), bs, ts, min(64, ts), min(64, ts), PAST_LEN,
        state_ret_S[0].astype(F32), state_mlstm_C[0].astype(F32), state_mlstm_n[0].astype(F32),
        state_mlstm_m[0].astype(F32))

    top_i, top_p = _route(jnp.concatenate([lgt_p, lgt_s], axis=1), tn=min(2048, n_s))
    pos, pads, tiles = _slot_layout(top_i, n_experts, tr, n_tiles)
    probs_t = top_p.T

    xs = _dispatch(pos, pads, tiles[1], u2_p, u2_s, n_slots, tm=tmd, tr=tr)
    hmid = _expert_up(tiles, xs, w_gate[0], w_up[0], b_gate[0], b_up[0], tr=tr, tf=1024)
    ys = _expert_down(tiles, hmid, w_down[0], b_down[0], tr=tr, tn=d)

    gf = g_final.reshape(1, d)
    y_p = _combine(pos[:, :n_p], h_p, probs_t[:n_p], gf, ys, tm=tmd)
    y_s = _combine(pos[:, n_p:], h_s, probs_t[n_p:], gf, ys, tm=tmd)

    return (y_p.reshape(bp, tp, d), y_s.reshape(bs, ts, d),
            st_p[0][None], st_p[1][None], st_p[2][None], st_p[3][None],
            st_s[0][None], st_s[1][None], st_s[2][None], st_s[3][None])
```

```python
import functools

import jax
import jax.numpy as jnp
import numpy as np
from jax import lax
from jax.experimental import pallas as pl
from jax.experimental.pallas import tpu as pltpu

F32 = jnp.float32
BF16 = jnp.bfloat16
I32 = jnp.int32

N_HEADS = 8
DK_RET = 128
DV_RET = 128
DK_ML = 64
DV_ML = 128
TOP_K = 4
PAST_LEN = 1024
ROPE_BASE = 10000.0
NORM_EPS = 1e-6
SWIGLU_ALPHA = 1.702
SWIGLU_LIMIT = 7.0

LANES = 128
SUBLANES = 8
VMEM_LIMIT = 56 * 1024 * 1024

DMA_ISSUE_UNROLL = 8

NT_DIMS = (((1,), (1,)), ((), ()))


def _cparams(sem):
    return pltpu.CompilerParams(dimension_semantics=sem, vmem_limit_bytes=VMEM_LIMIT)


def _sigmoid(x):
    return 1.0 / (1.0 + jnp.exp(-x))


def _log_sigmoid(x):
    return jnp.minimum(x, 0.0) - jnp.log(1.0 + jnp.exp(-jnp.abs(x)))


def _rms(x, g):
    return x * lax.rsqrt(jnp.mean(x * x, axis=-1, keepdims=True) + NORM_EPS) * g


def _head_norm(o, g):
    mu = jnp.mean(o, axis=-1, keepdims=True)
    d = o - mu
    var = jnp.mean(d * d, axis=-1, keepdims=True)
    return d * lax.rsqrt(var + NORM_EPS) * g


def _inproj_kernel(x_ref, g_ref, w_ref, wif_ref, wift_ref, bif_ref, bift_ref,
                   z_ref, gl_ref, glt_ref, u_scr):
    @pl.when(pl.program_id(1) == 0)
    def _():
        ub = _rms(x_ref[...], g_ref[...]).astype(BF16)
        u_scr[...] = ub
        gp = jnp.dot(ub, wif_ref[...], preferred_element_type=F32) + bif_ref[...]
        col = lax.broadcasted_iota(I32, gp.shape, 1)
        gl_ref[...] = jnp.where(col < N_HEADS, gp, _log_sigmoid(gp))
        gpt = lax.dot_general(wift_ref[...], ub, NT_DIMS, preferred_element_type=F32) + bift_ref[...]
        row = lax.broadcasted_iota(I32, gpt.shape, 0)
        glt_ref[...] = jnp.where(row < N_HEADS, gpt, _log_sigmoid(gpt))

    z_ref[...] = lax.dot_general(u_scr[...], w_ref[...].astype(BF16), NT_DIMS, preferred_element_type=F32)


def _inproj(x, g1, w_in_t, n_main, gate_row0, n_gate, wif, wift, bif, bift, *, tm, tn):
    n, d = x.shape
    ng = n_gate // tn
    c = n_gate + n_main
    assert n_main % tn == 0 and n_gate % tn == 0
    q = 16
    assert gate_row0 % q == 0 and tn % q == 0
    w_row = lambda i, j: (q * jnp.where(j < ng, gate_row0 // q + j * (tn // q), (j - ng) * (tn // q)), 0)
    return pl.pallas_call(
        _inproj_kernel,
        grid=(n // tm, c // tn),
        in_specs=[
            pl.BlockSpec((tm, d), lambda i, j: (i, 0)),
            pl.BlockSpec((1, d), lambda i, j: (0, 0)),
            pl.BlockSpec((pl.Element(tn), pl.Element(d)), w_row),
            pl.BlockSpec((d, LANES), lambda i, j: (0, 0)),
            pl.BlockSpec((2 * N_HEADS, d), lambda i, j: (0, 0)),
            pl.BlockSpec((1, LANES), lambda i, j: (0, 0)),
            pl.BlockSpec((2 * N_HEADS, 1), lambda i, j: (0, 0)),
        ],
        out_specs=[
            pl.BlockSpec((tm, tn), lambda i, j: (i, j)),
            pl.BlockSpec((tm, LANES), lambda i, j: (i, 0)),
            pl.BlockSpec((2 * N_HEADS, tm), lambda i, j: (0, i)),
        ],
        out_shape=[
            jax.ShapeDtypeStruct((n, c), F32),
            jax.ShapeDtypeStruct((n, LANES), F32),
            jax.ShapeDtypeStruct((2 * N_HEADS, n), F32),
        ],
        scratch_shapes=[pltpu.VMEM((tm, d), BF16)],
        compiler_params=_cparams(("parallel", "arbitrary")),
        name="inproj",
    )(x, g1, w_in_t, wif, wift, bif, bift)


def _ret_kernel(cd_ref, q_ref, k_ref, v_ref, rg_ref, cos_ref, sin_ref, s0_ref, g_ref,
                din_ref, qd_ref, kd_ref, y_ref, sout_ref, s_scr):
    @pl.when(pl.program_id(1) == 0)
    def _():
        s_scr[...] = s0_ref[0]

    cos = cos_ref[...]
    sin = sin_ref[...]
    heads = range(N_HEADS)
    sl = [slice(h * DK_RET, (h + 1) * DK_RET) for h in heads]
    q = [q_ref[:, sl[h]] for h in heads]
    k = [k_ref[:, sl[h]] for h in heads]
    q = [x * cos + pltpu.roll(x, DK_RET // 2, 1) * sin for x in q]
    k = [(x * cos + pltpu.roll(x, DK_RET // 2, 1) * sin) * (DK_RET ** -0.5) for x in k]
    v = [v_ref[:, sl[h]].astype(BF16) for h in heads]
    s = [s_scr[h] for h in heads]
    att = [lax.dot_general(q[h].astype(BF16), k[h].astype(BF16), NT_DIMS, preferred_element_type=F32) * din_ref[h]
           for h in heads]
    o = [jnp.dot(att[h].astype(BF16), v[h], preferred_element_type=F32)
         + jnp.dot((q[h] * qd_ref[h]).astype(BF16), s[h].astype(BF16), preferred_element_type=F32)
         for h in heads]
    s_new = [cd_ref[h] * s[h] + jnp.dot((k[h] * kd_ref[h]).T.astype(BF16), v[h], preferred_element_type=F32)
             for h in heads]
    for h in heads:
        s_scr[h] = s_new[h]
        sout_ref[0, h] = s_new[h]
        rg = rg_ref[:, sl[h]]
        y = _head_norm(o[h], g_ref[:, sl[h]]) * (rg * _sigmoid(rg))
        y_ref[:, sl[h]] = y.astype(BF16)


def _retention(z, row0, n_seq, t_len, L, cos, sin, s0, g_gn, tabs, colblk):
    din, qd, kd, cd = tabs
    nch = t_len // L
    rb0 = row0 // L
    w = N_HEADS * DK_RET
    zspec = lambda cb: pl.BlockSpec((L, w), lambda s, c: (rb0 + s * nch + c, cb))
    tspec = pl.BlockSpec((L, DK_RET), lambda s, c: (c, 0))
    const3 = lambda shape: pl.BlockSpec(shape, lambda s, c: (0, 0, 0))
    y, s_out = pl.pallas_call(
        _ret_kernel,
        grid=(n_seq, nch),
        in_specs=[
            pl.BlockSpec(memory_space=pltpu.SMEM),
            zspec(colblk["rq"]), zspec(colblk["rk"]), zspec(colblk["rv"]), zspec(colblk["rg"]),
            tspec, tspec,
            pl.BlockSpec((1, N_HEADS, DK_RET, DV_RET), lambda s, c: (s, 0, 0, 0)),
            pl.BlockSpec((1, w), lambda s, c: (0, 0)),
            const3((N_HEADS, L, L)), const3((N_HEADS, L, DK_RET)), const3((N_HEADS, L, DK_RET)),
        ],
        out_specs=[
            pl.BlockSpec((L, w), lambda s, c: (s * nch + c, 0)),
            pl.BlockSpec((1, N_HEADS, DK_RET, DV_RET), lambda s, c: (s, 0, 0, 0)),
        ],
        scratch_shapes=[pltpu.VMEM((N_HEADS, DK_RET, DV_RET), F32)],
        out_shape=[
            jax.ShapeDtypeStruct((n_seq * t_len, w), BF16),
            jax.ShapeDtypeStruct((n_seq, N_HEADS, DK_RET, DV_RET), F32),
        ],
        compiler_params=_cparams(("parallel", "arbitrary")),
        name="retention",
    )(cd, z, z, z, z, cos, sin, s0, g_gn, din, qd, kd)
    return y, s_out


def _retention_tables(L):
    log_g = np.log(1.0 - 2.0 ** (-5.0 - np.arange(N_HEADS, dtype=np.float64)))
    idx = np.arange(L, dtype=np.float64)
    rel = idx[:, None] - idx[None, :]
    din = np.where(rel >= 0, np.exp(rel[None] * log_g[:, None, None]), 0.0)
    qd = np.exp((idx[None, :] + 1.0) * log_g[:, None])[..., None] * np.ones((1, 1, DK_RET))
    kd = np.exp((L - 1.0 - idx[None, :]) * log_g[:, None])[..., None] * np.ones((1, 1, DK_RET))
    cd = np.exp(L * log_g)
    return (jnp.asarray(din, F32), jnp.asarray(qd, F32), jnp.asarray(kd, F32), jnp.asarray(cd, F32))


def _rotary_tables(pos0, t_len):
    half = DK_RET // 2
    inv = ROPE_BASE ** (-np.arange(half, dtype=np.float64) / half)
    ang = (pos0 + np.arange(t_len, dtype=np.float64))[:, None] * inv[None, :]
    c, s = np.cos(ang), np.sin(ang)
    return (jnp.asarray(np.concatenate([c, c], axis=-1), F32),
            jnp.asarray(np.concatenate([-s, s], axis=-1), F32))


def _ml_kernel(q_ref, k_ref, v_ref, mo_ref, gl_ref, glt_ref, c0_ref, n0_ref, m0_ref, g_ref,
               y_ref, cout_ref, nout_ref, mout_ref, c_scr, n_scr, m_scr):
    @pl.when(pl.program_id(1) == 0)
    def _():
        c_scr[...] = c0_ref[0]
        n_scr[...] = n0_ref[0]
        m_scr[...] = m0_ref[0]

    L = q_ref.shape[0]
    ii = lax.broadcasted_iota(I32, (L, L), 0)
    jj = lax.broadcasted_iota(I32, (L, L), 1)
    causal = jj <= ii
    gl = gl_ref[...]
    glt = glt_ref[0]
    m_all = m_scr[...]
    n_all = n_scr[...]
    heads = range(N_HEADS)
    qs = [slice(h * DK_ML, (h + 1) * DK_ML) for h in heads]
    vs = [slice(h * DV_ML, (h + 1) * DV_ML) for h in heads]
    qf = [q_ref[:, qs[h]] * (DK_ML ** -0.5) for h in heads]
    kf = [k_ref[:, qs[h]] for h in heads]
    vf = [v_ref[:, vs[h]] for h in heads]
    q = [x.astype(BF16) for x in qf]
    k = [x.astype(BF16) for x in kf]
    ig_c = [gl[:, h:h + 1] for h in heads]
    ig_r = [glt[h:h + 1, :] for h in heads]
    b_c = [jnp.sum(jnp.where(causal, jnp.broadcast_to(glt[N_HEADS + h:N_HEADS + h + 1, :], (L, L)), 0.0),
                   axis=1, keepdims=True) for h in heads]
    b_r = [jnp.sum(jnp.where(ii <= jj, jnp.broadcast_to(gl[:, N_HEADS + h:N_HEADS + h + 1], (L, L)), 0.0),
                   axis=0, keepdims=True) for h in heads]
    logw = [jnp.where(causal, b_c[h] - b_r[h] + ig_r[h], -jnp.inf) for h in heads]
    m_prev = [m_all[h:h + 1, 0:1] for h in heads]
    inter = [b_c[h] + m_prev[h] for h in heads]
    m_row = [jnp.maximum(inter[h], jnp.max(logw[h], axis=1, keepdims=True)) for h in heads]
    qk = [lax.dot_general(q[h], k[h], NT_DIMS, preferred_element_type=F32) for h in heads]
    w = [jnp.exp(logw[h] - m_row[h]) * qk[h] for h in heads]
    w_inter = [jnp.exp(inter[h] - m_row[h]) for h in heads]
    c_old = [c_scr[h] for h in heads]
    n_old = [n_all[h:h + 1, :] for h in heads]
    num = [jnp.dot(w[h].astype(BF16), vf[h].astype(BF16), preferred_element_type=F32)
           + w_inter[h] * lax.dot_general(q[h], c_old[h].astype(BF16), NT_DIMS, preferred_element_type=F32)
           for h in heads]
    den = [jnp.sum(w[h], axis=1, keepdims=True) + w_inter[h] * jnp.sum(qf[h] * n_old[h], axis=1, keepdims=True)
           for h in heads]
    hh = [num[h] / jnp.maximum(jnp.abs(den[h]), jnp.exp(-m_row[h])) for h in heads]
    m_new = [m_row[h][L - 1:L, :] for h in heads]
    b_last = [b_c[h][L - 1:L, :] for h in heads]
    decay = [jnp.exp(b_last[h] + m_prev[h] - m_new[h]) for h in heads]
    wk_c = [jnp.exp(b_last[h] - b_c[h] + ig_c[h] - m_new[h]) for h in heads]
    c_new = [decay[h] * c_old[h] + jnp.dot((vf[h] * wk_c[h]).T.astype(BF16), k[h], preferred_element_type=F32)
             for h in heads]
    n_new = [decay[h] * n_old[h] + jnp.sum(wk_c[h] * kf[h], axis=0, keepdims=True) for h in heads]
    for h in heads:
        c_scr[h] = c_new[h]
        cout_ref[0, h] = c_new[h]
        y = _sigmoid(mo_ref[:, vs[h]]) * _head_norm(hh[h], g_ref[:, vs[h]])
        y_ref[:, vs[h]] = y.astype(BF16)
    m_out = jnp.concatenate([jnp.broadcast_to(m, (1, LANES)) for m in m_new], axis=0)
    n_out = jnp.concatenate(n_new, axis=0)
    m_scr[...] = m_out
    n_scr[...] = n_out
    mout_ref[0] = m_out
    nout_ref[0] = n_out


def _mlstm(z, gl, glt, row0, n_seq, t_len, L, c0, n0, m0, g_gn, colblk):
    nch = t_len // L
    rb0 = row0 // L
    wqk = N_HEADS * DK_ML
    wv = N_HEADS * DV_ML
    qspec = lambda cb: pl.BlockSpec((L, wqk), lambda s, c: (rb0 + s * nch + c, cb))
    vspec = lambda cb: pl.BlockSpec((L, wv), lambda s, c: (rb0 + s * nch + c, cb))
    m0b = jnp.broadcast_to(m0[..., None], (n_seq, N_HEADS, LANES))
    glt = glt[:, row0:row0 + n_seq * t_len].reshape(2 * N_HEADS, n_seq * nch, L).transpose(1, 0, 2)
    y, c_out, n_out, m_out = pl.pallas_call(
        _ml_kernel,
        grid=(n_seq, nch),
        in_specs=[
            qspec(colblk["mq"]), qspec(colblk["mk"]), vspec(colblk["mv"]), vspec(colblk["mo"]),
            pl.BlockSpec((L, LANES), lambda s, c: (rb0 + s * nch + c, 0)),
            pl.BlockSpec((1, 2 * N_HEADS, L), lambda s, c: (s * nch + c, 0, 0)),
            pl.BlockSpec((1, N_HEADS, DV_ML, DK_ML), lambda s, c: (s, 0, 0, 0)),
            pl.BlockSpec((1, N_HEADS, DK_ML), lambda s, c: (s, 0, 0)),
            pl.BlockSpec((1, N_HEADS, LANES), lambda s, c: (s, 0, 0)),
            pl.BlockSpec((1, wv), lambda s, c: (0, 0)),
        ],
        out_specs=[
            pl.BlockSpec((L, wv), lambda s, c: (s * nch + c, 0)),
            pl.BlockSpec((1, N_HEADS, DV_ML, DK_ML), lambda s, c: (s, 0, 0, 0)),
            pl.BlockSpec((1, N_HEADS, DK_ML), lambda s, c: (s, 0, 0)),
            pl.BlockSpec((1, N_HEADS, LANES), lambda s, c: (s, 0, 0)),
        ],
        out_shape=[
            jax.ShapeDtypeStruct((n_seq * t_len, wv), BF16),
            jax.ShapeDtypeStruct((n_seq, N_HEADS, DV_ML, DK_ML), F32),
            jax.ShapeDtypeStruct((n_seq, N_HEADS, DK_ML), F32),
            jax.ShapeDtypeStruct((n_seq, N_HEADS, LANES), F32),
        ],
        scratch_shapes=[
            pltpu.VMEM((N_HEADS, DV_ML, DK_ML), F32),
            pltpu.VMEM((N_HEADS, DK_ML), F32),
            pltpu.VMEM((N_HEADS, LANES), F32),
        ],
        compiler_params=_cparams(("parallel", "arbitrary")),
        name="mlstm",
    )(z, z, z, z, gl, glt, c0, n0, m0b, g_gn)
    return y, c_out, n_out, m_out[..., 0]


def _merge_kernel(x_ref, ya_ref, yb_ref, ga_ref, gb_ref, wa_ref, wb_ref, wo_ref, g2_ref,
                  wrt_ref, brt_ref, h_ref, u2_ref, lgt_ref):
    bra = jnp.dot(ya_ref[...], wa_ref[...], preferred_element_type=F32)
    brb = jnp.dot(yb_ref[...], wb_ref[...], preferred_element_type=F32)
    merged = _sigmoid(ga_ref[...]) * bra + _sigmoid(gb_ref[...]) * brb
    hmid = x_ref[...] + jnp.dot(merged.astype(BF16), wo_ref[...], preferred_element_type=F32)
    h_ref[...] = hmid
    u2 = _rms(hmid, g2_ref[...])
    u2_ref[...] = u2
    lgt_ref[...] = (lax.dot_general(wrt_ref[...], u2.astype(BF16), NT_DIMS, preferred_element_type=F32)
                    + brt_ref[...])


def _merge(x, ya, yb, z, wa, wb, wo, g2, wrt, brt, *, tm, colblk):
    n, d = x.shape
    e = wrt.shape[0]
    full = lambda a: pl.BlockSpec(a.shape, lambda i: (0,) * a.ndim, pipeline_mode=pl.Buffered(1))
    return pl.pallas_call(
        _merge_kernel,
        grid=(n // tm,),
        in_specs=[
            pl.BlockSpec((tm, d), lambda i: (i, 0)),
            pl.BlockSpec((tm, ya.shape[1]), lambda i: (i, 0)),
            pl.BlockSpec((tm, yb.shape[1]), lambda i: (i, 0)),
            pl.BlockSpec((tm, d), lambda i: (i, colblk["ga"])),
            pl.BlockSpec((tm, d), lambda i: (i, colblk["gb"])),
            full(wa), full(wb), full(wo), full(g2), full(wrt), full(brt),
        ],
        out_specs=[
            pl.BlockSpec((tm, d), lambda i: (i, 0)),
            pl.BlockSpec((tm, d), lambda i: (i, 0)),
            pl.BlockSpec((e, tm), lambda i: (0, i)),
        ],
        out_shape=[
            jax.ShapeDtypeStruct((n, d), F32),
            jax.ShapeDtypeStruct((n, d), F32),
            jax.ShapeDtypeStruct((e, n), F32),
        ],
        compiler_params=_cparams(("parallel",)),
        name="merge",
    )(x, ya, yb, z, z, wa, wb, wo, g2, wrt, brt)


def _route_kernel(lg_ref, ti_ref, tp_ref):
    l = lg_ref[...]
    e = l.shape[0]
    eid = lax.broadcasted_iota(I32, l.shape, 0)
    vals, idxs = [], []
    for _ in range(TOP_K):
        mx = jnp.max(l, axis=0, keepdims=True)
        ix = jnp.min(jnp.where(l == mx, eid, e), axis=0, keepdims=True)
        vals.append(mx)
        idxs.append(ix)
        l = jnp.where(eid == ix, -jnp.inf, l)
    ex = [jnp.exp(v - vals[0]) for v in vals]
    tot = ex[0] + ex[1] + ex[2] + ex[3]
    for kk in range(TOP_K):
        ti_ref[kk:kk + 1, :] = idxs[kk]
        tp_ref[kk:kk + 1, :] = ex[kk] / tot


def _route(lgt, *, tn):
    e, n = lgt.shape
    return pl.pallas_call(
        _route_kernel,
        grid=(n // tn,),
        in_specs=[pl.BlockSpec((e, tn), lambda i: (0, i))],
        out_specs=[pl.BlockSpec((TOP_K, tn), lambda i: (0, i)),
                   pl.BlockSpec((TOP_K, tn), lambda i: (0, i))],
        out_shape=[jax.ShapeDtypeStruct((TOP_K, n), I32),
                   jax.ShapeDtypeStruct((TOP_K, n), F32)],
        compiler_params=_cparams(("parallel",)),
        name="route",
    )(lgt)


def _dispatch_kernel(pos_ref, pad0_ref, padn_ref, nu_ref, up_ref, us_ref, xs_ref, ztile, sem, zsem,
                     *, np_tiles, tr):
    i = pl.program_id(0)
    tm = up_ref.shape[0]
    n_exp = pad0_ref.shape[0]
    n_tiles = xs_ref.shape[0] // tr

    @pl.when(i == 0)
    def _():
        ztile[...] = jnp.zeros_like(ztile)

        def zero_copy(row0, rows):
            return pltpu.make_async_copy(ztile.at[pl.ds(0, rows)], xs_ref.at[pl.ds(row0, rows)], zsem)

        def for_all_zero_copies(fn):
            def per_expert(e, carry):
                a, n = pad0_ref[e], padn_ref[e]
                head = jnp.minimum((-a) & (SUBLANES - 1), n)
                lax.fori_loop(0, head, lambda r, c: (fn(zero_copy(a + r, 1)), c)[1], 0)
                a8, rem = a + head, n - head
                k = SUBLANES
                while k < tr:
                    @pl.when((rem & k) != 0)
                    def _(k=k):
                        fn(zero_copy(pl.multiple_of(a8 + (rem & (k - 1)), SUBLANES), k))
                    k *= 2
                return carry

            lax.fori_loop(0, n_exp, per_expert, 0)
            lax.fori_loop(nu_ref[0], n_tiles,
                          lambda t, c: (fn(zero_copy(pl.multiple_of(t * tr, tr), tr)), c)[1], 0)

        for_all_zero_copies(lambda cp: cp.start())
        for_all_zero_copies(lambda cp: cp.wait())

    def scatter(u_ref):
        def row_copy(r, kk):
            return pltpu.make_async_copy(u_ref.at[pl.ds(r, 1)], xs_ref.at[pl.ds(pos_ref[kk, r], 1)], sem)

        def start(r, carry):
            for kk in range(TOP_K):
                row_copy(r, kk).start(priority=kk % 2)
            return carry

        lax.fori_loop(0, tm, start, 0, unroll=DMA_ISSUE_UNROLL)
        for kk in range(TOP_K):
            pltpu.make_async_copy(u_ref, xs_ref.at[pl.ds(0, tm)], sem).wait()

    @pl.when(i < np_tiles)
    def _():
        scatter(up_ref)

    @pl.when(i >= np_tiles)
    def _():
        scatter(us_ref)


def _dispatch(pos, pads, nused, u2_p, u2_s, n_slots, *, tm, tr):
    pad0, padn = pads
    (n_p, c), n_s = u2_p.shape, u2_s.shape[0]
    npt = n_p // tm
    smem = pl.BlockSpec(memory_space=pltpu.SMEM)
    return pl.pallas_call(
        functools.partial(_dispatch_kernel, np_tiles=npt, tr=tr),
        grid=((n_p + n_s) // tm,),
        in_specs=[
            pl.BlockSpec((TOP_K, tm), lambda i: (0, i), memory_space=pltpu.SMEM),
            smem, smem, smem,
            pl.BlockSpec((tm, c), lambda i: (jnp.minimum(i, npt - 1), 0)),
            pl.BlockSpec((tm, c), lambda i: (jnp.maximum(i - npt, 0), 0)),
        ],
        out_specs=pl.BlockSpec(memory_space=pl.ANY),
        out_shape=jax.ShapeDtypeStruct((n_slots, c), u2_p.dtype),
        scratch_shapes=[pltpu.VMEM((tr, c), u2_p.dtype),
                        pltpu.SemaphoreType.DMA(()), pltpu.SemaphoreType.DMA(())],
        compiler_params=_cparams(("arbitrary",)),
        name="dispatch",
    )(pos, pad0, padn, nused, u2_p, u2_s)


def _stage_weights(first, te_ref, gn_ref, gi_ref, ng_ref, w_hbm, w_stage, sems):
    j, i = pl.program_id(0), pl.program_id(1)
    tw = w_stage[0].shape[2]
    slot = (j * ng_ref[0] + gi_ref[i]) & 1

    def copies(e, col, s):
        cols = pl.ds(pl.multiple_of(col * tw, tw), tw)
        return [pltpu.make_async_copy(w.at[e, :, cols], st.at[s], sems.at[k, s])
                for k, (w, st) in enumerate(zip(w_hbm, w_stage))]

    @pl.when(first)
    def _():
        e = te_ref[i]

        @pl.when(jnp.logical_and(i == 0, j == 0))
        def _():
            for cp in copies(e, j, slot):
                cp.start()

        for cp in copies(e, j, slot):
            cp.wait()

        nxt = gn_ref[i]

        @pl.when(nxt >= 0)
        def _():
            for cp in copies(nxt, j, 1 - slot):
                cp.start()

        @pl.when(jnp.logical_and(nxt < 0, j + 1 < pl.num_programs(0)))
        def _():
            for cp in copies(te_ref[0], j + 1, 1 - slot):
                cp.start()

    return slot


def _for_row_count(nv, out_ref, body):
    tr = out_ref.shape[0]
    quarter = tr // 4
    for m in range(1, 5):
        rows = m * quarter

        @pl.when(jnp.logical_and(nv > rows - quarter, nv <= rows))
        def _(rows=rows):
            body(rows)
            if rows < tr:
                out_ref[rows:, :] = jnp.zeros((tr - rows, out_ref.shape[1]), out_ref.dtype)

    @pl.when(nv <= 0)
    def _():
        out_ref[...] = jnp.zeros_like(out_ref)


def _expert_up_kernel(te_ref, nu_ref, gf_ref, gn_ref, nv_ref, gi_ref, ng_ref, xs_ref, wg_hbm, wu_hbm,
                      bg_ref, bu_ref, h_ref, wg_st, wu_st, sems):
    del nu_ref
    i = pl.program_id(1)
    slot = _stage_weights(gf_ref[i] == 1, te_ref, gn_ref, gi_ref, ng_ref, (wg_hbm, wu_hbm), (wg_st, wu_st), sems)

    def body(rows):
        x = xs_ref[:rows, :].astype(BF16)
        g = jnp.dot(x, wg_st[slot].astype(BF16), preferred_element_type=F32) + bg_ref[0]
        up = jnp.dot(x, wu_st[slot].astype(BF16), preferred_element_type=F32) + bu_ref[0]
        g = jnp.minimum(g, SWIGLU_LIMIT)
        up = jnp.clip(up, -SWIGLU_LIMIT, SWIGLU_LIMIT)
        h_ref[:rows, :] = (g * _sigmoid(SWIGLU_ALPHA * g) * (up + 1.0)).astype(BF16)

    _for_row_count(nv_ref[i], h_ref, body)


def _expert_up(tiles, xs, w_gate, w_up, b_gate, b_up, *, tr, tf):
    p, c = xs.shape
    e, d, f = w_gate.shape
    row = lambda j, i, te_, nu_, *_: (jnp.minimum(i, nu_[0] - 1), 0)
    bmap = lambda j, i, te_, *_: (te_[i], 0, j)
    return pl.pallas_call(
        _expert_up_kernel,
        grid_spec=pltpu.PrefetchScalarGridSpec(
            num_scalar_prefetch=len(tiles),
            grid=(f // tf, p // tr),
            in_specs=[
                pl.BlockSpec((tr, c), row),
                pl.BlockSpec(memory_space=pl.ANY),
                pl.BlockSpec(memory_space=pl.ANY),
                pl.BlockSpec((1, 1, tf), bmap),
                pl.BlockSpec((1, 1, tf), bmap),
            ],
            out_specs=pl.BlockSpec((tr, tf), lambda j, i, *_: (i, j)),
            scratch_shapes=[pltpu.VMEM((2, d, tf), F32), pltpu.VMEM((2, d, tf), F32),
                            pltpu.SemaphoreType.DMA((2, 2))],
        ),
        out_shape=jax.ShapeDtypeStruct((p, f), BF16),
        compiler_params=_cparams(("arbitrary", "arbitrary")),
        name="expert_up",
    )(*tiles, xs, w_gate, w_up, b_gate.reshape(e, 1, f), b_up.reshape(e, 1, f))


def _expert_down_kernel(te_ref, nu_ref, gf_ref, gn_ref, nv_ref, gi_ref, ng_ref, h_ref, wd_hbm, bd_ref, y_ref,
                        wd_st, sems):
    del nu_ref
    i = pl.program_id(1)
    slot = _stage_weights(gf_ref[i] == 1, te_ref, gn_ref, gi_ref, ng_ref, (wd_hbm,), (wd_st,), sems)

    def body(rows):
        y_ref[:rows, :] = (jnp.dot(h_ref[:rows, :], wd_st[slot].astype(BF16), preferred_element_type=F32)
                           + bd_ref[0])

    _for_row_count(nv_ref[i], y_ref, body)


def _expert_down(tiles, hmid, w_down, b_down, *, tr, tn):
    p, f = hmid.shape
    e, _, d = w_down.shape
    return pl.pallas_call(
        _expert_down_kernel,
        grid_spec=pltpu.PrefetchScalarGridSpec(
            num_scalar_prefetch=len(tiles),
            grid=(d // tn, p // tr),
            in_specs=[
                pl.BlockSpec((tr, f), lambda j, i, te_, nu_, *_: (jnp.minimum(i, nu_[0] - 1), 0)),
                pl.BlockSpec(memory_space=pl.ANY),
                pl.BlockSpec((1, 1, tn), lambda j, i, te_, *_: (te_[i], 0, j)),
            ],
            out_specs=pl.BlockSpec((tr, tn), lambda j, i, *_: (i, j)),
            scratch_shapes=[pltpu.VMEM((2, f, tn), F32), pltpu.SemaphoreType.DMA((1, 2))],
        ),
        out_shape=jax.ShapeDtypeStruct((p, d), F32),
        compiler_params=_cparams(("arbitrary", "arbitrary")),
        name="expert_down",
    )(*tiles, hmid, w_down, b_down.reshape(e, 1, d))


def _combine_kernel(pos_ref, h_ref, p_ref, gf_ref, ys_ref, o_ref, buf, sem):
    tm = h_ref.shape[0]

    def row_copy(r, kk):
        return pltpu.make_async_copy(ys_ref.at[pl.ds(pos_ref[kk, r], 1)], buf.at[kk, pl.ds(r, 1)], sem)

    def start(r, carry):
        for kk in range(TOP_K):
            row_copy(r, kk).start(priority=kk % 2)
        return carry

    lax.fori_loop(0, tm, start, 0, unroll=DMA_ISSUE_UNROLL)
    for kk in range(TOP_K):
        pltpu.make_async_copy(ys_ref.at[pl.ds(0, tm)], buf.at[kk], sem).wait()
    p = p_ref[...]
    acc = h_ref[...]
    for kk in range(TOP_K):
        acc = acc + p[:, kk:kk + 1] * buf[kk]
    o_ref[...] = _rms(acc, gf_ref[...])


def _combine(pos, h, probs_t, g_final, ys, *, tm):
    n, d = h.shape
    return pl.pallas_call(
        _combine_kernel,
        grid=(n // tm,),
        in_specs=[
            pl.BlockSpec((TOP_K, tm), lambda i: (0, i), memory_space=pltpu.SMEM),
            pl.BlockSpec((tm, d), lambda i: (i, 0)),
            pl.BlockSpec((tm, TOP_K), lambda i: (i, 0)),
            pl.BlockSpec((1, d), lambda i: (0, 0)),
            pl.BlockSpec(memory_space=pl.ANY),
        ],
        out_specs=pl.BlockSpec((tm, d), lambda i: (i, 0)),
        out_shape=jax.ShapeDtypeStruct((n, d), F32),
        scratch_shapes=[pltpu.VMEM((TOP_K, tm, d), F32), pltpu.SemaphoreType.DMA(())],
        compiler_params=_cparams(("arbitrary",)),
        name="combine",
    )(pos, h, probs_t, g_final, ys)


def _slot_layout(top_i, n_experts, tr, n_tiles):
    onehot = jnp.sum((top_i[:, :, None] == jnp.arange(n_experts, dtype=I32)[None, None, :]).astype(I32), axis=0)
    rank = jnp.cumsum(onehot, axis=0) - onehot
    counts = jnp.sum(onehot, axis=0)
    cpad = ((counts + tr - 1) // tr) * tr
    cend = jnp.cumsum(cpad)
    off = cend - cpad
    pos = jnp.take_along_axis((off[None, :] + rank).T, top_i, axis=0)
    nused = (cend[-1] // tr).astype(I32)
    tile_row = jnp.arange(n_tiles, dtype=I32) * tr
    te = jnp.sum((cend[None, :] <= tile_row[:, None]).astype(I32), axis=1)
    te = jnp.minimum(te, n_experts - 1)
    used = jnp.arange(n_tiles) < nused
    te = jnp.where(used, te, te[nused - 1])
    gfirst = jnp.logical_and(used, jnp.concatenate([jnp.ones((1,), bool), te[1:] != te[:-1]])).astype(I32)
    eid = jnp.arange(n_experts, dtype=I32)
    later = jnp.logical_and(eid[None, :] > eid[:, None], counts[None, :] > 0)
    nxt = jnp.min(jnp.where(later, eid[None, :], n_experts), axis=1)
    gnext = jnp.where(nxt < n_experts, nxt, -1).astype(I32)[te]
    nvalid = jnp.where(used, jnp.clip(counts[te] - (tile_row - off[te]), 0, tr), 0).astype(I32)
    gidx = (jnp.cumsum(gfirst) - 1).astype(I32)
    ngroups = jnp.sum(gfirst).astype(I32).reshape(1)
    pads = ((off + counts).astype(I32), (cpad - counts).astype(I32))
    return pos.astype(I32), pads, (te, nused.reshape(1), gfirst, gnext, nvalid, gidx, ngroups)


def kernel(x_prompt, x_sample, state_ret_S, state_mlstm_C, state_mlstm_n, state_mlstm_m, g_norm1, w_in, b_if,
           g_ret_gn, g_ml_gn, w_ret_br, w_ml_br, w_out, g_norm2, w_router, b_router, w_gate, b_gate, w_up, b_up,
           w_down, b_down, g_final):
    bp, tp, d = x_prompt.shape
    bs, ts, _ = x_sample.shape
    depth = w_in.shape[0]
    n_experts = w_router.shape[-1]
    n_p, n_s = bp * tp, bs * ts
    n = n_p + n_s

    ret_qk, ret_v = N_HEADS * DK_RET, N_HEADS * DV_RET
    ml_qk, ml_v = N_HEADS * DK_ML, N_HEADS * DV_ML
    o_mi = 2 * ret_qk + 2 * ret_v + 2 * ml_qk + 2 * ml_v
    o_ga = o_mi + 2 * N_HEADS
    colblk = {"ga": 0, "gb": 1}
    base = 2 * d
    colblk.update(rq=base // ret_qk, rk=(base + ret_qk) // ret_qk, rv=(base + 2 * ret_qk) // ret_v,
                  rg=(base + 2 * ret_qk + ret_v) // ret_v)
    mbase = base + 2 * ret_qk + 2 * ret_v
    colblk.update(mq=mbase // ml_qk, mk=(mbase + ml_qk) // ml_qk, mv=(mbase + 2 * ml_qk) // ml_v,
                  mo=(mbase + 2 * ml_qk + ml_v) // ml_v)

    assert depth == 1, "one layer: the combine kernel applies the final norm"
    tr = 512
    n_slots = TOP_K * n + n_experts * tr
    n_tiles = n_slots // tr
    tm = 256
    assert n_p % tm == 0 and n_s % tm == 0
    tmd = 2 * tm if (n_p % (2 * tm) == 0 and n_s % (2 * tm) == 0) else tm

    wl_t = w_in[0].T
    wift = wl_t[o_mi:o_ga].astype(BF16)
    wif = jnp.pad(wift.T, ((0, 0), (0, LANES - 2 * N_HEADS)))
    bflat = b_if[0].reshape(1, 2 * N_HEADS)
    bif = jnp.pad(bflat, ((0, 0), (0, LANES - 2 * N_HEADS)))
    bift = bflat.reshape(2 * N_HEADS, 1)
    g1 = g_norm1[0].reshape(1, d)
    g_r = g_ret_gn[0].reshape(1, ret_v)
    g_m = g_ml_gn[0].reshape(1, ml_v)
    wa, wb, wo = w_ret_br[0].astype(BF16), w_ml_br[0].astype(BF16), w_out[0].astype(BF16)
    g2 = g_norm2[0].reshape(1, d)
    wrt = w_router[0].T.astype(BF16)
    brt = b_router[0].reshape(n_experts, 1)

    def mixers(x, n_seq, t_len, L, L_ml, pos0, s0, c0, n0, m0):
        assert t_len % L == 0 and t_len % L_ml == 0
        z, gl, glt = _inproj(x, g1, wl_t, o_mi, o_ga, 2 * d, wif, wift, bif, bift,
                             tm=min(1024, x.shape[0]), tn=1024)
        cos, sin = _rotary_tables(pos0, t_len)
        ya, s_new = _retention(z, 0, n_seq, t_len, L, cos, sin, s0, g_r, _retention_tables(L), colblk)
        yb, c_new, n_new, m_new = _mlstm(z, gl, glt, 0, n_seq, t_len, L_ml, c0, n0, m0, g_m, colblk)
        h, u2, lgt = _merge(x, ya, yb, z, wa, wb, wo, g2, wrt, brt, tm=tm, colblk=colblk)
        return h, u2, lgt, (s_new, c_new, n_new, m_new)

    zeros = lambda *s: jnp.zeros(s, F32)
    h_p, u2_p, lgt_p, st_p = mixers(
        x_prompt.reshape(n_p, d), bp, tp, min(512, tp), min(512, tp), 0,
        zeros(bp, N_HEADS, DK_RET, DV_RET), zeros(bp, N_HEADS, DV_ML, DK_ML), zeros(bp, N_HEADS, DK_ML),
        zeros(bp, N_HEADS))
    h_s, u2_s, lgt_s, st_s = mixers(
        x_sample.reshape(n_s, d), bs, ts, min(64, ts), min(64, ts), PAST_LEN,
        state_ret_S[0].astype(F32), state_mlstm_C[0].astype(F32), state_mlstm_n[0].astype(F32),
        state_mlstm_m[0].astype(F32))

    top_i, top_p = _route(jnp.concatenate([lgt_p, lgt_s], axis=1), tn=min(2048, n_s))
    pos, pads, tiles = _slot_layout(top_i, n_experts, tr, n_tiles)
    probs_t = top_p.T

    xs = _dispatch(pos, pads, tiles[1], u2_p, u2_s, n_slots, tm=tmd, tr=tr)
    hmid = _expert_up(tiles, xs, w_gate[0], w_up[0], b_gate[0], b_up[0], tr=tr, tf=1024)
    ys = _expert_down(tiles, hmid, w_down[0], b_down[0], tr=tr, tn=d)

    gf = g_final.reshape(1, d)
    y_p = _combine(pos[:, :n_p], h_p, probs_t[:n_p], gf, ys, tm=tmd)
    y_s = _combine(pos[:, n_p:], h_s, probs_t[n_p:], gf, ys, tm=tmd)

    return (y_p.reshape(bp, tp, d), y_s.reshape(bs, ts, d),
            st_p[0][None], st_p[1][None], st_p[2][None], st_p[3][None],
            st_s[0][None], st_s[1][None], st_s[2][None], st_s[3][None])
```
